```python
import math
import jax
import jax.numpy as jnp
from jax import lax
import numpy as np

D_MODEL = 1024
BATCH = 8
SEQ = 4096
DEPTH = 2

CTX_LEN = 256
GRID_W = 64
NORM_EPS = 1e-6

HY_CH = 256
HY_ORDER = 2
HY_DIRS = 2
HY_POS_EMB = 33
HY_FILT_HID = 64
HY_FILT_OUT = HY_ORDER * HY_DIRS * HY_CH
HY_FAST_DECAY = 0.3
HY_SLOW_DECAY = 1.5
HY_DECAY_TARGET = 1e-2

ML_HEADS = 4
ML_HD = 64
ML_W = ML_HEADS * ML_HD
ML_CHUNK = 64

DA_HEADS = 4
DA_HD = 64
DA_QK = DA_HEADS * 2 * DA_HD
DA_V = DA_HEADS * 2 * DA_HD
ROPE_THETA = 10000.0
Q_BLOCK = 128

HY_OFF = 0
ML_OFF = HY_OFF + 3 * HY_CH
DA_OFF = ML_OFF + 4 * ML_W + 2 * 2 * ML_HEADS
N_IN = DA_OFF + 2 * DA_QK + DA_V
MIX_W = HY_CH + ML_W + DA_V

MOE_GROUPS = 4
MOE_PER_GROUP = 8
MOE_EXPERTS = MOE_GROUPS * MOE_PER_GROUP
MOE_TOPK = 2
MOE_FF = 512
MOE_BLOCK = 128

kernel_name = 'hybrid_hyena_mlstm_diffattn_hmoe_dit'


def rmsnorm(x, w):
    xf = x.astype(jnp.float32)
    y = xf * lax.rsqrt(jnp.mean(xf * xf, axis=-1, keepdims=True) + NORM_EPS)
    return (y * w.astype(jnp.float32)).astype(x.dtype)


def short_conv(u, w, b):
    L = u.shape[1]
    up = jnp.pad(u, ((0, 0), (1, 1), (0, 0)))
    return up[:, :L] * w[0] + up[:, 1:L + 1] * w[1] + up[:, 2:] * w[2] + b


def hyena_filters(L, p):
    f32 = jnp.float32
    t = jnp.linspace(0.0, 1.0, L, dtype=f32)[:, None]
    bands = (HY_POS_EMB - 1) // 2
    w = (2.0 * math.pi / L) * jnp.arange(L, dtype=f32)[:, None]
    f = jnp.linspace(1e-4, bands - 1, bands, dtype=f32)[None, :]
    z = jnp.concatenate([t, jnp.cos(f * w), -jnp.sin(f * w)], axis=-1)
    freq = p['hy_sin_freq']
    h = jnp.sin(freq * (z @ p['hy_filt_w1'] + p['hy_filt_b1']))
    h = jnp.sin(freq * (h @ p['hy_filt_w2'] + p['hy_filt_b2']))
    h = (h @ p['hy_filt_w3'] + p['hy_filt_b3']).astype(f32).reshape(L, HY_ORDER, HY_DIRS, HY_CH)
    deltas = jnp.abs(jnp.linspace(math.log(HY_DECAY_TARGET) / HY_SLOW_DECAY,
                                  math.log(HY_DECAY_TARGET) / HY_FAST_DECAY, HY_CH, dtype=f32))
    h = h * jnp.exp(-t[:, :, None, None] * deltas)
    k_fwd, k_bwd = h[:, :, 0], h[:, :, 1]
    k = jnp.concatenate([k_fwd, jnp.zeros_like(k_fwd[:1]), k_bwd[:0:-1]], axis=0)
    return k / jnp.sum(jnp.abs(k), axis=0, keepdims=True)


def fft_long_conv(u, k, d):
    L = u.shape[1]
    y = jnp.fft.irfft(jnp.fft.rfft(u, n=2 * L, axis=1) * jnp.fft.rfft(k, axis=0)[None], n=2 * L, axis=1)[:, :L]
    return y + u * d


def hyena_mix(u, p):
    L = u.shape[1]
    u = short_conv(u, p['hy_conv_w'], p['hy_conv_b']).astype(jnp.float32)
    v, x1, x2 = jnp.split(u, 3, axis=-1)
    k = hyena_filters(L, p)
    z = v
    for o, gate in enumerate((x1, x2)):
        z = gate * fft_long_conv(z, k[:, o], p['hy_bias_d'][o].astype(jnp.float32))
    return z


def to_chunks(a):
    B, L, H = a.shape[:3]
    a = a.reshape((B, L // ML_CHUNK, ML_CHUNK, H) + a.shape[3:])
    return jnp.moveaxis(a, 3, 1)


def from_chunks(a):
    a = jnp.moveaxis(a, 1, 3)
    return a.reshape((a.shape[0], a.shape[1] * a.shape[2]) + a.shape[3:])


def mlstm_states(k, v, ig, lf, state0):
    b = jnp.cumsum(lf, axis=-1)
    a = b[..., -1:] - b + ig
    m_loc = jnp.max(a, axis=-1)
    wv = jnp.exp(a - m_loc[..., None])[..., None] * v
    C_loc = jnp.einsum('bhnli,bhnlj->bhnij', wv, k)
    n_loc = jnp.einsum('bhnl,bhnlj->bhnj', jnp.exp(a - m_loc[..., None]), k)

    def step(carry, xs):
        C, n, m = carry
        bL, Cl, nl, ml = xs
        m_new = jnp.maximum(bL + m, ml)
        s_old = jnp.exp(bL + m - m_new)
        s_loc = jnp.exp(ml - m_new)
        new = (s_old[..., None, None] * C + s_loc[..., None, None] * Cl,
               s_old[..., None] * n + s_loc[..., None] * nl, m_new)
        return new, carry

    xs = tuple(jnp.moveaxis(t, 2, 0) for t in (b[..., -1], C_loc, n_loc, m_loc))
    final, entering = lax.scan(step, state0, xs)
    entering = tuple(jnp.moveaxis(t, 0, 2) for t in entering)
    return b, entering, final


def mlstm_out(q, k, v, ig, b, entering):
    C0, n0, m0 = entering
    Lc = q.shape[3]
    lower = jnp.tril(jnp.ones((Lc, Lc), bool))
    D = jnp.where(lower, b[..., :, None] - b[..., None, :] + ig[..., None, :], -jnp.inf)
    inter = b + m0[..., None]
    m = jnp.maximum(inter, jnp.max(D, axis=-1))
    P = jnp.exp(D - m[..., None]) * jnp.einsum('bhnti,bhnsi->bhnts', q, k)
    w_inter = jnp.exp(inter - m)
    num = jnp.einsum('bhnts,bhnsj->bhntj', P, v) + w_inter[..., None] * jnp.einsum('bhnij,bhntj->bhnti', C0, q)
    den = jnp.sum(P, axis=-1) + w_inter * jnp.einsum('bhnj,bhntj->bhnt', n0, q)
    return num / jnp.maximum(jnp.abs(den), jnp.exp(-m))[..., None]


def mlstm_direction(q, k, v, ig, fg, state0, reverse, need_out):
    if reverse:
        q, k, v, ig, fg = (a[:, ::-1] for a in (q, k, v, ig, fg))
    lf = to_chunks(jax.nn.log_sigmoid(fg))
    q, k, v, ig = (to_chunks(a) for a in (q, k, v, ig))
    b, entering, final = mlstm_states(k, v, ig, lf, state0)
    if not need_out:
        return None, final
    h = from_chunks(mlstm_out(q, k, v, ig, b, entering))
    return (h[:, ::-1] if reverse else h), final


def mlstm_prep(u, p):
    B, L, _ = u.shape
    f32 = jnp.float32
    qk = jax.nn.silu(short_conv(u[..., :2 * ML_W], p['ml_conv_w'], p['ml_conv_b'])).astype(f32)
    q = qk[..., :ML_W].reshape(B, L, ML_HEADS, ML_HD)
    k = qk[..., ML_W:].reshape(B, L, ML_HEADS, ML_HD) * (ML_HD ** -0.5)
    v = u[..., 2 * ML_W:3 * ML_W].astype(f32).reshape(B, L, ML_HEADS, ML_HD)
    o = u[..., 3 * ML_W:4 * ML_W]
    g = u[..., 4 * ML_W:].astype(f32).reshape(B, L, 2, 2, ML_HEADS)
    return q, k, v, o, g


def mlstm_finish(h, o, w):
    B, L = h.shape[:2]
    hn = rmsnorm(h, w.reshape(ML_HEADS, ML_HD)).reshape(B, L, ML_W)
    return jax.nn.sigmoid(o.astype(jnp.float32)) * hn


def mlstm_mix(u_lat, u_ctx, p, ctx_out):
    ql, kl, vl, ol, gl = mlstm_prep(u_lat, p)
    qc, kc, vc, oc, gc = mlstm_prep(u_ctx, p)
    B = u_lat.shape[0]
    f32 = jnp.float32
    state0 = (jnp.zeros((B, ML_HEADS, ML_HD, ML_HD), f32), jnp.zeros((B, ML_HEADS, ML_HD), f32),
              jnp.zeros((B, ML_HEADS), f32))
    h_lat, h_ctx = [], []
    for d in range(2):
        hc_d, ctx_final = mlstm_direction(qc, kc, vc, gc[..., d, 0, :], gc[..., d, 1, :], state0, d == 1, ctx_out)
        hl_d, _ = mlstm_direction(ql, kl, vl, gl[..., d, 0, :], gl[..., d, 1, :], ctx_final, d == 1, True)
        h_lat.append(hl_d)
        h_ctx.append(hc_d)
    y_lat = mlstm_finish(h_lat[0] + h_lat[1], ol, p['ml_norm_w'])
    y_ctx = mlstm_finish(h_ctx[0] + h_ctx[1], oc, p['ml_norm_w']) if ctx_out else None
    return y_lat, y_ctx


def rope_tables(row, col):
    half = DA_HD // 2
    inv = ROPE_THETA ** (-jnp.arange(0, half, 2, dtype=jnp.float32) / half)
    ang = jnp.stack([row, col], axis=-1)[:, :, None] * inv
    ang = jnp.stack([ang, ang], axis=-2).reshape(-1, DA_HD)
    return jnp.cos(ang), jnp.sin(ang)


def axial_rope(x, cos, sin):
    xs = x.reshape(x.shape[:-1] + (2, 2, DA_HD // 4))
    rot = jnp.stack([-xs[..., 1, :], xs[..., 0, :]], axis=-2).reshape(x.shape)
    cos = cos[None, :, None, None, :]
    sin = sin[None, :, None, None, :]
    return (x * cos + rot * sin).astype(x.dtype)


def diff_core(q, k, v, lam):
    s = jnp.einsum('bqhmd,bkhmd->bhmqk', q, k, preferred_element_type=jnp.float32) * (DA_HD ** -0.5)
    pr = jax.nn.softmax(s, axis=-1)
    a = pr[:, :, 0] - lam * pr[:, :, 1]
    return jnp.einsum('bhqk,bkhe->bqhe', a.astype(v.dtype), v)


def diff_attention(u_lat, u_ctx, p, layer_idx, cos, sin, ctx_out):
    lam_init = 0.8 - 0.6 * math.exp(-0.3 * layer_idx)
    lp = p['da_lambda'].astype(jnp.float32)
    lam = jnp.exp(jnp.sum(lp[0] * lp[1])) - jnp.exp(jnp.sum(lp[2] * lp[3])) + lam_init

    def split(u):
        B, L, _ = u.shape
        q = u[..., :DA_QK].reshape(B, L, DA_HEADS, 2, DA_HD)
        k = u[..., DA_QK:2 * DA_QK].reshape(B, L, DA_HEADS, 2, DA_HD)
        v = u[..., 2 * DA_QK:].reshape(B, L, DA_HEADS, 2 * DA_HD)
        return q, k, v

    ql, kl, vl = split(u_lat)
    qc, kc, vc = split(u_ctx)
    ql = axial_rope(ql, cos, sin)
    kl = axial_rope(kl, cos, sin)
    k_all = jnp.concatenate([kl, kc], axis=1)
    v_all = jnp.concatenate([vl, vc], axis=1)
    B, L = ql.shape[:2]
    nb = L // Q_BLOCK
    qb = jnp.moveaxis(ql.reshape((B, nb, Q_BLOCK) + ql.shape[2:]), 1, 0)
    ob = lax.map(lambda qi: diff_core(qi, k_all, v_all, lam), qb)
    o_lat = jnp.moveaxis(ob, 0, 1).reshape(B, L, DA_HEADS, 2 * DA_HD)

    def finish(o):
        return (rmsnorm(o, p['da_subln_w']) * (1.0 - lam_init)).reshape(o.shape[0], o.shape[1], DA_V)

    y_lat = finish(o_lat)
    y_ctx = finish(diff_core(qc, kc, vc, lam)) if ctx_out else None
    return y_lat, y_ctx


def hier_moe(h, p):
    N, D = h.shape
    f32 = jnp.float32
    g_logits = (h @ p['moe_wg'] + p['moe_bg']).astype(f32)
    g_idx = jnp.argmax(g_logits, axis=-1)
    g_w = jnp.take_along_axis(jax.nn.softmax(g_logits, axis=-1), g_idx[:, None], axis=-1)
    e_logits = (h @ p['moe_we'] + p['moe_be']).astype(f32).reshape(N, MOE_GROUPS, MOE_PER_GROUP)
    e_logits = jnp.take_along_axis(e_logits, g_idx[:, None, None], axis=1)[:, 0]
    top_v, top_i = lax.top_k(e_logits, MOE_TOPK)
    gate = jax.nn.softmax(top_v, axis=-1) * g_w
    expert = g_idx[:, None] * MOE_PER_GROUP + top_i
    A = N * MOE_TOPK
    flat_e = expert.reshape(-1).astype(jnp.int32)
    flat_t = jnp.repeat(jnp.arange(N, dtype=jnp.int32), MOE_TOPK)
    flat_g = gate.reshape(-1)
    order = jnp.argsort(flat_e)
    se, st, sg = flat_e[order], flat_t[order], flat_g[order]
    counts = jnp.bincount(flat_e, length=MOE_EXPERTS)
    starts = jnp.cumsum(counts) - counts
    pcounts = (counts + MOE_BLOCK - 1) // MOE_BLOCK * MOE_BLOCK
    pends = jnp.cumsum(pcounts)
    dest = (pends - pcounts)[se] + jnp.arange(A, dtype=jnp.int32) - starts[se]
    P = -(-A // MOE_BLOCK) * MOE_BLOCK + MOE_EXPERTS * MOE_BLOCK
    nblk = P // MOE_BLOCK
    slot_tok = jnp.full((P,), N, jnp.int32).at[dest].set(st)
    slot_g = jnp.zeros((P,), f32).at[dest].set(sg)
    blk_e = jnp.minimum(jnp.searchsorted(pends, jnp.arange(nblk) * MOE_BLOCK, side='right'), MOE_EXPERTS - 1)
    xb = jnp.concatenate([h, jnp.zeros((1, D), h.dtype)], axis=0)[slot_tok].reshape(nblk, MOE_BLOCK, D)

    def expert_block(args):
        xi, e = args
        return (jax.nn.silu(xi @ p['moe_w1'][e]) * (xi @ p['moe_w3'][e])) @ p['moe_w2'][e]

    yb = lax.map(expert_block, (xb, blk_e)).reshape(P, D)
    out = jax.ops.segment_sum(yb.astype(f32) * slot_g[:, None], slot_tok, num_segments=N + 1)
    return out[:N].astype(h.dtype)


def trunk_layer(x, xc, c_act, cc_act, p, cos, sin, layer_idx, last):
    B, L, D = x.shape
    mod = (c_act @ p['ada_w'] + p['ada_b'])[:, None, :]
    modc = cc_act @ p['ada_w'] + p['ada_b']
    sh1, sc1, g1, sh2, sc2, g2 = jnp.split(mod, 6, axis=-1)
    csh1, csc1, cg1, csh2, csc2, cg2 = jnp.split(modc, 6, axis=-1)

    h = rmsnorm(x, p['norm1_w']) * (1.0 + sc1) + sh1
    hc = rmsnorm(xc, p['norm1_w']) * (1.0 + csc1) + csh1
    u = h @ p['w_in'] + p['b_in']
    c0 = ML_OFF if last else 0
    uc = hc @ p['w_in'][:, c0:] + p['b_in'][c0:]

    y_hy = hyena_mix(u[..., HY_OFF:ML_OFF], p).astype(x.dtype)
    y_ml, yc_ml = mlstm_mix(u[..., ML_OFF:DA_OFF], uc[..., ML_OFF - c0:DA_OFF - c0], p, not last)
    y_da, yc_da = diff_attention(u[..., DA_OFF:], uc[..., DA_OFF - c0:], p, layer_idx, cos, sin, not last)
    y = jnp.concatenate([y_hy, y_ml.astype(x.dtype), y_da.astype(x.dtype)], axis=-1) @ p['w_out']
    x = x + g1 * y
    if not last:
        yc_hy = hyena_mix(uc[..., HY_OFF:ML_OFF], p).astype(xc.dtype)
        yc = jnp.concatenate([yc_hy, yc_ml.astype(xc.dtype), yc_da.astype(xc.dtype)], axis=-1) @ p['w_out']
        xc = xc + cg1 * yc

    h2 = rmsnorm(x, p['norm2_w']) * (1.0 + sc2) + sh2
    if last:
        f = hier_moe(h2.reshape(B * L, D), p)
        return x + g2 * f.reshape(B, L, D), xc
    hc2 = rmsnorm(xc, p['norm2_w']) * (1.0 + csc2) + csh2
    f = hier_moe(jnp.concatenate([h2.reshape(B * L, D), hc2.reshape(-1, D)], axis=0), p)
    x = x + g2 * f[:B * L].reshape(B, L, D)
    xc = xc + cg2 * f[B * L:].reshape(xc.shape)
    return x, xc


def setup_inputs(seed: int = 0) -> dict:
    key = jax.random.key(seed)
    keys = jax.random.split(key, 34)
    f32 = jnp.float32

    def nrm(i, shape, scale):
        return scale * jax.random.normal(keys[i], shape, f32)

    D = D_MODEL
    fg_idx = np.array([ML_OFF + 4 * ML_W + d * 2 * ML_HEADS + ML_HEADS + hh for d in range(2) for hh in range(ML_HEADS)])
    fg_bias = jnp.asarray(np.tile(np.linspace(3.0, 6.0, ML_HEADS, dtype=np.float32), 2))
    return {
        'x': nrm(0, (BATCH, SEQ, D), 1.0),
        'c': nrm(1, (BATCH, D), 1.0),
        'ctx': nrm(2, (BATCH, CTX_LEN, D), 1.0),
        'c_ctx': nrm(3, (D,), 1.0),
        'ada_w': nrm(4, (DEPTH, D, 6 * D), 0.5 * D ** -0.5),
        'ada_b': nrm(5, (DEPTH, 6 * D), 0.02),
        'norm1_w': 1.0 + nrm(6, (DEPTH, D), 0.02),
        'norm2_w': 1.0 + nrm(7, (DEPTH, D), 0.02),
        'w_in': nrm(8, (DEPTH, D, N_IN), D ** -0.5),
        'b_in': nrm(9, (DEPTH, N_IN), 0.02).at[:, fg_idx].add(fg_bias),
        'w_out': nrm(10, (DEPTH, MIX_W, D), MIX_W ** -0.5),
        'hy_conv_w': nrm(11, (DEPTH, 3, 3 * HY_CH), 3 ** -0.5),
        'hy_conv_b': nrm(12, (DEPTH, 3 * HY_CH), 0.02),
        'hy_filt_w1': nrm(13, (DEPTH, HY_POS_EMB, HY_FILT_HID), HY_POS_EMB ** -0.5),
        'hy_filt_b1': nrm(14, (DEPTH, HY_FILT_HID), 0.02),
        'hy_filt_w2': nrm(15, (DEPTH, HY_FILT_HID, HY_FILT_HID), HY_FILT_HID ** -0.5),
        'hy_filt_b2': nrm(16, (DEPTH, HY_FILT_HID), 0.02),
        'hy_filt_w3': nrm(17, (DEPTH, HY_FILT_HID, HY_FILT_OUT), HY_FILT_HID ** -0.5),
        'hy_filt_b3': nrm(18, (DEPTH, HY_FILT_OUT), 0.02),
        'hy_sin_freq': 1.0 + nrm(19, (DEPTH, HY_FILT_HID), 0.02),
        'hy_bias_d': nrm(20, (DEPTH, HY_ORDER, HY_CH), 1.0),
        'ml_conv_w': nrm(21, (DEPTH, 3, 2 * ML_W), 3 ** -0.5),
        'ml_conv_b': nrm(22, (DEPTH, 2 * ML_W), 0.02),
        'ml_norm_w': 1.0 + nrm(23, (DEPTH, ML_W), 0.02),
        'da_lambda': nrm(24, (DEPTH, 4, DA_HD), 0.1),
        'da_subln_w': 1.0 + nrm(25, (DEPTH, 2 * DA_HD), 0.02),
        'moe_wg': nrm(26, (DEPTH, D, MOE_GROUPS), D ** -0.5),
        'moe_bg': nrm(27, (DEPTH, MOE_GROUPS), 0.01),
        'moe_we': nrm(28, (DEPTH, D, MOE_EXPERTS), D ** -0.5),
        'moe_be': nrm(29, (DEPTH, MOE_EXPERTS), 0.01),
        'moe_w1': nrm(30, (DEPTH, MOE_EXPERTS, D, MOE_FF), D ** -0.5),
        'moe_w3': nrm(31, (DEPTH, MOE_EXPERTS, D, MOE_FF), D ** -0.5),
        'moe_w2': nrm(32, (DEPTH, MOE_EXPERTS, MOE_FF, D), MOE_FF ** -0.5),
        'final_norm_w': 1.0 + nrm(33, (D,), 0.02),
    }


def reference(x, c, ctx, c_ctx, ada_w, ada_b, norm1_w, norm2_w, w_in, b_in, w_out,
              hy_conv_w, hy_conv_b, hy_filt_w1, hy_filt_b1, hy_filt_w2, hy_filt_b2, hy_filt_w3, hy_filt_b3,
              hy_sin_freq, hy_bias_d, ml_conv_w, ml_conv_b, ml_norm_w, da_lambda, da_subln_w,
              moe_wg, moe_bg, moe_we, moe_be, moe_w1, moe_w3, moe_w2, final_norm_w):
    B, L, D = x.shape
    ROWS = L // GRID_W
    row = jnp.repeat(jnp.arange(ROWS, dtype=jnp.float32), GRID_W)
    col = jnp.tile(jnp.arange(GRID_W, dtype=jnp.float32), ROWS)
    cos, sin = rope_tables(row, col)
    c_act = jax.nn.silu(c)
    cc_act = jax.nn.silu(c_ctx)
    xc = ctx
    for l in range(DEPTH):
        p = {
            'ada_w': ada_w[l], 'ada_b': ada_b[l], 'norm1_w': norm1_w[l], 'norm2_w': norm2_w[l],
            'w_in': w_in[l], 'b_in': b_in[l], 'w_out': w_out[l],
            'hy_conv_w': hy_conv_w[l], 'hy_conv_b': hy_conv_b[l],
            'hy_filt_w1': hy_filt_w1[l], 'hy_filt_b1': hy_filt_b1[l],
            'hy_filt_w2': hy_filt_w2[l], 'hy_filt_b2': hy_filt_b2[l],
            'hy_filt_w3': hy_filt_w3[l], 'hy_filt_b3': hy_filt_b3[l],
            'hy_sin_freq': hy_sin_freq[l], 'hy_bias_d': hy_bias_d[l],
            'ml_conv_w': ml_conv_w[l], 'ml_conv_b': ml_conv_b[l], 'ml_norm_w': ml_norm_w[l],
            'da_lambda': da_lambda[l], 'da_subln_w': da_subln_w[l],
            'moe_wg': moe_wg[l], 'moe_bg': moe_bg[l], 'moe_we': moe_we[l], 'moe_be': moe_be[l],
            'moe_w1': moe_w1[l], 'moe_w3': moe_w3[l], 'moe_w2': moe_w2[l],
        }
        x, xc = trunk_layer(x, xc, c_act, cc_act, p, cos, sin, l, l == DEPTH - 1)
    return rmsnorm(x, final_norm_w)
```

```python
import functools
import math

import numpy as np
import jax
import jax.numpy as jnp
from jax import lax
from jax.experimental import pallas as pl
from jax.experimental.pallas import tpu as pltpu

F32 = jnp.float32
BF16 = jnp.bfloat16

NORM_EPS = 1e-6
GRID_W = 64

HY_CH = 256
HY_POS_EMB = 33
HY_FAST_DECAY = 0.3
HY_SLOW_DECAY = 1.5
HY_DECAY_TARGET = 1e-2

ML_HEADS = 4
ML_HD = 64
ML_W = ML_HEADS * ML_HD

DA_HEADS = 4
DA_HD = 64
DA_QK = DA_HEADS * 2 * DA_HD
DA_V = DA_HEADS * 2 * DA_HD
ROPE_THETA = 10000.0

HY_OFF = 0
ML_OFF = HY_OFF + 3 * HY_CH
GATE_OFF = ML_OFF + 4 * ML_W
DA_OFF = GATE_OFF + 2 * 2 * ML_HEADS

MOE_GROUPS = 4
MOE_PER_GROUP = 8
MOE_EXPERTS = MOE_GROUPS * MOE_PER_GROUP

LANES = 128
SUBLANES = 8
VMEM_BYTES_V7X = 64 * 1024 * 1024
VMEM_LIMIT = VMEM_BYTES_V7X * 7 // 8

ROW_TILE = 256
ML_CHUNK = 128
ATT_Q_TILE = 256
MOE_ROWS = 256


def _cparams(*sem):
    return pltpu.CompilerParams(dimension_semantics=tuple(sem), vmem_limit_bytes=VMEM_LIMIT)


def _dot(a, b):
    return jnp.dot(a.astype(BF16), b.astype(BF16), preferred_element_type=F32)


def _split(a):
    hi = a.astype(BF16)
    lo = (a - hi.astype(F32)).astype(BF16)
    return hi, lo


def _dot3(a, b):
    ah, al = _split(a)
    bh, bl = _split(b)
    d = functools.partial(jnp.dot, preferred_element_type=F32)
    return d(ah, bh) + (d(ah, bl) + d(al, bh))


def _dot2_exact_rhs(a, b_exact):
    ah, al = _split(a)
    d = functools.partial(jnp.dot, preferred_element_type=F32)
    return d(ah, b_exact) + d(al, b_exact)


def _ada_body(a_ref, w_ref, b_ref, o_ref):
    a = a_ref[...]
    a = a * jax.nn.sigmoid(a)
    o_ref[...] = _dot3(a, w_ref[...]) + b_ref[...]


def _ada(cond, w, b):
    R, D = cond.shape
    N = w.shape[1]
    tn = 1536
    return pl.pallas_call(
        _ada_body,
        grid=(N // tn,),
        in_specs=[pl.BlockSpec((R, D), lambda j: (0, 0)),
                  pl.BlockSpec((D, tn), lambda j: (0, j)),
                  pl.BlockSpec((1, tn), lambda j: (0, j))],
        out_specs=pl.BlockSpec((R, tn), lambda j: (0, j)),
        out_shape=jax.ShapeDtypeStruct((R, N), F32),
        compiler_params=_cparams("parallel"),
    )(cond, w, b[None])


def _mod_index(i, n_lat_tiles, tiles_per_batch, n_batch):
    return jnp.where(i < n_lat_tiles, i // tiles_per_batch, n_batch)


def _norm_mod(x, nw, shift, scale):
    ms = jnp.mean(x * x, axis=-1, keepdims=True)
    return (x * lax.rsqrt(ms + NORM_EPS) * nw) * (1.0 + scale) + shift


def _proj_in_body(x_ref, mod_ref, nw_ref, why_ref, wml_ref, wda_ref, wgt_ref, bhy_ref, bml_ref, bda_ref, bgt_ref,
                  ohy_ref, oml_ref, oda_ref, ogt_ref):
    h = _norm_mod(x_ref[...], nw_ref[...], mod_ref[0, 0:1, :], mod_ref[0, 1:2, :])
    hb = h.astype(BF16)
    d = functools.partial(jnp.dot, preferred_element_type=F32)
    ohy_ref[...] = d(hb, why_ref[...]) + bhy_ref[...]
    oml_ref[...] = d(hb, wml_ref[...]) + bml_ref[...]
    oda_ref[...] = d(hb, wda_ref[...]) + bda_ref[...]
    ogt_ref[...] = lax.dot_general(wgt_ref[...], hb, (((1,), (1,)), ((), ())),
                                   preferred_element_type=F32) + bgt_ref[...]


def _proj_in(xa, mod, nw, w_in, b_in, n_lat_tiles, tiles_per_batch, n_batch):
    M, D = xa.shape
    tm = ROW_TILE
    wb = w_in.astype(BF16)
    why, wml, wg, wda = wb[:, :ML_OFF], wb[:, ML_OFF:GATE_OFF], wb[:, GATE_OFF:DA_OFF], wb[:, DA_OFF:]
    bhy, bml, bg, bda = b_in[:ML_OFF], b_in[ML_OFF:GATE_OFF], b_in[GATE_OFF:DA_OFF], b_in[DA_OFF:]
    ng = wg.shape[1]
    full = lambda a: pl.BlockSpec(a.shape, lambda i: (0,) * a.ndim)
    args = (xa, mod, nw[None], why, wml, wda, wg.T, bhy[None], bml[None], bda[None], bg[:, None])
    in_specs = [pl.BlockSpec((tm, D), lambda i: (i, 0)),
                pl.BlockSpec((1,) + mod.shape[1:],
                             lambda i: (_mod_index(i, n_lat_tiles, tiles_per_batch, n_batch), 0, 0))]
    in_specs += [full(a) for a in args[2:]]
    widths = (why.shape[1], wml.shape[1], wda.shape[1])
    out_specs = [pl.BlockSpec((tm, wd), lambda i: (i, 0)) for wd in widths]
    out_specs.append(pl.BlockSpec((ng, tm), lambda i: (0, i)))
    out_shape = [jax.ShapeDtypeStruct((M, wd), F32) for wd in widths]
    out_shape.append(jax.ShapeDtypeStruct((ng, M), F32))
    return pl.pallas_call(
        _proj_in_body, grid=(M // tm,), in_specs=in_specs, out_specs=out_specs, out_shape=out_shape,
        compiler_params=_cparams("parallel"),
    )(*args)


def _conv3(x, prev_row, next_row, w, b, at_start, at_end):
    T = x.shape[0]
    rows = lax.broadcasted_iota(jnp.int32, x.shape, 0)
    prev_row = jnp.where(at_start, 0.0, prev_row)
    next_row = jnp.where(at_end, 0.0, next_row)
    up = jnp.where(rows == 0, prev_row, pltpu.roll(x, 1, 0))
    dn = jnp.where(rows == T - 1, next_row, pltpu.roll(x, T - 1, 0))
    return up * w[0:1, :] + x * w[1:2, :] + dn * w[2:3, :] + b


def _seq_edges(i, regions):
    at_start = jnp.bool_(False)
    at_end = jnp.bool_(False)
    pos = 0
    for (_, n, tps) in regions:
        inside = (i >= pos) & (i < pos + n)
        r = (i - pos) % tps
        at_start = at_start | (inside & (r == 0))
        at_end = at_end | (inside & (r == tps - 1))
        pos += n
    return at_start, at_end


def _seq_tile(i, regions):
    pos = 0
    t = jnp.int32(0)
    for (first, n, _) in regions:
        t = jnp.where((i >= pos) & (i < pos + n), first + (i - pos), t)
        pos += n
    return t


def _halo_specs(tm, width, colblk, regions, n_rows):
    per = tm // SUBLANES
    last8 = n_rows // SUBLANES - 1
    cur = pl.BlockSpec((tm, width), lambda i: (_seq_tile(i, regions), colblk))
    prv = pl.BlockSpec((SUBLANES, width), lambda i: (jnp.maximum(_seq_tile(i, regions) * per - 1, 0), colblk))
    nxt = pl.BlockSpec((SUBLANES, width),
                       lambda i: (jnp.minimum((_seq_tile(i, regions) + 1) * per, last8), colblk))
    return [cur, prv, nxt]


def _hy_prep_body(x_ref, p_ref, n_ref, w_ref, b_ref, v_ref, x1_ref, x2_ref, *, regions):
    at_start, at_end = _seq_edges(pl.program_id(0), regions)
    y = _conv3(x_ref[...], p_ref[SUBLANES - 1:SUBLANES, :], n_ref[0:1, :], w_ref[...], b_ref[...], at_start, at_end)
    v_ref[...] = y[:, :HY_CH]
    x1_ref[...] = y[:, HY_CH:2 * HY_CH]
    x2_ref[...] = y[:, 2 * HY_CH:]


def _hy_prep(u_hy, w, b, first_tile, n_tiles, tiles_per_seq):
    M, W = u_hy.shape
    tm = ROW_TILE
    regions = ((first_tile, n_tiles, tiles_per_seq),)
    out = jax.ShapeDtypeStruct((n_tiles * tm, HY_CH), F32)
    return pl.pallas_call(
        functools.partial(_hy_prep_body, regions=regions),
        grid=(n_tiles,),
        in_specs=_halo_specs(tm, W, 0, regions, M) + [pl.BlockSpec((3, W), lambda i: (0, 0)),
                                                      pl.BlockSpec((1, W), lambda i: (0, 0))],
        out_specs=[pl.BlockSpec((tm, HY_CH), lambda i: (i, 0))] * 3,
        out_shape=[out] * 3,
        compiler_params=_cparams("parallel"),
    )(u_hy, u_hy, u_hy, w, b[None])


def _filt_body(z_ref, w1_ref, b1_ref, w2_ref, b2_ref, fr_ref, w3_ref, b3_ref, env_ref, o_ref):
    f = fr_ref[...]
    h = jnp.sin(f * (_dot3(z_ref[...], w1_ref[...]) + b1_ref[...]))
    h = jnp.sin(f * (_dot3(h, w2_ref[...]) + b2_ref[...]))
    k = _dot3(h, w3_ref[...]) + b3_ref[...]
    e = env_ref[...]
    kf = k[:, :HY_CH] * e
    kb = k[:, HY_CH:] * e
    s = (jnp.sum(jnp.abs(kf), axis=0, keepdims=True) + jnp.sum(jnp.abs(kb), axis=0, keepdims=True)
         - jnp.abs(kb[0:1, :]))
    o_ref[:, :HY_CH] = kf / s
    o_ref[:, HY_CH:] = kb / s


def _hy_filters(L, p):
    t = jnp.linspace(0.0, 1.0, L, dtype=F32)[:, None]
    bands = (HY_POS_EMB - 1) // 2
    w = (2.0 * math.pi / L) * jnp.arange(L, dtype=F32)[:, None]
    f = jnp.linspace(1e-4, bands - 1, bands, dtype=F32)[None, :]
    z = jnp.concatenate([t, jnp.cos(f * w), -jnp.sin(f * w)], axis=-1)
    zp = jnp.pad(z, ((0, 0), (0, LANES - HY_POS_EMB)))
    w1 = jnp.pad(p['hy_filt_w1'], ((0, LANES - HY_POS_EMB), (0, 0)))
    deltas = jnp.abs(jnp.linspace(math.log(HY_DECAY_TARGET) / HY_SLOW_DECAY,
                                  math.log(HY_DECAY_TARGET) / HY_FAST_DECAY, HY_CH, dtype=F32))
    env = jnp.exp(-t * deltas)
    hid = w1.shape[1]
    nout = p['hy_filt_w3'].shape[1]
    n_order = nout // (2 * HY_CH)
    c0 = lambda a: pl.BlockSpec(a.shape, lambda o: (0,) * a.ndim)
    args = (zp, w1, p['hy_filt_b1'][None], p['hy_filt_w2'], p['hy_filt_b2'][None], p['hy_sin_freq'][None],
            p['hy_filt_w3'], p['hy_filt_b3'][None], env)
    in_specs = [c0(a) for a in args[:6]]
    in_specs += [pl.BlockSpec((hid, 2 * HY_CH), lambda o: (0, o)), pl.BlockSpec((1, 2 * HY_CH), lambda o: (0, o)),
                 c0(env)]
    return pl.pallas_call(
        _filt_body, grid=(n_order,), in_specs=in_specs,
        out_specs=pl.BlockSpec((L, 2 * HY_CH), lambda o: (0, o)),
        out_shape=jax.ShapeDtypeStruct((L, nout), F32),
        compiler_params=_cparams("parallel"),
    )(*args)


def _fft_plan(L):
    N = 2 * L
    Bn = 128 if N >= 4096 else 16
    A = N // Bn
    assert A * Bn == N and A % 16 == 0
    return A, Bn


def _dft_small(A, N):
    ka = np.arange(A)[:, None]
    a = np.arange(A)[None, :]
    th = 2.0 * np.pi * ((ka * a) % A) / A
    d1 = np.concatenate([np.cos(th), -np.sin(th)], axis=0)
    d4 = np.concatenate([np.cos(th.T), -np.sin(th.T)], axis=1) / N
    return jnp.asarray(d1, F32), jnp.asarray(d4, F32)


def _dft_mid(A, Bn):
    N = A * Bn
    ka = jnp.arange(A, dtype=jnp.int32)[:, None, None]
    kb = jnp.arange(Bn, dtype=jnp.int32)[None, :, None]
    b = jnp.arange(Bn, dtype=jnp.int32)[None, None, :]
    m = (b * (kb * A + ka)) % N
    ph = m.astype(F32) * (2.0 * math.pi / N)
    c, s = jnp.cos(ph), jnp.sin(ph)
    mf = jnp.concatenate([jnp.concatenate([c, s], axis=2), jnp.concatenate([-s, c], axis=2)], axis=1)
    ct, st = jnp.swapaxes(c, 1, 2), jnp.swapaxes(s, 1, 2)
    mi = jnp.concatenate([jnp.concatenate([ct, -st], axis=2), jnp.concatenate([st, ct], axis=2)], axis=1)
    return mf.astype(BF16), mi.astype(BF16)


def _fft1_body(d_ref, x_ref, o_ref):
    o_ref[0] = jnp.dot(d_ref[...], x_ref[0].astype(BF16), preferred_element_type=F32).astype(BF16)


def _fft1(d1, x3):
    S, Ain, W = x3.shape
    A2 = d1.shape[0]
    tl = min(W, 8192)
    return pl.pallas_call(
        _fft1_body, grid=(S, W // tl),
        in_specs=[pl.BlockSpec((A2, Ain), lambda s, j: (0, 0)), pl.BlockSpec((1, Ain, tl), lambda s, j: (s, 0, j))],
        out_specs=pl.BlockSpec((1, A2, tl), lambda s, j: (s, 0, j)),
        out_shape=jax.ShapeDtypeStruct((S, A2, W), BF16),
        compiler_params=_cparams("parallel", "parallel"),
    )(d1.astype(BF16), x3)


FFT_GROUP = 8


def _fft_spec_body(p_ref, mf_ref, k_ref, *, Bn):
    for j in range(FFT_GROUP):
        pin = jnp.concatenate([p_ref[0, 0, j], p_ref[0, 1, j]], axis=0)
        X = jnp.dot(mf_ref[j], pin, preferred_element_type=F32)
        k_ref[0, j, 0] = X[:Bn]
        k_ref[0, j, 1] = X[Bn:]


def _fft_spec(p5, mf):
    S, _, A, Bn, C = p5.shape
    G = FFT_GROUP
    return pl.pallas_call(
        functools.partial(_fft_spec_body, Bn=Bn), grid=(A // G, S),
        in_specs=[pl.BlockSpec((1, 2, G, Bn, C), lambda g, s: (s, 0, g, 0, 0)),
                  pl.BlockSpec((G, 2 * Bn, 2 * Bn), lambda g, s: (g, 0, 0))],
        out_specs=pl.BlockSpec((1, G, 2, Bn, C), lambda g, s: (s, g, 0, 0, 0)),
        out_shape=jax.ShapeDtypeStruct((S, A, 2, Bn, C), F32),
        compiler_params=_cparams("parallel", "parallel"),
    )(p5, mf)


def _fft_mid_body(p_ref, mf_ref, mi_ref, k_ref, q_ref, *, Bn):
    for j in range(FFT_GROUP):
        pin = jnp.concatenate([p_ref[0, 0, j], p_ref[0, 1, j]], axis=0)
        X = jnp.dot(mf_ref[j], pin, preferred_element_type=F32)
        xr, xi = X[:Bn], X[Bn:]
        kr, ki = k_ref[j, 0], k_ref[j, 1]
        Y = jnp.concatenate([xr * kr - xi * ki, xr * ki + xi * kr], axis=0).astype(BF16)
        Q = jnp.dot(mi_ref[j], Y, preferred_element_type=F32)
        q_ref[0, 0, j] = Q[:Bn].astype(BF16)
        q_ref[0, 1, j] = Q[Bn:].astype(BF16)


def _fft_mid(p5, mf, mi, kspec):
    S, _, A, Bn, C = p5.shape
    G = FFT_GROUP
    blk = pl.BlockSpec((1, 2, G, Bn, C), lambda g, s: (s, 0, g, 0, 0))
    mat = pl.BlockSpec((G, 2 * Bn, 2 * Bn), lambda g, s: (g, 0, 0))
    return pl.pallas_call(
        functools.partial(_fft_mid_body, Bn=Bn), grid=(A // G, S),
        in_specs=[blk, mat, mat, pl.BlockSpec((G, 2, Bn, C), lambda g, s: (g, 0, 0, 0))],
        out_specs=blk,
        out_shape=jax.ShapeDtypeStruct(p5.shape, BF16),
        compiler_params=_cparams("parallel", "parallel"),
    )(p5, mf, mi, kspec)


def _fft4_body(d_ref, q_ref, z_ref, g_ref, dt_ref, o_ref):
    y = jnp.dot(d_ref[...], q_ref[0], preferred_element_type=F32)
    o_ref[0] = g_ref[0] * (y + z_ref[0] * dt_ref[...])


def _fft4(d4, q3, z3, g3, dt):
    S, A2, W = q3.shape
    Ah = d4.shape[0]
    tl = min(W, 8192)
    sig = pl.BlockSpec((1, Ah, tl), lambda s, j: (s, 0, j))
    return pl.pallas_call(
        _fft4_body, grid=(S, W // tl),
        in_specs=[pl.BlockSpec((Ah, A2), lambda s, j: (0, 0)), pl.BlockSpec((1, A2, tl), lambda s, j: (s, 0, j)),
                  sig, sig, pl.BlockSpec((1, tl), lambda s, j: (0, j))],
        out_specs=sig,
        out_shape=jax.ShapeDtypeStruct((S, Ah, W), F32),
        compiler_params=_cparams("parallel", "parallel"),
    )(d4.astype(BF16), q3, z3, g3, dt)


def _hyena(v, x1, x2, n_seq, L, p):
    C = HY_CH
    A, Bn = _fft_plan(L)
    Ah = A // 2
    W = Bn * C
    d1, d4 = _dft_small(A, A * Bn)
    mf, mi = _dft_mid(A, Bn)
    kn = _hy_filters(L, p).reshape(L, -1, 2, C)
    n_order = kn.shape[1]
    zero = jnp.zeros((1, C), F32)
    k2 = jnp.stack([jnp.concatenate([kn[:, o, 0], zero, jnp.flip(kn[1:, o, 1], axis=0)], axis=0)
                    for o in range(n_order)])
    kspec = _fft_spec(_fft1(d1, k2.reshape(n_order, A, W)).reshape(n_order, 2, A, Bn, C), mf)
    z = v.reshape(n_seq, Ah, W)
    for o, gate in enumerate((x1, x2)):
        P = _fft1(d1[:, :Ah], z).reshape(n_seq, 2, A, Bn, C)
        Q = _fft_mid(P, mf, mi, kspec[o]).reshape(n_seq, 2 * A, W)
        dt = jnp.tile(p['hy_bias_d'][o], Bn)[None]
        z = _fft4(d4[:Ah], Q, z, gate.reshape(n_seq, Ah, W), dt)
    return z.reshape(n_seq * L, C)


def _ml_prep_body(x_ref, p_ref, n_ref, w_ref, b_ref, q_ref, kt_ref, *, regions):
    at_start, at_end = _seq_edges(pl.program_id(0), regions)
    y = _conv3(x_ref[...], p_ref[SUBLANES - 1:SUBLANES, :], n_ref[0:1, :], w_ref[...], b_ref[...], at_start, at_end)
    y = y * jax.nn.sigmoid(y)
    q_ref[...] = y[:, :ML_W].astype(BF16)
    kt_ref[...] = (y[:, ML_W:] * (ML_HD ** -0.5)).T


def _ml_prep(u_ml, w, b, regions):
    M = u_ml.shape[0]
    tm = ROW_TILE
    n = sum(r[1] for r in regions)
    return pl.pallas_call(
        functools.partial(_ml_prep_body, regions=regions), grid=(n,),
        in_specs=_halo_specs(tm, 2 * ML_W, 0, regions, M) + [pl.BlockSpec((3, 2 * ML_W), lambda i: (0, 0)),
                                                             pl.BlockSpec((1, 2 * ML_W), lambda i: (0, 0))],
        out_specs=[pl.BlockSpec((tm, ML_W), lambda i: (_seq_tile(i, regions), 0)),
                   pl.BlockSpec((ML_W, tm), lambda i: (0, _seq_tile(i, regions)))],
        out_shape=[jax.ShapeDtypeStruct((M, ML_W), BF16), jax.ShapeDtypeStruct((ML_W, M), F32)],
        compiler_params=_cparams("parallel"),
    )(u_ml, u_ml, u_ml, w, b[None])


def _ml_body(q_ref, kt_ref, v_ref, g_ref, o_ref, st_ref, m_ref):
    T = q_ref.shape[0]
    H, W = ML_HEADS, ML_W
    WA = W + LANES
    d = pl.program_id(1)

    @pl.when(pl.program_id(2) == 0)
    def _():
        st_ref[...] = jnp.zeros_like(st_ref)
        m_ref[...] = jnp.zeros_like(m_ref)

    sgn = 1 - 2 * d
    g = g_ref[...]
    gs = jnp.where(d == 0, g[0:2 * H], g[2 * H:4 * H])
    ig = gs[0:H]
    lf8 = -(jnp.maximum(-gs, 0.0) + jnp.log1p(jnp.exp(-jnp.abs(gs))))
    lf = lf8[H:2 * H]
    r_i = lax.broadcasted_iota(jnp.int32, (T, T), 0)
    c_i = lax.broadcasted_iota(jnp.int32, (T, T), 1)
    prec = ((c_i - r_i) * sgn) <= 0
    incl = jnp.where(((r_i - c_i) * sgn) <= 0, 1.0, 0.0).astype(BF16)
    after = jnp.where(((c_i - r_i) * sgn) < 0, 1.0, 0.0).astype(BF16)
    b_rows = _dot2_exact_rhs(lf8, incl)[H:2 * H]
    bL = jnp.sum(lf, axis=1, keepdims=True)
    a_row = bL - b_rows + ig
    m_loc = jnp.max(a_row, axis=1, keepdims=True)
    w_row = jnp.exp(a_row - m_loc)
    m0 = m_ref[0:H, 0:1]
    m_new = jnp.maximum(bL + m0, m_loc)
    s_old = jnp.exp(bL + m0 - m_new)
    s_loc = jnp.exp(m_loc - m_new)

    qb = q_ref[...]
    kt = kt_ref[...]
    row_head = lax.broadcasted_iota(jnp.int32, (W, T), 0) // ML_HD
    lane = lax.broadcasted_iota(jnp.int32, (1, WA), 1)
    lane_head = jnp.where(lane < W, lane // ML_HD, lane - W)
    v_aug = jnp.concatenate([v_ref[...], jnp.ones((T, LANES), F32)], axis=1)

    ps, vbd = [], []
    w_inter = jnp.zeros((T, WA), F32)
    e_m = jnp.zeros((T, WA), F32)
    for h in range(H):
        lf_h = lf[h:h + 1]
        Lf = jnp.where(prec, lf_h, 0.0)
        b_col = jnp.sum(Lf, axis=1, keepdims=True)
        E = _dot2_exact_rhs(Lf, after)
        Dm = jnp.where(prec, E + ig[h:h + 1], -jnp.inf)
        inter = b_col + m0[h:h + 1]
        m_col = jnp.maximum(inter, jnp.max(Dm, axis=1, keepdims=True))
        kth = jnp.where(row_head == h, kt, 0.0).astype(BF16)
        S = jnp.dot(qb, kth, preferred_element_type=F32)
        ps.append((jnp.exp(Dm - m_col) * S).astype(BF16))
        sel = lane_head == h
        vbd.append(jnp.where(sel, v_aug, 0.0).astype(BF16))
        w_inter = w_inter + jnp.where(sel, jnp.exp(inter - m_col), 0.0)
        e_m = e_m + jnp.where(sel, jnp.exp(-m_col), 0.0)
    nd = jnp.dot(jnp.concatenate(ps, axis=1), jnp.concatenate(vbd, axis=0), preferred_element_type=F32)
    nd = nd + w_inter * jnp.dot(qb, st_ref[...].astype(BF16), preferred_element_type=F32)
    den = jnp.zeros((T, W), F32)
    for h in range(H):
        den = den + jnp.where(lane_head[:, :W] == h, nd[:, W + h:W + h + 1], 0.0)
    o_ref[0] = nd[:, :W] / jnp.maximum(jnp.abs(den), e_m[:, :W])

    wk = jnp.zeros((W, T), F32)
    scol = jnp.zeros((1, WA), F32)
    ws = w_row * s_loc
    for h in range(H):
        wk = wk + jnp.where(row_head == h, ws[h:h + 1], 0.0)
        scol = scol + jnp.where(lane_head == h, s_old[h:h + 1], 0.0)
    st_loc = jnp.dot((kt * wk).astype(BF16), v_aug.astype(BF16), preferred_element_type=F32)
    diag = (lax.broadcasted_iota(jnp.int32, (W, WA), 0) // ML_HD) == lane_head
    st_ref[...] = jnp.where(diag, st_ref[...] * scol + st_loc, 0.0)
    m_ref[0:H, :] = jnp.broadcast_to(m_new, (H, LANES))


def _mlstm(q, kt, u_ml, g_t, B, L, Lc):
    M = q.shape[0]
    T = ML_CHUNK
    nC, nL = Lc // T, L // T
    lat0 = 0
    ctx0 = (B * L) // T

    def blk(b, d, i):
        cc = jnp.where(d == 0, i, nC - 1 - i)
        j = i - nC
        cl = jnp.where(d == 0, j, nL - 1 - j)
        return jnp.where(i < nC, ctx0 + b * nC + cc, lat0 + b * nL + cl)

    return pl.pallas_call(
        _ml_body, grid=(B, 2, nC + nL),
        in_specs=[pl.BlockSpec((T, ML_W), lambda b, d, i: (blk(b, d, i), 0)),
                  pl.BlockSpec((ML_W, T), lambda b, d, i: (0, blk(b, d, i))),
                  pl.BlockSpec((T, ML_W), lambda b, d, i: (blk(b, d, i), 2)),
                  pl.BlockSpec((4 * ML_HEADS, T), lambda b, d, i: (0, blk(b, d, i)))],
        out_specs=pl.BlockSpec((1, T, ML_W), lambda b, d, i: (d, blk(b, d, i), 0)),
        out_shape=jax.ShapeDtypeStruct((2, M, ML_W), F32),
        scratch_shapes=[pltpu.VMEM((ML_W, ML_W + LANES), F32), pltpu.VMEM((SUBLANES, LANES), F32)],
        compiler_params=_cparams("parallel", "parallel", "arbitrary"),
    )(q, kt, u_ml, g_t)


def _rope_tables(L):
    rows = L // GRID_W
    row = jnp.repeat(jnp.arange(rows, dtype=F32), GRID_W)
    col = jnp.tile(jnp.arange(GRID_W, dtype=F32), rows)
    half = DA_HD // 2
    inv = ROPE_THETA ** (-jnp.arange(0, half, 2, dtype=F32) / half)
    ang = jnp.stack([row, col], axis=-1)[:, :, None] * inv
    ang = jnp.stack([ang, ang], axis=-2).reshape(-1, DA_HD)
    ang = jnp.concatenate([ang, ang], axis=1)
    return jnp.cos(ang), jnp.sin(ang)


def _da_prep_body(u_ref, cos_ref, sin_ref, q_ref, k_ref, v_ref, *, n_lat_tiles):
    is_lat = pl.program_id(0) < n_lat_tiles
    u = u_ref[...]
    reps = DA_QK // cos_ref.shape[1]
    cs = jnp.concatenate([cos_ref[...]] * reps, axis=1)
    sn = jnp.concatenate([sin_ref[...]] * reps, axis=1)
    lane = lax.broadcasted_iota(jnp.int32, (1, DA_QK), 1)
    first = (lane % (DA_HD // 2)) < (DA_HD // 4)

    def rope(x):
        rot = jnp.where(first, -pltpu.roll(x, DA_QK - DA_HD // 4, 1), pltpu.roll(x, DA_HD // 4, 1))
        return jnp.where(is_lat, x * cs + rot * sn, x)

    q_ref[...] = (rope(u[:, :DA_QK]) * (DA_HD ** -0.5)).astype(BF16)
    k_ref[...] = rope(u[:, DA_QK:2 * DA_QK]).astype(BF16)
    v_ref[...] = u[:, 2 * DA_QK:].astype(BF16)


def _da_prep(u_da, cos, sin, n_lat_tiles, tiles_per_seq):
    M = u_da.shape[0]
    tm = ROW_TILE
    tab = pl.BlockSpec((tm, cos.shape[1]), lambda i: (jnp.where(i < n_lat_tiles, i % tiles_per_seq, 0), 0))
    out = pl.BlockSpec((tm, DA_QK), lambda i: (i, 0))
    return pl.pallas_call(
        functools.partial(_da_prep_body, n_lat_tiles=n_lat_tiles), grid=(M // tm,),
        in_specs=[pl.BlockSpec((tm, u_da.shape[1]), lambda i: (i, 0)), tab, tab],
        out_specs=[out] * 3,
        out_shape=[jax.ShapeDtypeStruct((M, DA_QK), BF16)] * 3,
        compiler_params=_cparams("parallel"),
    )(u_da, cos, sin)


def _da_body(lam_ref, w_ref, q_ref, *rest, nseg, lam_init):
    ks, vs, o_ref = rest[:nseg], rest[nseg:2 * nseg], rest[2 * nseg]
    lp = lam_ref[...]
    lam = (jnp.exp(jnp.sum(lp[0:1] * lp[1:2], axis=1, keepdims=True))
           - jnp.exp(jnp.sum(lp[2:3] * lp[3:4], axis=1, keepdims=True)) + lam_init)
    q = q_ref[...]
    lane = lax.broadcasted_iota(jnp.int32, (1, 2 * DA_HD), 1)
    outs = []
    for m in range(2):
        qm = jnp.where((lane // DA_HD) == m, q, jnp.zeros_like(q))
        ss = [lax.dot_general(qm, k[...], (((1,), (1,)), ((), ())), preferred_element_type=F32) for k in ks]
        mx = functools.reduce(jnp.maximum, [jnp.max(s, axis=1, keepdims=True) for s in ss])
        ps = [jnp.exp(s - mx) for s in ss]
        den = functools.reduce(jnp.add, [jnp.sum(pr, axis=1, keepdims=True) for pr in ps])
        acc = functools.reduce(jnp.add, [jnp.dot(pr.astype(BF16), v[...], preferred_element_type=F32)
                                         for pr, v in zip(ps, vs)])
        outs.append(acc / den)
    o = outs[0] - lam * outs[1]
    ms = jnp.mean(o * o, axis=1, keepdims=True)
    o_ref[...] = (o * lax.rsqrt(ms + NORM_EPS) * w_ref[...]) * (1.0 - lam_init)


def _diff_attn(qa, ka, va, lam_p, subln_w, lam_init, n_batch, q_rows0, q_len, segs):
    tq = min(ATT_Q_TILE, q_len)
    HW = 2 * DA_HD
    nq = q_len // tq
    q0 = q_rows0 // tq
    kv_specs = [pl.BlockSpec((n, HW), functools.partial(lambda b, h, i, f, n: (f // n + b, h), f=f, n=n))
                for (f, n) in segs]
    return pl.pallas_call(
        functools.partial(_da_body, nseg=len(segs), lam_init=lam_init),
        grid=(n_batch, DA_HEADS, nq),
        in_specs=[pl.BlockSpec(lam_p.shape, lambda b, h, i: (0, 0)), pl.BlockSpec((1, HW), lambda b, h, i: (0, 0)),
                  pl.BlockSpec((tq, HW), lambda b, h, i: (q0 + b * nq + i, h))] + kv_specs + kv_specs,
        out_specs=pl.BlockSpec((tq, HW), lambda b, h, i: (b * nq + i, h)),
        out_shape=jax.ShapeDtypeStruct((n_batch * q_len, DA_V), F32),
        compiler_params=_cparams("parallel", "parallel", "parallel"),
    )(lam_p, subln_w[None], qa, *([ka] * len(segs)), *([va] * len(segs)))


def _proj_out_body(x_ref, mod_ref, hy_ref, hm_ref, og_ref, da_ref, mw_ref, why_ref, wml_ref, wda_ref, o_ref):
    hs = hm_ref[0] + hm_ref[1]
    W = hs.shape[1]
    r = lax.broadcasted_iota(jnp.int32, (W, W), 0) // ML_HD
    c = lax.broadcasted_iota(jnp.int32, (W, W), 1) // ML_HD
    same_head = jnp.where(r == c, 1.0, 0.0).astype(BF16)
    ms = _dot2_exact_rhs(hs * hs, same_head) * (1.0 / ML_HD)
    y_ml = jax.nn.sigmoid(og_ref[...]) * (hs * lax.rsqrt(ms + NORM_EPS) * mw_ref[...])
    y = (_dot(hy_ref[...], why_ref[...]) + _dot(y_ml, wml_ref[...])) + _dot(da_ref[...], wda_ref[...])
    o_ref[...] = x_ref[...] + mod_ref[0, 2:3, :] * y


def _proj_out(xa, mod, y_hy, h_ml, u_ml, y_da, ml_norm_w, w_out, n_rows, n_lat_tiles, tiles_per_batch, n_batch):
    D = xa.shape[1]
    tm = ROW_TILE
    wb = w_out.astype(BF16)
    why, wml, wda = wb[:HY_CH], wb[HY_CH:HY_CH + ML_W], wb[HY_CH + ML_W:]
    full = lambda a: pl.BlockSpec(a.shape, lambda i: (0,) * a.ndim)
    row = lambda wd, cb=0: pl.BlockSpec((tm, wd), lambda i: (i, cb))
    mw = ml_norm_w[None]
    return pl.pallas_call(
        _proj_out_body, grid=(n_rows // tm,),
        in_specs=[row(D), pl.BlockSpec((1,) + mod.shape[1:],
                                       lambda i: (_mod_index(i, n_lat_tiles, tiles_per_batch, n_batch), 0, 0)),
                  row(HY_CH), pl.BlockSpec((2, tm, ML_W), lambda i: (0, i, 0)), row(ML_W, 3), row(DA_V),
                  full(mw), full(why), full(wml), full(wda)],
        out_specs=row(D),
        out_shape=jax.ShapeDtypeStruct((n_rows, D), F32),
        compiler_params=_cparams("parallel"),
    )(xa, mod, y_hy, h_ml, u_ml, y_da, mw, why, wml, wda)


ROUTE_LANE0 = MOE_GROUPS


def _router_body(x_ref, mod_ref, nw_ref, wr_ref, br_ref, h_ref, ri_ref, rf_ref, cnt_ref, run_ref):
    @pl.when(pl.program_id(0) == 0)
    def _():
        run_ref[...] = jnp.zeros_like(run_ref)

    h = _norm_mod(x_ref[...], nw_ref[...], mod_ref[0, 3:4, :], mod_ref[0, 4:5, :])
    h_ref[...] = h
    lg = _dot3(h, wr_ref[...]) + br_ref[...]
    tm = lg.shape[0]
    lane = lax.broadcasted_iota(jnp.int32, lg.shape, 1)
    neg = -jnp.inf
    is_g = lane < MOE_GROUPS
    gl = jnp.where(is_g, lg, neg)
    gmax = jnp.max(gl, axis=1, keepdims=True)
    gidx = jnp.min(jnp.where(gl == gmax, lane, LANES), axis=1, keepdims=True)
    gw = 1.0 / jnp.sum(jnp.where(is_g, jnp.exp(gl - gmax), 0.0), axis=1, keepdims=True)
    e_of = lane - ROUTE_LANE0
    in_grp = (e_of >= 0) & (e_of < MOE_EXPERTS) & ((e_of // MOE_PER_GROUP) == gidx)
    el = jnp.where(in_grp, lg, neg)
    t1 = jnp.max(el, axis=1, keepdims=True)
    i1 = jnp.min(jnp.where(el == t1, lane, LANES), axis=1, keepdims=True)
    el2 = jnp.where(lane == i1, neg, el)
    t2 = jnp.max(el2, axis=1, keepdims=True)
    i2 = jnp.min(jnp.where(el2 == t2, lane, LANES), axis=1, keepdims=True)
    ex = jnp.exp(t2 - t1)
    g1 = gw / (1.0 + ex)
    g2 = gw * ex / (1.0 + ex)
    oh = jnp.where((lane == i1) | (lane == i2), 1.0, 0.0)
    r_i = lax.broadcasted_iota(jnp.int32, (tm, tm), 0)
    c_i = lax.broadcasted_iota(jnp.int32, (tm, tm), 1)
    earlier = jnp.where(c_i < r_i, 1.0, 0.0).astype(BF16)
    cum = jnp.dot(earlier, oh.astype(BF16), preferred_element_type=F32) + run_ref[0:1, :]
    r1 = jnp.sum(jnp.where(lane == i1, cum, 0.0), axis=1, keepdims=True).astype(jnp.int32)
    r2 = jnp.sum(jnp.where(lane == i2, cum, 0.0), axis=1, keepdims=True).astype(jnp.int32)
    run = run_ref[0:1, :] + jnp.sum(oh, axis=0, keepdims=True)
    run_ref[...] = jnp.broadcast_to(run, run_ref.shape)
    cnt_ref[...] = jnp.broadcast_to(run, cnt_ref.shape)
    zi = jnp.zeros_like(lane)
    ri_ref[...] = jnp.where(lane == 0, i1 - ROUTE_LANE0, jnp.where(lane == 1, i2 - ROUTE_LANE0,
                            jnp.where(lane == 2, r1, jnp.where(lane == 3, r2, zi))))
    rf_ref[...] = jnp.where(lane == 0, g1, jnp.where(lane == 1, g2, 0.0))


def _router(xa, mod, nw, wg, bg, we, be, n_rows, n_lat_tiles, tiles_per_batch, n_batch):
    D = xa.shape[1]
    tm = ROW_TILE
    pad = LANES - MOE_GROUPS - MOE_EXPERTS
    wr = jnp.concatenate([wg, we, jnp.zeros((D, pad), F32)], axis=1)
    br = jnp.concatenate([bg, be, jnp.zeros((pad,), F32)])[None]
    row = lambda wd: pl.BlockSpec((tm, wd), lambda i: (i, 0))
    full = lambda a: pl.BlockSpec(a.shape, lambda i: (0,) * a.ndim)
    return pl.pallas_call(
        _router_body, grid=(n_rows // tm,),
        in_specs=[row(D), pl.BlockSpec((1,) + mod.shape[1:],
                                       lambda i: (_mod_index(i, n_lat_tiles, tiles_per_batch, n_batch), 0, 0)),
                  full(nw[None]), full(wr), full(br)],
        out_specs=[row(D), row(LANES), row(LANES), pl.BlockSpec((SUBLANES, LANES), lambda i: (0, 0))],
        out_shape=[jax.ShapeDtypeStruct((n_rows, D), F32), jax.ShapeDtypeStruct((n_rows, LANES), jnp.int32),
                   jax.ShapeDtypeStruct((n_rows, LANES), F32), jax.ShapeDtypeStruct((SUBLANES, LANES), F32)],
        scratch_shapes=[pltpu.VMEM((SUBLANES, LANES), F32)],
        compiler_params=_cparams("arbitrary"),
    )(xa, mod, nw[None], wr, br)


def _dispatch_body(dest_ref, lb_ref, nb_ref, h_ref, xb_ref, zbuf, sem, zsem):
    i = pl.program_id(0)
    tm = h_ref.shape[0]
    TB = zbuf.shape[0]
    n_blk = xb_ref.shape[0] // TB

    def zero_copy(blk):
        return pltpu.make_async_copy(zbuf, xb_ref.at[pl.ds(pl.multiple_of(blk * TB, TB), TB)], zsem)

    @pl.when(i == 0)
    def _():
        zbuf[...] = jnp.zeros_like(zbuf)
        for phase in ("start", "wait"):
            def tail(blk, carry, phase=phase):
                getattr(zero_copy(blk), phase)()
                return carry

            for e in range(lb_ref.shape[0]):
                @pl.when(lb_ref[e] >= 0)
                def _(e=e, phase=phase):
                    getattr(zero_copy(lb_ref[e]), phase)()
            lax.fori_loop(nb_ref[0], n_blk, tail, 0)

    def copy(r, j):
        return pltpu.make_async_copy(h_ref.at[pl.ds(r, 1)], xb_ref.at[pl.ds(dest_ref[i, 2 * r + j], 1)], sem)

    def issue(r, carry):
        copy(r, 0).start()
        copy(r, 1).start()
        return carry

    def drain(r, carry):
        copy(r, 0).wait()
        copy(r, 1).wait()
        return carry

    lax.fori_loop(0, tm, issue, 0)
    lax.fori_loop(0, tm, drain, 0)


def _dispatch(dest2d, last_blk, n_used, h2, n_slot_rows):
    n_rows, D = h2.shape
    tm = ROW_TILE
    return pl.pallas_call(
        _dispatch_body,
        grid_spec=pltpu.PrefetchScalarGridSpec(
            num_scalar_prefetch=3, grid=(n_rows // tm,),
            in_specs=[pl.BlockSpec((tm, D), lambda i, d, lb, nb: (i, 0))],
            out_specs=pl.BlockSpec(memory_space=pl.ANY),
            scratch_shapes=[pltpu.VMEM((MOE_ROWS, D), F32), pltpu.SemaphoreType.DMA(()),
                            pltpu.SemaphoreType.DMA(())]),
        out_shape=jax.ShapeDtypeStruct((n_slot_rows, D), F32),
        compiler_params=_cparams("arbitrary"),
    )(dest2d, last_blk, n_used, h2)


def _ffn_body(be_ref, nb_ref, x_ref, w1_ref, w3_ref, w2_ref, y_ref):
    del be_ref
    used = pl.program_id(0) < nb_ref[0]

    @pl.when(used)
    def _():
        xb = x_ref[...].astype(BF16)
        a = jnp.dot(xb, w1_ref[0].astype(BF16), preferred_element_type=F32)
        b = jnp.dot(xb, w3_ref[0].astype(BF16), preferred_element_type=F32)
        hmid = ((a * jax.nn.sigmoid(a)) * b).astype(BF16)
        y_ref[...] = jnp.dot(hmid, w2_ref[0].astype(BF16), preferred_element_type=F32)

    @pl.when(jnp.logical_not(used))
    def _():
        y_ref[...] = jnp.zeros_like(y_ref)


def _expert_ffn(blk_e, n_used, xb, w1, w3, w2):
    P, D = xb.shape
    F = w1.shape[2]
    TB = MOE_ROWS
    rows = pl.BlockSpec((TB, D), lambda i, be, nb: (i, 0))
    return pl.pallas_call(
        _ffn_body,
        grid_spec=pltpu.PrefetchScalarGridSpec(
            num_scalar_prefetch=2, grid=(P // TB,),
            in_specs=[rows, pl.BlockSpec((1, D, F), lambda i, be, nb: (be[i], 0, 0)),
                      pl.BlockSpec((1, D, F), lambda i, be, nb: (be[i], 0, 0)),
                      pl.BlockSpec((1, F, D), lambda i, be, nb: (be[i], 0, 0))],
            out_specs=rows),
        out_shape=jax.ShapeDtypeStruct((P, D), F32),
        compiler_params=_cparams("arbitrary"),
    )(blk_e, n_used, xb, w1, w3, w2)


def _combine_body(dest_ref, x_ref, mod_ref, rf_ref, yb_ref, fw_ref, o_ref, buf, sem, *, final):
    i = pl.program_id(0)
    tm = x_ref.shape[0]

    def copy(r, j):
        return pltpu.make_async_copy(yb_ref.at[pl.ds(dest_ref[i, 2 * r + j], 1)], buf.at[j, pl.ds(r, 1)], sem)

    def issue(r, carry):
        copy(r, 0).start()
        copy(r, 1).start()
        return carry

    def drain(r, carry):
        copy(r, 0).wait()
        copy(r, 1).wait()
        return carry

    lax.fori_loop(0, tm, issue, 0)
    lax.fori_loop(0, tm, drain, 0)
    g = rf_ref[...]
    f = g[:, 0:1] * buf[0] + g[:, 1:2] * buf[1]
    xn = x_ref[...] + mod_ref[0, 5:6, :] * f
    if final:
        ms = jnp.mean(xn * xn, axis=-1, keepdims=True)
        xn = xn * lax.rsqrt(ms + NORM_EPS) * fw_ref[...]
    o_ref[...] = xn


def _combine(dest2d, xa, mod, rf, yb, final_w, final, n_rows, n_lat_tiles, tiles_per_batch, n_batch):
    D = xa.shape[1]
    tm = ROW_TILE
    row = lambda wd: pl.BlockSpec((tm, wd), lambda i, d: (i, 0))
    return pl.pallas_call(
        functools.partial(_combine_body, final=final),
        grid_spec=pltpu.PrefetchScalarGridSpec(
            num_scalar_prefetch=1, grid=(n_rows // tm,),
            in_specs=[row(D), pl.BlockSpec((1,) + mod.shape[1:],
                                           lambda i, d: (_mod_index(i, n_lat_tiles, tiles_per_batch, n_batch), 0, 0)),
                      row(LANES), pl.BlockSpec(memory_space=pl.ANY), pl.BlockSpec((1, D), lambda i, d: (0, 0))],
            out_specs=row(D),
            scratch_shapes=[pltpu.VMEM((2, tm, D), F32), pltpu.SemaphoreType.DMA(())]),
        out_shape=jax.ShapeDtypeStruct((n_rows, D), F32),
        compiler_params=_cparams("arbitrary"),
    )(dest2d, xa, mod, rf, yb, final_w[None])


def _moe(xa, mod, p, final_w, final, n_rows, n_lat_tiles, tiles_per_batch, n_batch):
    D = xa.shape[1]
    TB = MOE_ROWS
    tm = ROW_TILE
    h2, ri, rf, cnt = _router(xa, mod, p['norm2_w'], p['moe_wg'], p['moe_bg'], p['moe_we'], p['moe_be'],
                              n_rows, n_lat_tiles, tiles_per_batch, n_batch)
    counts = cnt[0, ROUTE_LANE0:ROUTE_LANE0 + MOE_EXPERTS].astype(jnp.int32)
    pc = (counts + TB - 1) // TB * TB
    pend = jnp.cumsum(pc)
    base = pend - pc
    n_blk = -(-2 * n_rows // TB) + MOE_EXPERTS
    n_used = (pend[-1] // TB).astype(jnp.int32)
    dest = jnp.stack([base[ri[:, 0]] + ri[:, 2], base[ri[:, 1]] + ri[:, 3]], axis=1).astype(jnp.int32)
    dest2d = dest.reshape(n_rows // tm, 2 * tm)
    blk = jnp.arange(n_blk, dtype=jnp.int32)
    blk_e = jnp.minimum(jnp.searchsorted(pend, blk * TB, side='right'), MOE_EXPERTS - 1).astype(jnp.int32)
    blk_e = jnp.where(blk < n_used, blk_e, blk_e[n_used - 1])
    last_blk = jnp.where(pc > 0, pend // TB - 1, -1).astype(jnp.int32)
    xb = _dispatch(dest2d, last_blk, n_used[None], h2, n_blk * TB)
    yb = _expert_ffn(blk_e, n_used[None], xb, p['moe_w1'], p['moe_w3'], p['moe_w2'])
    return _combine(dest2d, xa, mod, rf, yb, final_w, final, n_rows, n_lat_tiles, tiles_per_batch, n_batch)


_LAYER_KEYS = ('ada_w', 'ada_b', 'norm1_w', 'norm2_w', 'w_in', 'b_in', 'w_out', 'hy_conv_w', 'hy_conv_b',
               'hy_filt_w1', 'hy_filt_b1', 'hy_filt_w2', 'hy_filt_b2', 'hy_filt_w3', 'hy_filt_b3', 'hy_sin_freq',
               'hy_bias_d', 'ml_conv_w', 'ml_conv_b', 'ml_norm_w', 'da_lambda', 'da_subln_w', 'moe_wg', 'moe_bg',
               'moe_we', 'moe_be', 'moe_w1', 'moe_w3', 'moe_w2')


def kernel(x, c, ctx, c_ctx, ada_w, ada_b, norm1_w, norm2_w, w_in, b_in, w_out, hy_conv_w, hy_conv_b, hy_filt_w1,
           hy_filt_b1, hy_filt_w2, hy_filt_b2, hy_filt_w3, hy_filt_b3, hy_sin_freq, hy_bias_d, ml_conv_w, ml_conv_b,
           ml_norm_w, da_lambda, da_subln_w, moe_wg, moe_bg, moe_we, moe_be, moe_w1, moe_w3, moe_w2, final_norm_w):
    stacked = dict(zip(_LAYER_KEYS, (ada_w, ada_b, norm1_w, norm2_w, w_in, b_in, w_out, hy_conv_w, hy_conv_b,
                                     hy_filt_w1, hy_filt_b1, hy_filt_w2, hy_filt_b2, hy_filt_w3, hy_filt_b3,
                                     hy_sin_freq, hy_bias_d, ml_conv_w, ml_conv_b, ml_norm_w, da_lambda, da_subln_w,
                                     moe_wg, moe_bg, moe_we, moe_be, moe_w1, moe_w3, moe_w2)))
    B, L, D = x.shape
    Lc = ctx.shape[1]
    depth = ada_w.shape[0]
    tm = ROW_TILE
    assert L % tm == 0 and Lc % tm == 0 and L % GRID_W == 0 and L % ML_CHUNK == 0 and Lc % ML_CHUNK == 0
    ML, MC = B * L, B * Lc
    M = ML + MC
    n_lat, n_ctx = ML // tm, MC // tm
    tpb = L // tm
    regions = ((0, n_lat, tpb), (n_lat, n_ctx, Lc // tm))

    xa = jnp.concatenate([x.reshape(ML, D), ctx.reshape(MC, D)], axis=0)
    R = -(-(B + 1) // SUBLANES) * SUBLANES
    cond = jnp.concatenate([c, c_ctx[None], jnp.zeros((R - B - 1, D), F32)], axis=0)
    cos, sin = _rope_tables(L)

    for l in range(depth):
        last = l == depth - 1
        p = {k: v[l] for k, v in stacked.items()}
        lam_init = 0.8 - 0.6 * math.exp(-0.3 * l)
        mod = _ada(cond, p['ada_w'], p['ada_b']).reshape(R, 6, D)
        u_hy, u_ml, u_da, g_t = _proj_in(xa, mod, p['norm1_w'], p['w_in'], p['b_in'], n_lat, tpb, B)

        y_hy = _hyena(*_hy_prep(u_hy, p['hy_conv_w'], p['hy_conv_b'], 0, n_lat, tpb), B, L, p)
        q_m, k_t = _ml_prep(u_ml, p['ml_conv_w'], p['ml_conv_b'], regions)
        h_ml = _mlstm(q_m, k_t, u_ml, g_t, B, L, Lc)
        qa, ka, va = _da_prep(u_da, cos, sin, n_lat, tpb)
        y_da = _diff_attn(qa, ka, va, p['da_lambda'], p['da_subln_w'], lam_init, B, 0, L, ((0, L), (ML, Lc)))
        n_rows = ML if last else M
        if not last:
            yc_hy = _hyena(*_hy_prep(u_hy, p['hy_conv_w'], p['hy_conv_b'], n_lat, n_ctx, Lc // tm), B, Lc, p)
            yc_da = _diff_attn(qa, ka, va, p['da_lambda'], p['da_subln_w'], lam_init, B, ML, Lc, ((ML, Lc),))
            y_hy = jnp.concatenate([y_hy, yc_hy], axis=0)
            y_da = jnp.concatenate([y_da, yc_da], axis=0)
        xa = _proj_out(xa, mod, y_hy, h_ml, u_ml, y_da, p['ml_norm_w'], p['w_out'], n_rows, n_lat, tpb, B)
        xa = _moe(xa, mod, p, final_norm_w, last, n_rows, n_lat, tpb, B)
    return xa.reshape(B, L, D)
```

```python
import functools
import math

import numpy as np
import jax
import jax.numpy as jnp
from jax import lax
from jax.experimental import pallas as pl
from jax.experimental.pallas import tpu as pltpu

F32 = jnp.float32
BF16 = jnp.bfloat16

NORM_EPS = 1e-6
GRID_W = 64

HY_CH = 256
HY_POS_EMB = 33
HY_FAST_DECAY = 0.3
HY_SLOW_DECAY = 1.5
HY_DECAY_TARGET = 1e-2

ML_HEADS = 4
ML_HD = 64
ML_W = ML_HEADS * ML_HD

DA_HEADS = 4
DA_HD = 64
DA_QK = DA_HEADS * 2 * DA_HD
DA_V = DA_HEADS * 2 * DA_HD
ROPE_THETA = 10000.0

HY_OFF = 0
ML_OFF = HY_OFF + 3 * HY_CH
GATE_OFF = ML_OFF + 4 * ML_W
DA_OFF = GATE_OFF + 2 * 2 * ML_HEADS

MOE_GROUPS = 4
MOE_PER_GROUP = 8
MOE_EXPERTS = MOE_GROUPS * MOE_PER_GROUP

LANES = 128
SUBLANES = 8
VMEM_BYTES_V7X = 64 * 1024 * 1024
VMEM_LIMIT = VMEM_BYTES_V7X * 7 // 8

ROW_TILE = 256
ML_CHUNK = 128
ATT_Q_TILE = 256
MOE_ROWS = 256


def _cparams(*sem):
    return pltpu.CompilerParams(dimension_semantics=tuple(sem), vmem_limit_bytes=VMEM_LIMIT)


def _dot(a, b):
    return jnp.dot(a.astype(BF16), b.astype(BF16), preferred_element_type=F32)


def _split(a):
    hi = a.astype(BF16)
    lo = (a - hi.astype(F32)).astype(BF16)
    return hi, lo


def _dot3(a, b):
    ah, al = _split(a)
    bh, bl = _split(b)
    d = functools.partial(jnp.dot, preferred_element_type=F32)
    return d(ah, bh) + (d(ah, bl) + d(al, bh))


def _dot2_exact_rhs(a, b_exact):
    ah, al = _split(a)
    d = functools.partial(jnp.dot, preferred_element_type=F32)
    return d(ah, b_exact) + d(al, b_exact)


def _ada_body(a_ref, w_ref, b_ref, o_ref):
    a = a_ref[...]
    a = a * jax.nn.sigmoid(a)
    o_ref[...] = _dot3(a, w_ref[...]) + b_ref[...]


def _ada(cond, w, b):
    R, D = cond.shape
    N = w.shape[1]
    tn = 1536
    return pl.pallas_call(
        _ada_body,
        grid=(N // tn,),
        in_specs=[pl.BlockSpec((R, D), lambda j: (0, 0)),
                  pl.BlockSpec((D, tn), lambda j: (0, j)),
                  pl.BlockSpec((1, tn), lambda j: (0, j))],
        out_specs=pl.BlockSpec((R, tn), lambda j: (0, j)),
        out_shape=jax.ShapeDtypeStruct((R, N), F32),
        compiler_params=_cparams("parallel"), name="ada_mod",
    )(cond, w, b[None])


def _mod_index(i, n_lat_tiles, tiles_per_batch, n_batch):
    return jnp.where(i < n_lat_tiles, i // tiles_per_batch, n_batch)


def _norm_mod(x, nw, shift, scale):
    ms = jnp.mean(x * x, axis=-1, keepdims=True)
    return (x * lax.rsqrt(ms + NORM_EPS) * nw) * (1.0 + scale) + shift


def _proj_in_body(x_ref, mod_ref, nw_ref, why_ref, wml_ref, wda_ref, wgt_ref, bhy_ref, bml_ref, bda_ref, bgt_ref,
                  ohy_ref, oml_ref, oda_ref, ogt_ref):
    h = _norm_mod(x_ref[...], nw_ref[...], mod_ref[0, 0:1, :], mod_ref[0, 1:2, :])
    hb = h.astype(BF16)
    d = functools.partial(jnp.dot, preferred_element_type=F32)
    ohy_ref[...] = d(hb, why_ref[...]) + bhy_ref[...]
    oml_ref[...] = d(hb, wml_ref[...]) + bml_ref[...]
    oda_ref[...] = d(hb, wda_ref[...]) + bda_ref[...]
    ogt_ref[...] = lax.dot_general(wgt_ref[...], hb, (((1,), (1,)), ((), ())),
                                   preferred_element_type=F32) + bgt_ref[...]


def _proj_in(xa, mod, nw, w_in, b_in, n_lat_tiles, tiles_per_batch, n_batch):
    M, D = xa.shape
    tm = ROW_TILE
    wb = w_in.astype(BF16)
    why, wml, wg, wda = wb[:, :ML_OFF], wb[:, ML_OFF:GATE_OFF], wb[:, GATE_OFF:DA_OFF], wb[:, DA_OFF:]
    bhy, bml, bg, bda = b_in[:ML_OFF], b_in[ML_OFF:GATE_OFF], b_in[GATE_OFF:DA_OFF], b_in[DA_OFF:]
    ng = wg.shape[1]
    full = lambda a: pl.BlockSpec(a.shape, lambda i: (0,) * a.ndim)
    args = (xa, mod, nw[None], why, wml, wda, wg.T, bhy[None], bml[None], bda[None], bg[:, None])
    in_specs = [pl.BlockSpec((tm, D), lambda i: (i, 0)),
                pl.BlockSpec((1,) + mod.shape[1:],
                             lambda i: (_mod_index(i, n_lat_tiles, tiles_per_batch, n_batch), 0, 0))]
    in_specs += [full(a) for a in args[2:]]
    widths = (why.shape[1], wml.shape[1], wda.shape[1])
    out_specs = [pl.BlockSpec((tm, wd), lambda i: (i, 0)) for wd in widths]
    out_specs.append(pl.BlockSpec((ng, tm), lambda i: (0, i)))
    out_shape = [jax.ShapeDtypeStruct((M, wd), F32) for wd in widths]
    out_shape.append(jax.ShapeDtypeStruct((ng, M), F32))
    return pl.pallas_call(
        _proj_in_body, grid=(M // tm,), in_specs=in_specs, out_specs=out_specs, out_shape=out_shape,
        compiler_params=_cparams("parallel"), name="proj_in",
    )(*args)


def _conv3(x, prev_row, next_row, w, b, at_start, at_end):
    T = x.shape[0]
    rows = lax.broadcasted_iota(jnp.int32, x.shape, 0)
    prev_row = jnp.where(at_start, 0.0, prev_row)
    next_row = jnp.where(at_end, 0.0, next_row)
    up = jnp.where(rows == 0, prev_row, pltpu.roll(x, 1, 0))
    dn = jnp.where(rows == T - 1, next_row, pltpu.roll(x, T - 1, 0))
    return up * w[0:1, :] + x * w[1:2, :] + dn * w[2:3, :] + b


def _seq_edges(i, regions):
    at_start = jnp.bool_(False)
    at_end = jnp.bool_(False)
    pos = 0
    for (_, n, tps) in regions:
        inside = (i >= pos) & (i < pos + n)
        r = (i - pos) % tps
        at_start = at_start | (inside & (r == 0))
        at_end = at_end | (inside & (r == tps - 1))
        pos += n
    return at_start, at_end


def _seq_tile(i, regions):
    pos = 0
    t = jnp.int32(0)
    for (first, n, _) in regions:
        t = jnp.where((i >= pos) & (i < pos + n), first + (i - pos), t)
        pos += n
    return t


def _halo_specs(tm, width, colblk, regions, n_rows):
    per = tm // SUBLANES
    last8 = n_rows // SUBLANES - 1
    cur = pl.BlockSpec((tm, width), lambda i: (_seq_tile(i, regions), colblk))
    prv = pl.BlockSpec((SUBLANES, width), lambda i: (jnp.maximum(_seq_tile(i, regions) * per - 1, 0), colblk))
    nxt = pl.BlockSpec((SUBLANES, width),
                       lambda i: (jnp.minimum((_seq_tile(i, regions) + 1) * per, last8), colblk))
    return [cur, prv, nxt]


def _hy_prep_body(x_ref, p_ref, n_ref, w_ref, b_ref, v_ref, x1_ref, x2_ref, *, regions):
    at_start, at_end = _seq_edges(pl.program_id(0), regions)
    y = _conv3(x_ref[...], p_ref[SUBLANES - 1:SUBLANES, :], n_ref[0:1, :], w_ref[...], b_ref[...], at_start, at_end)
    v_ref[...] = y[:, :HY_CH]
    x1_ref[...] = y[:, HY_CH:2 * HY_CH]
    x2_ref[...] = y[:, 2 * HY_CH:]


def _hy_prep(u_hy, w, b, first_tile, n_tiles, tiles_per_seq):
    M, W = u_hy.shape
    tm = ROW_TILE
    regions = ((first_tile, n_tiles, tiles_per_seq),)
    out = jax.ShapeDtypeStruct((n_tiles * tm, HY_CH), F32)
    return pl.pallas_call(
        functools.partial(_hy_prep_body, regions=regions),
        grid=(n_tiles,),
        in_specs=_halo_specs(tm, W, 0, regions, M) + [pl.BlockSpec((3, W), lambda i: (0, 0)),
                                                      pl.BlockSpec((1, W), lambda i: (0, 0))],
        out_specs=[pl.BlockSpec((tm, HY_CH), lambda i: (i, 0))] * 3,
        out_shape=[out] * 3,
        compiler_params=_cparams("parallel"), name="hy_prep",
    )(u_hy, u_hy, u_hy, w, b[None])


def _filt_body(z_ref, w1_ref, b1_ref, w2_ref, b2_ref, fr_ref, w3_ref, b3_ref, env_ref, o_ref):
    f = fr_ref[...]
    h = jnp.sin(f * (_dot3(z_ref[...], w1_ref[...]) + b1_ref[...]))
    h = jnp.sin(f * (_dot3(h, w2_ref[...]) + b2_ref[...]))
    k = _dot3(h, w3_ref[...]) + b3_ref[...]
    e = env_ref[...]
    kf = k[:, :HY_CH] * e
    kb = k[:, HY_CH:] * e
    s = (jnp.sum(jnp.abs(kf), axis=0, keepdims=True) + jnp.sum(jnp.abs(kb), axis=0, keepdims=True)
         - jnp.abs(kb[0:1, :]))
    o_ref[:, :HY_CH] = kf / s
    o_ref[:, HY_CH:] = kb / s


def _hy_filters(L, p):
    t = jnp.linspace(0.0, 1.0, L, dtype=F32)[:, None]
    bands = (HY_POS_EMB - 1) // 2
    w = (2.0 * math.pi / L) * jnp.arange(L, dtype=F32)[:, None]
    f = jnp.linspace(1e-4, bands - 1, bands, dtype=F32)[None, :]
    z = jnp.concatenate([t, jnp.cos(f * w), -jnp.sin(f * w)], axis=-1)
    zp = jnp.pad(z, ((0, 0), (0, LANES - HY_POS_EMB)))
    w1 = jnp.pad(p['hy_filt_w1'], ((0, LANES - HY_POS_EMB), (0, 0)))
    deltas = jnp.abs(jnp.linspace(math.log(HY_DECAY_TARGET) / HY_SLOW_DECAY,
                                  math.log(HY_DECAY_TARGET) / HY_FAST_DECAY, HY_CH, dtype=F32))
    env = jnp.exp(-t * deltas)
    hid = w1.shape[1]
    nout = p['hy_filt_w3'].shape[1]
    n_order = nout // (2 * HY_CH)
    c0 = lambda a: pl.BlockSpec(a.shape, lambda o: (0,) * a.ndim)
    args = (zp, w1, p['hy_filt_b1'][None], p['hy_filt_w2'], p['hy_filt_b2'][None], p['hy_sin_freq'][None],
            p['hy_filt_w3'], p['hy_filt_b3'][None], env)
    in_specs = [c0(a) for a in args[:6]]
    in_specs += [pl.BlockSpec((hid, 2 * HY_CH), lambda o: (0, o)), pl.BlockSpec((1, 2 * HY_CH), lambda o: (0, o)),
                 c0(env)]
    return pl.pallas_call(
        _filt_body, grid=(n_order,), in_specs=in_specs,
        out_specs=pl.BlockSpec((L, 2 * HY_CH), lambda o: (0, o)),
        out_shape=jax.ShapeDtypeStruct((L, nout), F32),
        compiler_params=_cparams("parallel"), name="hy_filter",
    )(*args)


def _fft_plan(L):
    N = 2 * L
    Bn = 128 if N >= 4096 else 16
    A = N // Bn
    assert A * Bn == N and A % 16 == 0
    return A, Bn


def _dft_small(A, N):
    ka = np.arange(A)[:, None]
    a = np.arange(A)[None, :]
    th = 2.0 * np.pi * ((ka * a) % A) / A
    d1 = np.concatenate([np.cos(th), -np.sin(th)], axis=0)
    d4 = np.concatenate([np.cos(th.T), -np.sin(th.T)], axis=1) / N
    return jnp.asarray(d1, F32), jnp.asarray(d4, F32)


def _dft_mid(A, Bn):
    N = A * Bn
    ka = jnp.arange(A, dtype=jnp.int32)[:, None, None]
    kb = jnp.arange(Bn, dtype=jnp.int32)[None, :, None]
    b = jnp.arange(Bn, dtype=jnp.int32)[None, None, :]
    m = (b * (kb * A + ka)) % N
    ph = m.astype(F32) * (2.0 * math.pi / N)
    c, s = jnp.cos(ph), jnp.sin(ph)
    mf = jnp.concatenate([jnp.concatenate([c, s], axis=2), jnp.concatenate([-s, c], axis=2)], axis=1)
    ct, st = jnp.swapaxes(c, 1, 2), jnp.swapaxes(s, 1, 2)
    mi = jnp.concatenate([jnp.concatenate([ct, -st], axis=2), jnp.concatenate([st, ct], axis=2)], axis=1)
    return mf.astype(BF16), mi.astype(BF16)


def _fft1_body(d_ref, x_ref, o_ref):
    o_ref[0] = jnp.dot(d_ref[...], x_ref[0].astype(BF16), preferred_element_type=F32).astype(BF16)


def _fft1(d1, x3):
    S, Ain, W = x3.shape
    A2 = d1.shape[0]
    tl = min(W, 8192)
    return pl.pallas_call(
        _fft1_body, grid=(S, W // tl),
        in_specs=[pl.BlockSpec((A2, Ain), lambda s, j: (0, 0)), pl.BlockSpec((1, Ain, tl), lambda s, j: (s, 0, j))],
        out_specs=pl.BlockSpec((1, A2, tl), lambda s, j: (s, 0, j)),
        out_shape=jax.ShapeDtypeStruct((S, A2, W), BF16),
        compiler_params=_cparams("parallel", "parallel"), name="hy_dft_slow",
    )(d1.astype(BF16), x3)


FFT_GROUP = 8


def _fft_spec_body(p_ref, mf_ref, k_ref, *, Bn):
    for j in range(FFT_GROUP):
        pin = jnp.concatenate([p_ref[0, 0, j], p_ref[0, 1, j]], axis=0)
        X = jnp.dot(mf_ref[j], pin, preferred_element_type=F32)
        k_ref[0, j, 0] = X[:Bn]
        k_ref[0, j, 1] = X[Bn:]


def _fft_spec(p5, mf):
    S, _, A, Bn, C = p5.shape
    G = FFT_GROUP
    return pl.pallas_call(
        functools.partial(_fft_spec_body, Bn=Bn), grid=(A // G, S),
        in_specs=[pl.BlockSpec((1, 2, G, Bn, C), lambda g, s: (s, 0, g, 0, 0)),
                  pl.BlockSpec((G, 2 * Bn, 2 * Bn), lambda g, s: (g, 0, 0))],
        out_specs=pl.BlockSpec((1, G, 2, Bn, C), lambda g, s: (s, g, 0, 0, 0)),
        out_shape=jax.ShapeDtypeStruct((S, A, 2, Bn, C), F32),
        compiler_params=_cparams("parallel", "parallel"), name="hy_filter_spec",
    )(p5, mf)


def _fft_mid_body(p_ref, mf_ref, mi_ref, k_ref, q_ref, *, Bn):
    for j in range(FFT_GROUP):
        pin = jnp.concatenate([p_ref[0, 0, j], p_ref[0, 1, j]], axis=0)
        X = jnp.dot(mf_ref[j], pin, preferred_element_type=F32)
        xr, xi = X[:Bn], X[Bn:]
        kr, ki = k_ref[j, 0], k_ref[j, 1]
        Y = jnp.concatenate([xr * kr - xi * ki, xr * ki + xi * kr], axis=0).astype(BF16)
        Q = jnp.dot(mi_ref[j], Y, preferred_element_type=F32)
        q_ref[0, 0, j] = Q[:Bn].astype(BF16)
        q_ref[0, 1, j] = Q[Bn:].astype(BF16)


def _fft_mid(p5, mf, mi, kspec):
    S, _, A, Bn, C = p5.shape
    G = FFT_GROUP
    blk = pl.BlockSpec((1, 2, G, Bn, C), lambda g, s: (s, 0, g, 0, 0))
    mat = pl.BlockSpec((G, 2 * Bn, 2 * Bn), lambda g, s: (g, 0, 0))
    return pl.pallas_call(
        functools.partial(_fft_mid_body, Bn=Bn), grid=(A // G, S),
        in_specs=[blk, mat, mat, pl.BlockSpec((G, 2, Bn, C), lambda g, s: (g, 0, 0, 0))],
        out_specs=blk,
        out_shape=jax.ShapeDtypeStruct(p5.shape, BF16),
        compiler_params=_cparams("parallel", "parallel"), name="hy_dft_fast",
    )(p5, mf, mi, kspec)


def _fft4_body(d_ref, q_ref, z_ref, g_ref, dt_ref, o_ref):
    y = jnp.dot(d_ref[...], q_ref[0], preferred_element_type=F32)
    o_ref[0] = g_ref[0] * (y + z_ref[0] * dt_ref[...])


def _fft4(d4, q3, z3, g3, dt):
    S, A2, W = q3.shape
    Ah = d4.shape[0]
    tl = min(W, 8192)
    sig = pl.BlockSpec((1, Ah, tl), lambda s, j: (s, 0, j))
    return pl.pallas_call(
        _fft4_body, grid=(S, W // tl),
        in_specs=[pl.BlockSpec((Ah, A2), lambda s, j: (0, 0)), pl.BlockSpec((1, A2, tl), lambda s, j: (s, 0, j)),
                  sig, sig, pl.BlockSpec((1, tl), lambda s, j: (0, j))],
        out_specs=sig,
        out_shape=jax.ShapeDtypeStruct((S, Ah, W), F32),
        compiler_params=_cparams("parallel", "parallel"), name="hy_idft_gate",
    )(d4.astype(BF16), q3, z3, g3, dt)


def _hyena(v, x1, x2, n_seq, L, p):
    C = HY_CH
    A, Bn = _fft_plan(L)
    Ah = A // 2
    W = Bn * C
    d1, d4 = _dft_small(A, A * Bn)
    mf, mi = _dft_mid(A, Bn)
    kn = _hy_filters(L, p).reshape(L, -1, 2, C)
    n_order = kn.shape[1]
    zero = jnp.zeros((1, C), F32)
    k2 = jnp.stack([jnp.concatenate([kn[:, o, 0], zero, jnp.flip(kn[1:, o, 1], axis=0)], axis=0)
                    for o in range(n_order)])
    kspec = _fft_spec(_fft1(d1, k2.reshape(n_order, A, W)).reshape(n_order, 2, A, Bn, C), mf)
    z = v.reshape(n_seq, Ah, W)
    for o, gate in enumerate((x1, x2)):
        P = _fft1(d1[:, :Ah], z).reshape(n_seq, 2, A, Bn, C)
        Q = _fft_mid(P, mf, mi, kspec[o]).reshape(n_seq, 2 * A, W)
        dt = jnp.tile(p['hy_bias_d'][o], Bn)[None]
        z = _fft4(d4[:Ah], Q, z, gate.reshape(n_seq, Ah, W), dt)
    return z.reshape(n_seq * L, C)


def _ml_prep_body(x_ref, p_ref, n_ref, w_ref, b_ref, q_ref, kt_ref, *, regions):
    at_start, at_end = _seq_edges(pl.program_id(0), regions)
    y = _conv3(x_ref[...], p_ref[SUBLANES - 1:SUBLANES, :], n_ref[0:1, :], w_ref[...], b_ref[...], at_start, at_end)
    y = y * jax.nn.sigmoid(y)
    q_ref[...] = y[:, :ML_W].astype(BF16)
    kt_ref[...] = (y[:, ML_W:] * (ML_HD ** -0.5)).T


def _ml_prep(u_ml, w, b, regions):
    M = u_ml.shape[0]
    tm = ROW_TILE
    n = sum(r[1] for r in regions)
    return pl.pallas_call(
        functools.partial(_ml_prep_body, regions=regions), grid=(n,),
        in_specs=_halo_specs(tm, 2 * ML_W, 0, regions, M) + [pl.BlockSpec((3, 2 * ML_W), lambda i: (0, 0)),
                                                             pl.BlockSpec((1, 2 * ML_W), lambda i: (0, 0))],
        out_specs=[pl.BlockSpec((tm, ML_W), lambda i: (_seq_tile(i, regions), 0)),
                   pl.BlockSpec((ML_W, tm), lambda i: (0, _seq_tile(i, regions)))],
        out_shape=[jax.ShapeDtypeStruct((M, ML_W), BF16), jax.ShapeDtypeStruct((ML_W, M), F32)],
        compiler_params=_cparams("parallel"), name="ml_prep",
    )(u_ml, u_ml, u_ml, w, b[None])


def _ml_body(q_ref, kt_ref, v_ref, g_ref, o_ref, st_ref, m_ref):
    T = q_ref.shape[0]
    H, W = ML_HEADS, ML_W
    WA = W + LANES
    d = pl.program_id(1)

    @pl.when(pl.program_id(2) == 0)
    def _():
        st_ref[...] = jnp.zeros_like(st_ref)
        m_ref[...] = jnp.zeros_like(m_ref)

    sgn = 1 - 2 * d
    g = g_ref[...]
    gs = jnp.where(d == 0, g[0:2 * H], g[2 * H:4 * H])
    ig = gs[0:H]
    lf8 = -(jnp.maximum(-gs, 0.0) + jnp.log1p(jnp.exp(-jnp.abs(gs))))
    lf = lf8[H:2 * H]
    r_i = lax.broadcasted_iota(jnp.int32, (T, T), 0)
    c_i = lax.broadcasted_iota(jnp.int32, (T, T), 1)
    prec = ((c_i - r_i) * sgn) <= 0
    incl = jnp.where(((r_i - c_i) * sgn) <= 0, 1.0, 0.0).astype(BF16)
    after = jnp.where(((c_i - r_i) * sgn) < 0, 1.0, 0.0).astype(BF16)
    b_rows = _dot2_exact_rhs(lf8, incl)[H:2 * H]
    bL = jnp.sum(lf, axis=1, keepdims=True)
    a_row = bL - b_rows + ig
    m_loc = jnp.max(a_row, axis=1, keepdims=True)
    w_row = jnp.exp(a_row - m_loc)
    m0 = m_ref[0:H, 0:1]
    m_new = jnp.maximum(bL + m0, m_loc)
    s_old = jnp.exp(bL + m0 - m_new)
    s_loc = jnp.exp(m_loc - m_new)

    qb = q_ref[...]
    kt = kt_ref[...]
    row_head = lax.broadcasted_iota(jnp.int32, (W, T), 0) // ML_HD
    lane = lax.broadcasted_iota(jnp.int32, (1, WA), 1)
    lane_head = jnp.where(lane < W, lane // ML_HD, lane - W)
    v_aug = jnp.concatenate([v_ref[...], jnp.ones((T, LANES), F32)], axis=1)

    ps, vbd = [], []
    w_inter = jnp.zeros((T, WA), F32)
    e_m = jnp.zeros((T, WA), F32)
    for h in range(H):
        lf_h = lf[h:h + 1]
        Lf = jnp.where(prec, lf_h, 0.0)
        b_col = jnp.sum(Lf, axis=1, keepdims=True)
        E = _dot2_exact_rhs(Lf, after)
        Dm = jnp.where(prec, E + ig[h:h + 1], -jnp.inf)
        inter = b_col + m0[h:h + 1]
        m_col = jnp.maximum(inter, jnp.max(Dm, axis=1, keepdims=True))
        kth = jnp.where(row_head == h, kt, 0.0).astype(BF16)
        S = jnp.dot(qb, kth, preferred_element_type=F32)
        ps.append((jnp.exp(Dm - m_col) * S).astype(BF16))
        sel = lane_head == h
        vbd.append(jnp.where(sel, v_aug, 0.0).astype(BF16))
        w_inter = w_inter + jnp.where(sel, jnp.exp(inter - m_col), 0.0)
        e_m = e_m + jnp.where(sel, jnp.exp(-m_col), 0.0)
    nd = jnp.dot(jnp.concatenate(ps, axis=1), jnp.concatenate(vbd, axis=0), preferred_element_type=F32)
    nd = nd + w_inter * jnp.dot(qb, st_ref[...].astype(BF16), preferred_element_type=F32)
    den = jnp.zeros((T, W), F32)
    for h in range(H):
        den = den + jnp.where(lane_head[:, :W] == h, nd[:, W + h:W + h + 1], 0.0)
    o_ref[0] = nd[:, :W] / jnp.maximum(jnp.abs(den), e_m[:, :W])

    wk = jnp.zeros((W, T), F32)
    scol = jnp.zeros((1, WA), F32)
    ws = w_row * s_loc
    for h in range(H):
        wk = wk + jnp.where(row_head == h, ws[h:h + 1], 0.0)
        scol = scol + jnp.where(lane_head == h, s_old[h:h + 1], 0.0)
    st_loc = jnp.dot((kt * wk).astype(BF16), v_aug.astype(BF16), preferred_element_type=F32)
    diag = (lax.broadcasted_iota(jnp.int32, (W, WA), 0) // ML_HD) == lane_head
    st_ref[...] = jnp.where(diag, st_ref[...] * scol + st_loc, 0.0)
    m_ref[0:H, :] = jnp.broadcast_to(m_new, (H, LANES))


def _mlstm(q, kt, u_ml, g_t, B, L, Lc):
    M = q.shape[0]
    T = ML_CHUNK
    nC, nL = Lc // T, L // T
    lat0 = 0
    ctx0 = (B * L) // T

    def blk(b, d, i):
        cc = jnp.where(d == 0, i, nC - 1 - i)
        j = i - nC
        cl = jnp.where(d == 0, j, nL - 1 - j)
        return jnp.where(i < nC, ctx0 + b * nC + cc, lat0 + b * nL + cl)

    return pl.pallas_call(
        _ml_body, grid=(B, 2, nC + nL),
        in_specs=[pl.BlockSpec((T, ML_W), lambda b, d, i: (blk(b, d, i), 0)),
                  pl.BlockSpec((ML_W, T), lambda b, d, i: (0, blk(b, d, i))),
                  pl.BlockSpec((T, ML_W), lambda b, d, i: (blk(b, d, i), 2)),
                  pl.BlockSpec((4 * ML_HEADS, T), lambda b, d, i: (0, blk(b, d, i)))],
        out_specs=pl.BlockSpec((1, T, ML_W), lambda b, d, i: (d, blk(b, d, i), 0)),
        out_shape=jax.ShapeDtypeStruct((2, M, ML_W), F32),
        scratch_shapes=[pltpu.VMEM((ML_W, ML_W + LANES), F32), pltpu.VMEM((SUBLANES, LANES), F32)],
        compiler_params=_cparams("parallel", "parallel", "arbitrary"), name="mlstm",
    )(q, kt, u_ml, g_t)


def _rope_tables(L):
    rows = L // GRID_W
    row = jnp.repeat(jnp.arange(rows, dtype=F32), GRID_W)
    col = jnp.tile(jnp.arange(GRID_W, dtype=F32), rows)
    half = DA_HD // 2
    inv = ROPE_THETA ** (-jnp.arange(0, half, 2, dtype=F32) / half)
    ang = jnp.stack([row, col], axis=-1)[:, :, None] * inv
    ang = jnp.stack([ang, ang], axis=-2).reshape(-1, DA_HD)
    ang = jnp.concatenate([ang, ang], axis=1)
    return jnp.cos(ang), jnp.sin(ang)


def _da_prep_body(u_ref, cos_ref, sin_ref, q_ref, k_ref, va_ref, *, n_lat_tiles):
    is_lat = pl.program_id(0) < n_lat_tiles
    u = u_ref[...]
    reps = DA_QK // cos_ref.shape[1]
    cs = jnp.concatenate([cos_ref[...]] * reps, axis=1)
    sn = jnp.concatenate([sin_ref[...]] * reps, axis=1)
    lane = lax.broadcasted_iota(jnp.int32, (1, DA_QK), 1)
    first = (lane % (DA_HD // 2)) < (DA_HD // 4)

    def rope(x):
        rot = jnp.where(first, -pltpu.roll(x, DA_QK - DA_HD // 4, 1), pltpu.roll(x, DA_HD // 4, 1))
        return jnp.where(is_lat, x * cs + rot * sn, x)

    q_ref[...] = (rope(u[:, :DA_QK]) * (DA_HD ** -0.5 * math.log2(math.e))).astype(BF16)
    k_ref[...] = rope(u[:, DA_QK:2 * DA_QK]).astype(BF16)
    va_ref[...] = u[:, 2 * DA_QK:].astype(BF16)


def _da_prep(u_da, cos, sin, n_lat_tiles, tiles_per_seq):
    M = u_da.shape[0]
    tm = ROW_TILE
    tab = pl.BlockSpec((tm, cos.shape[1]), lambda i: (jnp.where(i < n_lat_tiles, i % tiles_per_seq, 0), 0))
    out = pl.BlockSpec((tm, DA_QK), lambda i: (i, 0))
    return pl.pallas_call(
        functools.partial(_da_prep_body, n_lat_tiles=n_lat_tiles), grid=(M // tm,),
        in_specs=[pl.BlockSpec((tm, u_da.shape[1]), lambda i: (i, 0)), tab, tab],
        out_specs=[out] * 3,
        out_shape=[jax.ShapeDtypeStruct((M, DA_QK), BF16)] * 3,
        compiler_params=_cparams("parallel"), name="da_prep",
    )(u_da, cos, sin)


def _da_body(lam_ref, w_ref, q_ref, *rest, nseg, lam_init):
    ks, vas, o_ref = rest[:nseg], rest[nseg:2 * nseg], rest[2 * nseg]
    lp = lam_ref[...]
    lam = (jnp.exp(jnp.sum(lp[0:1] * lp[1:2], axis=1, keepdims=True))
           - jnp.exp(jnp.sum(lp[2:3] * lp[3:4], axis=1, keepdims=True)) + lam_init)
    q = q_ref[...]
    HW = 2 * DA_HD
    lane = lax.broadcasted_iota(jnp.int32, (1, HW), 1)
    outs = []
    for m in range(2):
        qm = jnp.where((lane // DA_HD) == m, q, jnp.zeros_like(q))
        ss = [lax.dot_general(qm, k[...], (((1,), (1,)), ((), ())), preferred_element_type=F32) for k in ks]
        mx = functools.reduce(jnp.maximum, [jnp.max(s, axis=1, keepdims=True) for s in ss])
        ps = [jnp.exp2(s - mx) for s in ss]
        den = functools.reduce(jnp.add, [jnp.sum(pr, axis=1, keepdims=True) for pr in ps])
        acc = functools.reduce(jnp.add, [jnp.dot(pr.astype(BF16), va[...], preferred_element_type=F32)
                                         for pr, va in zip(ps, vas)])
        outs.append(acc / den)
    o = outs[0] - lam * outs[1]
    ms = jnp.mean(o * o, axis=1, keepdims=True)
    o_ref[...] = (o * lax.rsqrt(ms + NORM_EPS) * w_ref[...]) * (1.0 - lam_init)


def _diff_attn(qa, ka, vaa, lam_p, subln_w, lam_init, n_batch, q_rows0, q_len, segs):
    tq = min(ATT_Q_TILE, q_len)
    HW = 2 * DA_HD
    nq = q_len // tq
    q0 = q_rows0 // tq
    k_specs = [pl.BlockSpec((n, HW), functools.partial(lambda b, h, i, f, n: (f // n + b, h), f=f, n=n))
               for (f, n) in segs]
    va_specs = k_specs
    return pl.pallas_call(
        functools.partial(_da_body, nseg=len(segs), lam_init=lam_init),
        grid=(n_batch, DA_HEADS, nq),
        in_specs=[pl.BlockSpec(lam_p.shape, lambda b, h, i: (0, 0)), pl.BlockSpec((1, HW), lambda b, h, i: (0, 0)),
                  pl.BlockSpec((tq, HW), lambda b, h, i: (q0 + b * nq + i, h))] + k_specs + va_specs,
        out_specs=pl.BlockSpec((tq, HW), lambda b, h, i: (b * nq + i, h)),
        out_shape=jax.ShapeDtypeStruct((n_batch * q_len, DA_V), F32),
        compiler_params=_cparams("parallel", "parallel", "parallel"), name="diff_attn",
    )(lam_p, subln_w[None], qa, *([ka] * len(segs)), *([vaa] * len(segs)))


def _proj_out_body(x_ref, mod_ref, hy_ref, hm_ref, og_ref, da_ref, mw_ref, why_ref, wml_ref, wda_ref, o_ref):
    hs = hm_ref[0] + hm_ref[1]
    W = hs.shape[1]
    r = lax.broadcasted_iota(jnp.int32, (W, W), 0) // ML_HD
    c = lax.broadcasted_iota(jnp.int32, (W, W), 1) // ML_HD
    same_head = jnp.where(r == c, 1.0, 0.0).astype(BF16)
    ms = _dot2_exact_rhs(hs * hs, same_head) * (1.0 / ML_HD)
    y_ml = jax.nn.sigmoid(og_ref[...]) * (hs * lax.rsqrt(ms + NORM_EPS) * mw_ref[...])
    y = (_dot(hy_ref[...], why_ref[...]) + _dot(y_ml, wml_ref[...])) + _dot(da_ref[...], wda_ref[...])
    o_ref[...] = x_ref[...] + mod_ref[0, 2:3, :] * y


def _proj_out(xa, mod, y_hy, h_ml, u_ml, y_da, ml_norm_w, w_out, n_rows, n_lat_tiles, tiles_per_batch, n_batch):
    D = xa.shape[1]
    tm = ROW_TILE
    wb = w_out.astype(BF16)
    why, wml, wda = wb[:HY_CH], wb[HY_CH:HY_CH + ML_W], wb[HY_CH + ML_W:]
    full = lambda a: pl.BlockSpec(a.shape, lambda i: (0,) * a.ndim)
    row = lambda wd, cb=0: pl.BlockSpec((tm, wd), lambda i: (i, cb))
    mw = ml_norm_w[None]
    return pl.pallas_call(
        _proj_out_body, grid=(n_rows // tm,),
        in_specs=[row(D), pl.BlockSpec((1,) + mod.shape[1:],
                                       lambda i: (_mod_index(i, n_lat_tiles, tiles_per_batch, n_batch), 0, 0)),
                  row(HY_CH), pl.BlockSpec((2, tm, ML_W), lambda i: (0, i, 0)), row(ML_W, 3), row(DA_V),
                  full(mw), full(why), full(wml), full(wda)],
        out_specs=row(D),
        out_shape=jax.ShapeDtypeStruct((n_rows, D), F32),
        compiler_params=_cparams("parallel"), name="proj_out",
    )(xa, mod, y_hy, h_ml, u_ml, y_da, mw, why, wml, wda)


ROUTE_LANE0 = MOE_GROUPS


def _router_body(x_ref, mod_ref, nw_ref, wr_ref, br_ref, h_ref, ri_ref, rf_ref, cnt_ref, run_ref):
    @pl.when(pl.program_id(0) == 0)
    def _():
        run_ref[...] = jnp.zeros_like(run_ref)

    h = _norm_mod(x_ref[...], nw_ref[...], mod_ref[0, 3:4, :], mod_ref[0, 4:5, :])
    h_ref[...] = h
    lg = _dot3(h, wr_ref[...]) + br_ref[...]
    tm = lg.shape[0]
    lane = lax.broadcasted_iota(jnp.int32, lg.shape, 1)
    neg = -jnp.inf
    is_g = lane < MOE_GROUPS
    gl = jnp.where(is_g, lg, neg)
    gmax = jnp.max(gl, axis=1, keepdims=True)
    gidx = jnp.min(jnp.where(gl == gmax, lane, LANES), axis=1, keepdims=True)
    gw = 1.0 / jnp.sum(jnp.where(is_g, jnp.exp(gl - gmax), 0.0), axis=1, keepdims=True)
    e_of = lane - ROUTE_LANE0
    in_grp = (e_of >= 0) & (e_of < MOE_EXPERTS) & ((e_of // MOE_PER_GROUP) == gidx)
    el = jnp.where(in_grp, lg, neg)
    t1 = jnp.max(el, axis=1, keepdims=True)
    i1 = jnp.min(jnp.where(el == t1, lane, LANES), axis=1, keepdims=True)
    el2 = jnp.where(lane == i1, neg, el)
    t2 = jnp.max(el2, axis=1, keepdims=True)
    i2 = jnp.min(jnp.where(el2 == t2, lane, LANES), axis=1, keepdims=True)
    ex = jnp.exp(t2 - t1)
    g1 = gw / (1.0 + ex)
    g2 = gw * ex / (1.0 + ex)
    oh = jnp.where((lane == i1) | (lane == i2), 1.0, 0.0)
    r_i = lax.broadcasted_iota(jnp.int32, (tm, tm), 0)
    c_i = lax.broadcasted_iota(jnp.int32, (tm, tm), 1)
    earlier = jnp.where(c_i < r_i, 1.0, 0.0).astype(BF16)
    cum = jnp.dot(earlier, oh.astype(BF16), preferred_element_type=F32) + run_ref[0:1, :]
    r1 = jnp.sum(jnp.where(lane == i1, cum, 0.0), axis=1, keepdims=True).astype(jnp.int32)
    r2 = jnp.sum(jnp.where(lane == i2, cum, 0.0), axis=1, keepdims=True).astype(jnp.int32)
    run = run_ref[0:1, :] + jnp.sum(oh, axis=0, keepdims=True)
    run_ref[...] = jnp.broadcast_to(run, run_ref.shape)
    cnt_ref[...] = jnp.broadcast_to(run, cnt_ref.shape)
    zi = jnp.zeros_like(lane)
    ri_ref[...] = jnp.where(lane == 0, i1 - ROUTE_LANE0, jnp.where(lane == 1, i2 - ROUTE_LANE0,
                            jnp.where(lane == 2, r1, jnp.where(lane == 3, r2, zi))))
    rf_ref[...] = jnp.where(lane == 0, g1, jnp.where(lane == 1, g2, 0.0))


def _router(xa, mod, nw, wg, bg, we, be, n_rows, n_lat_tiles, tiles_per_batch, n_batch):
    D = xa.shape[1]
    tm = ROW_TILE
    pad = LANES - MOE_GROUPS - MOE_EXPERTS
    wr = jnp.concatenate([wg, we, jnp.zeros((D, pad), F32)], axis=1)
    br = jnp.concatenate([bg, be, jnp.zeros((pad,), F32)])[None]
    row = lambda wd: pl.BlockSpec((tm, wd), lambda i: (i, 0))
    full = lambda a: pl.BlockSpec(a.shape, lambda i: (0,) * a.ndim)
    return pl.pallas_call(
        _router_body, grid=(n_rows // tm,),
        in_specs=[row(D), pl.BlockSpec((1,) + mod.shape[1:],
                                       lambda i: (_mod_index(i, n_lat_tiles, tiles_per_batch, n_batch), 0, 0)),
                  full(nw[None]), full(wr), full(br)],
        out_specs=[row(D), row(LANES), row(LANES), pl.BlockSpec((SUBLANES, LANES), lambda i: (0, 0))],
        out_shape=[jax.ShapeDtypeStruct((n_rows, D), F32), jax.ShapeDtypeStruct((n_rows, LANES), jnp.int32),
                   jax.ShapeDtypeStruct((n_rows, LANES), F32), jax.ShapeDtypeStruct((SUBLANES, LANES), F32)],
        scratch_shapes=[pltpu.VMEM((SUBLANES, LANES), F32)],
        compiler_params=_cparams("arbitrary"), name="moe_router",
    )(xa, mod, nw[None], wr, br)


def _plan_body(ri_ref, base_ref, d_ref):
    ri = ri_ref[...]
    lane = lax.broadcasted_iota(jnp.int32, ri.shape, 1)
    base = base_ref[...]
    d = []
    for j in range(2):
        b = jnp.sum(jnp.where(lane == ri[:, j:j + 1], base, 0.0), axis=1, keepdims=True)
        d.append(b + ri[:, 2 + j:3 + j].astype(F32))
    slots = jnp.where(lane == 0, d[0], jnp.where(lane == 1, d[1], 0.0))
    d_ref[0] = slots.T[0:SUBLANES, :].astype(jnp.int32)


def _plan(ri, base):
    n_rows = ri.shape[0]
    tm = ROW_TILE
    basef = jnp.pad(base.astype(F32), (0, LANES - base.shape[0]))[None]
    out = pl.pallas_call(
        _plan_body, grid=(n_rows // tm,),
        in_specs=[pl.BlockSpec((tm, LANES), lambda i: (i, 0)), pl.BlockSpec((1, LANES), lambda i: (0, 0))],
        out_specs=pl.BlockSpec((1, SUBLANES, tm), lambda i: (i, 0, 0)),
        out_shape=jax.ShapeDtypeStruct((n_rows // tm, SUBLANES, tm), jnp.int32),
        compiler_params=_cparams("parallel"), name="moe_plan",
    )(ri, basef)
    return out[:, 0, :], out[:, 1, :]


DMA_UNROLL = 8


def _dispatch_body(d0_ref, d1_ref, lb_ref, nb_ref, h_ref, xb_ref, zbuf, sem, zsem):
    i = pl.program_id(0)
    tm = h_ref.shape[0]
    TB = zbuf.shape[0]
    n_blk = xb_ref.shape[0] // TB

    def zero_copy(blk):
        return pltpu.make_async_copy(zbuf, xb_ref.at[pl.ds(pl.multiple_of(blk * TB, TB), TB)], zsem)

    @pl.when(i == 0)
    def _():
        zbuf[...] = jnp.zeros_like(zbuf)
        for phase in ("start", "wait"):
            def tail(blk, carry, phase=phase):
                getattr(zero_copy(blk), phase)()
                return carry

            for e in range(lb_ref.shape[0]):
                @pl.when(lb_ref[e] >= 0)
                def _(e=e, phase=phase):
                    getattr(zero_copy(lb_ref[e]), phase)()
            lax.fori_loop(nb_ref[0], n_blk, tail, 0)

    def issue(r, carry):
        for j, d_ref in enumerate((d0_ref, d1_ref)):
            pltpu.make_async_copy(h_ref.at[pl.ds(r, 1)], xb_ref.at[pl.ds(d_ref[i, r], 1)], sem).start()
        return carry

    lax.fori_loop(0, tm, issue, 0, unroll=DMA_UNROLL)
    pltpu.make_async_copy(xb_ref.at[pl.ds(0, 2 * tm)], xb_ref.at[pl.ds(0, 2 * tm)], sem).wait()


def _dispatch(d0, d1, last_blk, n_used, h2, n_slot_rows):
    n_rows, D = h2.shape
    tm = ROW_TILE
    return pl.pallas_call(
        _dispatch_body,
        grid_spec=pltpu.PrefetchScalarGridSpec(
            num_scalar_prefetch=4, grid=(n_rows // tm,),
            in_specs=[pl.BlockSpec((tm, D), lambda i, *_: (i, 0))],
            out_specs=pl.BlockSpec(memory_space=pl.ANY),
            scratch_shapes=[pltpu.VMEM((MOE_ROWS, D), F32), pltpu.SemaphoreType.DMA(()),
                            pltpu.SemaphoreType.DMA(())]),
        out_shape=jax.ShapeDtypeStruct((n_slot_rows, D), F32),
        compiler_params=pltpu.CompilerParams(dimension_semantics=("arbitrary",), vmem_limit_bytes=VMEM_LIMIT,
                                             disable_bounds_checks=True),
        name="moe_dispatch",
    )(d0, d1, last_blk, n_used, h2)


def _ffn_body(be_ref, nb_ref, x_ref, w1_ref, w3_ref, w2_ref, y_ref):
    del be_ref
    used = pl.program_id(0) < nb_ref[0]

    @pl.when(used)
    def _():
        xb = x_ref[...].astype(BF16)
        a = jnp.dot(xb, w1_ref[0].astype(BF16), preferred_element_type=F32)
        b = jnp.dot(xb, w3_ref[0].astype(BF16), preferred_element_type=F32)
        hmid = ((a * jax.nn.sigmoid(a)) * b).astype(BF16)
        y_ref[...] = jnp.dot(hmid, w2_ref[0].astype(BF16), preferred_element_type=F32)

    @pl.when(jnp.logical_not(used))
    def _():
        y_ref[...] = jnp.zeros_like(y_ref)


def _expert_ffn(blk_e, n_used, xb, w1, w3, w2):
    P, D = xb.shape
    F = w1.shape[2]
    TB = MOE_ROWS
    rows = pl.BlockSpec((TB, D), lambda i, be, nb: (i, 0))
    return pl.pallas_call(
        _ffn_body,
        grid_spec=pltpu.PrefetchScalarGridSpec(
            num_scalar_prefetch=2, grid=(P // TB,),
            in_specs=[rows, pl.BlockSpec((1, D, F), lambda i, be, nb: (be[i], 0, 0)),
                      pl.BlockSpec((1, D, F), lambda i, be, nb: (be[i], 0, 0)),
                      pl.BlockSpec((1, F, D), lambda i, be, nb: (be[i], 0, 0))],
            out_specs=rows),
        out_shape=jax.ShapeDtypeStruct((P, D), F32),
        compiler_params=_cparams("arbitrary"), name="moe_ffn",
    )(blk_e, n_used, xb, w1, w3, w2)


def _combine_body(d0_ref, d1_ref, x_ref, mod_ref, rf_ref, yb_ref, fw_ref, o_ref, buf, sem, *, final):
    i = pl.program_id(0)
    tm = x_ref.shape[0]

    def issue(r, carry):
        for j, d_ref in enumerate((d0_ref, d1_ref)):
            pltpu.make_async_copy(yb_ref.at[pl.ds(d_ref[i, r], 1)], buf.at[j, pl.ds(r, 1)], sem).start()
        return carry

    lax.fori_loop(0, tm, issue, 0, unroll=DMA_UNROLL)
    pltpu.make_async_copy(buf, buf, sem).wait()
    g = rf_ref[...]
    f = g[:, 0:1] * buf[0] + g[:, 1:2] * buf[1]
    xn = x_ref[...] + mod_ref[0, 5:6, :] * f
    if final:
        ms = jnp.mean(xn * xn, axis=-1, keepdims=True)
        xn = xn * lax.rsqrt(ms + NORM_EPS) * fw_ref[...]
    o_ref[...] = xn


def _combine(d0, d1, xa, mod, rf, yb, final_w, final, n_rows, n_lat_tiles, tiles_per_batch, n_batch):
    D = xa.shape[1]
    tm = ROW_TILE
    row = lambda wd: pl.BlockSpec((tm, wd), lambda i, *_: (i, 0))
    return pl.pallas_call(
        functools.partial(_combine_body, final=final),
        grid_spec=pltpu.PrefetchScalarGridSpec(
            num_scalar_prefetch=2, grid=(n_rows // tm,),
            in_specs=[row(D), pl.BlockSpec((1,) + mod.shape[1:],
                                           lambda i, *_: (_mod_index(i, n_lat_tiles, tiles_per_batch, n_batch), 0, 0)),
                      row(LANES), pl.BlockSpec(memory_space=pl.ANY), pl.BlockSpec((1, D), lambda i, *_: (0, 0))],
            out_specs=row(D),
            scratch_shapes=[pltpu.VMEM((2, tm, D), F32), pltpu.SemaphoreType.DMA(())]),
        out_shape=jax.ShapeDtypeStruct((n_rows, D), F32),
        compiler_params=pltpu.CompilerParams(dimension_semantics=("arbitrary",), vmem_limit_bytes=VMEM_LIMIT,
                                             disable_bounds_checks=True),
        name="moe_combine",
    )(d0, d1, xa, mod, rf, yb, final_w[None])


def _moe(xa, mod, p, final_w, final, n_rows, n_lat_tiles, tiles_per_batch, n_batch):
    TB = MOE_ROWS
    h2, ri, rf, cnt = _router(xa, mod, p['norm2_w'], p['moe_wg'], p['moe_bg'], p['moe_we'], p['moe_be'],
                              n_rows, n_lat_tiles, tiles_per_batch, n_batch)
    counts = cnt[0, ROUTE_LANE0:ROUTE_LANE0 + MOE_EXPERTS].astype(jnp.int32)
    pc = (counts + TB - 1) // TB * TB
    pend = jnp.cumsum(pc)
    base = pend - pc
    n_blk = -(-2 * n_rows // TB) + MOE_EXPERTS
    n_used = (pend[-1] // TB).astype(jnp.int32)
    blk = jnp.arange(n_blk, dtype=jnp.int32)
    blk_e = jnp.sum((pend[None, :] <= (jnp.minimum(blk, n_used - 1) * TB)[:, None]).astype(jnp.int32), axis=1)
    blk_e = jnp.minimum(blk_e, MOE_EXPERTS - 1).astype(jnp.int32)
    last_blk = jnp.where(pc > 0, pend // TB - 1, -1).astype(jnp.int32)
    d0, d1 = _plan(ri, base)
    xb = _dispatch(d0, d1, last_blk, n_used[None], h2, n_blk * TB)
    yb = _expert_ffn(blk_e, n_used[None], xb, p['moe_w1'], p['moe_w3'], p['moe_w2'])
    return _combine(d0, d1, xa, mod, rf, yb, final_w, final, n_rows, n_lat_tiles, tiles_per_batch, n_batch)


_LAYER_KEYS = ('ada_w', 'ada_b', 'norm1_w', 'norm2_w', 'w_in', 'b_in', 'w_out', 'hy_conv_w', 'hy_conv_b',
               'hy_filt_w1', 'hy_filt_b1', 'hy_filt_w2', 'hy_filt_b2', 'hy_filt_w3', 'hy_filt_b3', 'hy_sin_freq',
               'hy_bias_d', 'ml_conv_w', 'ml_conv_b', 'ml_norm_w', 'da_lambda', 'da_subln_w', 'moe_wg', 'moe_bg',
               'moe_we', 'moe_be', 'moe_w1', 'moe_w3', 'moe_w2')


def kernel(x, c, ctx, c_ctx, ada_w, ada_b, norm1_w, norm2_w, w_in, b_in, w_out, hy_conv_w, hy_conv_b, hy_filt_w1,
           hy_filt_b1, hy_filt_w2, hy_filt_b2, hy_filt_w3, hy_filt_b3, hy_sin_freq, hy_bias_d, ml_conv_w, ml_conv_b,
           ml_norm_w, da_lambda, da_subln_w, moe_wg, moe_bg, moe_we, moe_be, moe_w1, moe_w3, moe_w2, final_norm_w):
    stacked = dict(zip(_LAYER_KEYS, (ada_w, ada_b, norm1_w, norm2_w, w_in, b_in, w_out, hy_conv_w, hy_conv_b,
                                     hy_filt_w1, hy_filt_b1, hy_filt_w2, hy_filt_b2, hy_filt_w3, hy_filt_b3,
                                     hy_sin_freq, hy_bias_d, ml_conv_w, ml_conv_b, ml_norm_w, da_lambda, da_subln_w,
                                     moe_wg, moe_bg, moe_we, moe_be, moe_w1, moe_w3, moe_w2)))
    B, L, D = x.shape
    Lc = ctx.shape[1]
    depth = ada_w.shape[0]
    tm = ROW_TILE
    assert L % tm == 0 and Lc % tm == 0 and L % GRID_W == 0 and L % ML_CHUNK == 0 and Lc % ML_CHUNK == 0
    ML, MC = B * L, B * Lc
    M = ML + MC
    n_lat, n_ctx = ML // tm, MC // tm
    tpb = L // tm
    regions = ((0, n_lat, tpb), (n_lat, n_ctx, Lc // tm))

    xa = jnp.concatenate([x.reshape(ML, D), ctx.reshape(MC, D)], axis=0)
    R = -(-(B + 1) // SUBLANES) * SUBLANES
    cond = jnp.concatenate([c, c_ctx[None], jnp.zeros((R - B - 1, D), F32)], axis=0)
    cos, sin = _rope_tables(L)

    for l in range(depth):
        last = l == depth - 1
        p = {k: v[l] for k, v in stacked.items()}
        lam_init = 0.8 - 0.6 * math.exp(-0.3 * l)
        mod = _ada(cond, p['ada_w'], p['ada_b']).reshape(R, 6, D)
        u_hy, u_ml, u_da, g_t = _proj_in(xa, mod, p['norm1_w'], p['w_in'], p['b_in'], n_lat, tpb, B)

        y_hy = _hyena(*_hy_prep(u_hy, p['hy_conv_w'], p['hy_conv_b'], 0, n_lat, tpb), B, L, p)
        q_m, k_t = _ml_prep(u_ml, p['ml_conv_w'], p['ml_conv_b'], regions)
        h_ml = _mlstm(q_m, k_t, u_ml, g_t, B, L, Lc)
        qa, ka, va = _da_prep(u_da, cos, sin, n_lat, tpb)
        y_da = _diff_attn(qa, ka, va, p['da_lambda'], p['da_subln_w'], lam_init, B, 0, L, ((0, L), (ML, Lc)))
        n_rows = ML if last else M
        if not last:
            yc_hy = _hyena(*_hy_prep(u_hy, p['hy_conv_w'], p['hy_conv_b'], n_lat, n_ctx, Lc // tm), B, Lc, p)
            yc_da = _diff_attn(qa, ka, va, p['da_lambda'], p['da_subln_w'], lam_init, B, ML, Lc, ((ML, Lc),))
            y_hy = jnp.concatenate([y_hy, yc_hy], axis=0)
            y_da = jnp.concatenate([y_da, yc_da], axis=0)
        xa = _proj_out(xa, mod, y_hy, h_ml, u_ml, y_da, p['ml_norm_w'], p['w_out'], n_rows, n_lat, tpb, B)
        xa = _moe(xa, mod, p, final_norm_w, last, n_rows, n_lat, tpb, B)
    return xa.reshape(B, L, D)
```

```python
import functools
import math

import numpy as np
import jax
import jax.numpy as jnp
from jax import lax
from jax.experimental import pallas as pl
from jax.experimental.pallas import tpu as pltpu

F32 = jnp.float32
BF16 = jnp.bfloat16

NORM_EPS = 1e-6
GRID_W = 64

HY_CH = 256
HY_HALVES = 2
HY_POS_EMB = 33
HY_FAST_DECAY = 0.3
HY_SLOW_DECAY = 1.5
HY_DECAY_TARGET = 1e-2

ML_HEADS = 4
ML_HD = 64
ML_W = ML_HEADS * ML_HD

DA_HEADS = 4
DA_HD = 64
DA_QK = DA_HEADS * 2 * DA_HD
DA_V = DA_HEADS * 2 * DA_HD
ROPE_THETA = 10000.0

HY_OFF = 0
ML_OFF = HY_OFF + 3 * HY_CH
GATE_OFF = ML_OFF + 4 * ML_W
DA_OFF = GATE_OFF + 2 * 2 * ML_HEADS

MOE_GROUPS = 4
MOE_PER_GROUP = 8
MOE_EXPERTS = MOE_GROUPS * MOE_PER_GROUP

LANES = 128
SUBLANES = 8
VMEM_BYTES_V7X = 64 * 1024 * 1024
VMEM_LIMIT = VMEM_BYTES_V7X * 7 // 8

ROW_TILE = 256
ML_CHUNK = 128
ATT_Q_TILE = 512
ATT_SUB_TILE = 256
MOE_ROWS = 256


def _cparams(*sem):
    return pltpu.CompilerParams(dimension_semantics=tuple(sem), vmem_limit_bytes=VMEM_LIMIT)


def _dot(a, b):
    return jnp.dot(a.astype(BF16), b.astype(BF16), preferred_element_type=F32)


def _split(a):
    hi = a.astype(BF16)
    lo = (a - hi.astype(F32)).astype(BF16)
    return hi, lo


def _dot3(a, b):
    ah, al = _split(a)
    bh, bl = _split(b)
    d = functools.partial(jnp.dot, preferred_element_type=F32)
    return d(ah, bh) + (d(ah, bl) + d(al, bh))


def _dot2_exact_rhs(a, b_exact):
    ah, al = _split(a)
    d = functools.partial(jnp.dot, preferred_element_type=F32)
    return d(ah, b_exact) + d(al, b_exact)


def _ada_body(a_ref, w_ref, b_ref, o_ref):
    a = a_ref[...]
    a = a * jax.nn.sigmoid(a)
    o_ref[...] = _dot3(a, w_ref[...]) + b_ref[...]


def _ada(cond, w, b):
    R, D = cond.shape
    N = w.shape[1]
    tn = 1536
    return pl.pallas_call(
        _ada_body,
        grid=(N // tn,),
        in_specs=[pl.BlockSpec((R, D), lambda j: (0, 0)),
                  pl.BlockSpec((D, tn), lambda j: (0, j)),
                  pl.BlockSpec((1, tn), lambda j: (0, j))],
        out_specs=pl.BlockSpec((R, tn), lambda j: (0, j)),
        out_shape=jax.ShapeDtypeStruct((R, N), F32),
        compiler_params=_cparams("parallel"), name="ada_mod",
    )(cond, w, b[None])


def _mod_index(i, n_lat_tiles, tiles_per_batch, n_batch):
    return jnp.where(i < n_lat_tiles, i // tiles_per_batch, n_batch)


def _norm_mod(x, nw, shift, scale):
    ms = jnp.mean(x * x, axis=-1, keepdims=True)
    return (x * lax.rsqrt(ms + NORM_EPS) * nw) * (1.0 + scale) + shift


def _proj_in_body(x_ref, mod_ref, nw_ref, why_ref, wml_ref, wda_ref, wgt_ref, bhy_ref, bml_ref, bda_ref, bgt_ref,
                  ohy_ref, oml_ref, oda_ref, ogt_ref):
    h = _norm_mod(x_ref[...], nw_ref[...], mod_ref[0, 0:1, :], mod_ref[0, 1:2, :])
    hb = h.astype(BF16)
    d = functools.partial(jnp.dot, preferred_element_type=F32)
    ohy_ref[...] = d(hb, why_ref[...]) + bhy_ref[...]
    oml_ref[...] = d(hb, wml_ref[...]) + bml_ref[...]
    oda_ref[...] = d(hb, wda_ref[...]) + bda_ref[...]
    ogt_ref[...] = lax.dot_general(wgt_ref[...], hb, (((1,), (1,)), ((), ())),
                                   preferred_element_type=F32) + bgt_ref[...]


def _proj_in(xa, mod, nw, w_in, b_in, n_lat_tiles, tiles_per_batch, n_batch):
    M, D = xa.shape
    tm = ROW_TILE
    wb = w_in.astype(BF16)
    why, wml, wg, wda = wb[:, :ML_OFF], wb[:, ML_OFF:GATE_OFF], wb[:, GATE_OFF:DA_OFF], wb[:, DA_OFF:]
    bhy, bml, bg, bda = b_in[:ML_OFF], b_in[ML_OFF:GATE_OFF], b_in[GATE_OFF:DA_OFF], b_in[DA_OFF:]
    ng = wg.shape[1]
    full = lambda a: pl.BlockSpec(a.shape, lambda i: (0,) * a.ndim)
    args = (xa, mod, nw[None], why, wml, wda, wg.T, bhy[None], bml[None], bda[None], bg[:, None])
    in_specs = [pl.BlockSpec((tm, D), lambda i: (i, 0)),
                pl.BlockSpec((1,) + mod.shape[1:],
                             lambda i: (_mod_index(i, n_lat_tiles, tiles_per_batch, n_batch), 0, 0))]
    in_specs += [full(a) for a in args[2:]]
    widths = (why.shape[1], wml.shape[1], wda.shape[1])
    out_specs = [pl.BlockSpec((tm, wd), lambda i: (i, 0)) for wd in widths]
    out_specs.append(pl.BlockSpec((ng, tm), lambda i: (0, i)))
    out_shape = [jax.ShapeDtypeStruct((M, wd), F32) for wd in widths]
    out_shape.append(jax.ShapeDtypeStruct((ng, M), F32))
    return pl.pallas_call(
        _proj_in_body, grid=(M // tm,), in_specs=in_specs, out_specs=out_specs, out_shape=out_shape,
        compiler_params=_cparams("parallel"), name="proj_in",
    )(*args)


def _conv3(x, prev_row, next_row, w, b, at_start, at_end):
    T = x.shape[0]
    rows = lax.broadcasted_iota(jnp.int32, x.shape, 0)
    prev_row = jnp.where(at_start, 0.0, prev_row)
    next_row = jnp.where(at_end, 0.0, next_row)
    up = jnp.where(rows == 0, prev_row, pltpu.roll(x, 1, 0))
    dn = jnp.where(rows == T - 1, next_row, pltpu.roll(x, T - 1, 0))
    return up * w[0:1, :] + x * w[1:2, :] + dn * w[2:3, :] + b


def _seq_edges(i, regions):
    at_start = jnp.bool_(False)
    at_end = jnp.bool_(False)
    pos = 0
    for (_, n, tps) in regions:
        inside = (i >= pos) & (i < pos + n)
        r = (i - pos) % tps
        at_start = at_start | (inside & (r == 0))
        at_end = at_end | (inside & (r == tps - 1))
        pos += n
    return at_start, at_end


def _seq_tile(i, regions):
    pos = 0
    t = jnp.int32(0)
    for (first, n, _) in regions:
        t = jnp.where((i >= pos) & (i < pos + n), first + (i - pos), t)
        pos += n
    return t


def _halo_specs(tm, width, colblk, regions, n_rows):
    per = tm // SUBLANES
    last8 = n_rows // SUBLANES - 1
    cur = pl.BlockSpec((tm, width), lambda i: (_seq_tile(i, regions), colblk))
    prv = pl.BlockSpec((SUBLANES, width), lambda i: (jnp.maximum(_seq_tile(i, regions) * per - 1, 0), colblk))
    nxt = pl.BlockSpec((SUBLANES, width),
                       lambda i: (jnp.minimum((_seq_tile(i, regions) + 1) * per, last8), colblk))
    return [cur, prv, nxt]


def _hy_prep_body(x_ref, p_ref, n_ref, w_ref, b_ref, v_ref, x1_ref, x2_ref, *, regions):
    at_start, at_end = _seq_edges(pl.program_id(0), regions)
    y = _conv3(x_ref[...], p_ref[SUBLANES - 1:SUBLANES, :], n_ref[0:1, :], w_ref[...], b_ref[...], at_start, at_end)
    for k, ref in enumerate((v_ref, x1_ref, x2_ref)):
        for h in range(HY_HALVES):
            ref[h] = y[:, k * HY_CH + h * LANES:k * HY_CH + (h + 1) * LANES]


def _hy_prep(u_hy, w, b, first_tile, n_tiles, tiles_per_seq):
    M, W = u_hy.shape
    tm = ROW_TILE
    regions = ((first_tile, n_tiles, tiles_per_seq),)
    out = jax.ShapeDtypeStruct((HY_HALVES, n_tiles * tm, LANES), F32)
    return pl.pallas_call(
        functools.partial(_hy_prep_body, regions=regions),
        grid=(n_tiles,),
        in_specs=_halo_specs(tm, W, 0, regions, M) + [pl.BlockSpec((3, W), lambda i: (0, 0)),
                                                      pl.BlockSpec((1, W), lambda i: (0, 0))],
        out_specs=[pl.BlockSpec((HY_HALVES, tm, LANES), lambda i: (0, i, 0))] * 3,
        out_shape=[out] * 3,
        compiler_params=_cparams("parallel"), name="hy_prep",
    )(u_hy, u_hy, u_hy, w, b[None])


def _filt_body(z_ref, w1_ref, b1_ref, w2_ref, b2_ref, fr_ref, w3_ref, b3_ref, env_ref, o_ref):
    f = fr_ref[...]
    h = jnp.sin(f * (_dot3(z_ref[...], w1_ref[...]) + b1_ref[...]))
    h = jnp.sin(f * (_dot3(h, w2_ref[...]) + b2_ref[...]))
    k = _dot3(h, w3_ref[...]) + b3_ref[...]
    e = env_ref[...]
    kf = k[:, :HY_CH] * e
    kb = k[:, HY_CH:] * e
    s = (jnp.sum(jnp.abs(kf), axis=0, keepdims=True) + jnp.sum(jnp.abs(kb), axis=0, keepdims=True)
         - jnp.abs(kb[0:1, :]))
    o_ref[:, :HY_CH] = kf / s
    o_ref[:, HY_CH:] = kb / s


def _hy_filters(L, p):
    t = jnp.linspace(0.0, 1.0, L, dtype=F32)[:, None]
    bands = (HY_POS_EMB - 1) // 2
    w = (2.0 * math.pi / L) * jnp.arange(L, dtype=F32)[:, None]
    f = jnp.linspace(1e-4, bands - 1, bands, dtype=F32)[None, :]
    z = jnp.concatenate([t, jnp.cos(f * w), -jnp.sin(f * w)], axis=-1)
    zp = jnp.pad(z, ((0, 0), (0, LANES - HY_POS_EMB)))
    w1 = jnp.pad(p['hy_filt_w1'], ((0, LANES - HY_POS_EMB), (0, 0)))
    deltas = jnp.abs(jnp.linspace(math.log(HY_DECAY_TARGET) / HY_SLOW_DECAY,
                                  math.log(HY_DECAY_TARGET) / HY_FAST_DECAY, HY_CH, dtype=F32))
    env = jnp.exp(-t * deltas)
    hid = w1.shape[1]
    nout = p['hy_filt_w3'].shape[1]
    n_order = nout // (2 * HY_CH)
    c0 = lambda a: pl.BlockSpec(a.shape, lambda o: (0,) * a.ndim)
    args = (zp, w1, p['hy_filt_b1'][None], p['hy_filt_w2'], p['hy_filt_b2'][None], p['hy_sin_freq'][None],
            p['hy_filt_w3'], p['hy_filt_b3'][None], env)
    in_specs = [c0(a) for a in args[:6]]
    in_specs += [pl.BlockSpec((hid, 2 * HY_CH), lambda o: (0, o)), pl.BlockSpec((1, 2 * HY_CH), lambda o: (0, o)),
                 c0(env)]
    return pl.pallas_call(
        _filt_body, grid=(n_order,), in_specs=in_specs,
        out_specs=pl.BlockSpec((L, 2 * HY_CH), lambda o: (0, o)),
        out_shape=jax.ShapeDtypeStruct((L, nout), F32),
        compiler_params=_cparams("parallel"), name="hy_filter",
    )(*args)


def _fft_plan(L):
    N = 2 * L
    Bn = 128 if N >= 4096 else 16
    A = N // Bn
    assert A * Bn == N and A % 16 == 0
    return A, Bn


def _dft_small(A, N):
    ka = np.arange(A)[:, None]
    a = np.arange(A)[None, :]
    th = 2.0 * np.pi * ((ka * a) % A) / A
    d1 = np.concatenate([np.cos(th), -np.sin(th)], axis=0)
    d4 = np.concatenate([np.cos(th.T), -np.sin(th.T)], axis=1) / N
    return jnp.asarray(d1, F32), jnp.asarray(d4, F32)


def _dft_mid(A, Bn):
    N = A * Bn
    ka = jnp.arange(A, dtype=jnp.int32)[:, None, None]
    kb = jnp.arange(Bn, dtype=jnp.int32)[None, :, None]
    b = jnp.arange(Bn, dtype=jnp.int32)[None, None, :]
    m = (b * (kb * A + ka)) % N
    ph = m.astype(F32) * (2.0 * math.pi / N)
    c, s = jnp.cos(ph), jnp.sin(ph)
    mf = jnp.concatenate([jnp.concatenate([c, s], axis=2), jnp.concatenate([-s, c], axis=2)], axis=1)
    ct, st = jnp.swapaxes(c, 1, 2), jnp.swapaxes(s, 1, 2)
    mi = jnp.concatenate([jnp.concatenate([ct, -st], axis=2), jnp.concatenate([st, ct], axis=2)], axis=1)
    return mf.astype(BF16), mi.astype(BF16)


FFT_BT = 16
FFT_GROUP = 8


def _fft1_body(d_ref, x_ref, o_ref, *, Bn):
    Ain = d_ref.shape[1]
    b0 = pl.program_id(1) * o_ref.shape[1]
    d = d_ref[...]
    for t in range(o_ref.shape[1]):
        rows = pl.ds(b0 + t, Ain, stride=Bn)
        xb = jnp.concatenate([x_ref[h, rows, :] for h in range(HY_HALVES)], axis=1).astype(BF16)
        o_ref[0, t] = jnp.dot(d, xb, preferred_element_type=F32)


def _fft1(d1, x, n_seq, Bn):
    A2, Ain = d1.shape
    rows = Ain * Bn
    bt = min(FFT_BT, Bn)
    return pl.pallas_call(
        functools.partial(_fft1_body, Bn=Bn), grid=(n_seq, Bn // bt),
        in_specs=[pl.BlockSpec((A2, Ain), lambda s, j: (0, 0)),
                  pl.BlockSpec((HY_HALVES, rows, LANES), lambda s, j: (0, s, 0))],
        out_specs=pl.BlockSpec((1, bt, A2, HY_CH), lambda s, j: (s, j, 0, 0)),
        out_shape=jax.ShapeDtypeStruct((n_seq, Bn, A2, HY_CH), F32),
        compiler_params=_cparams("parallel", "arbitrary"), name="hy_dft_slow",
    )(d1.astype(BF16), x)


def _lane_half_specs(block, index_map):
    return [pl.BlockSpec(block[:-1] + (LANES,), functools.partial(lambda *a, h, f: f(*a)[:-1] + (h,), h=h, f=index_map))
            for h in range(HY_HALVES)]


def _fast_operand(p_refs, j):
    return jnp.concatenate([r[0, :, j, :] for r in p_refs], axis=1).astype(BF16)


def _fft_spec_body(p0_ref, p1_ref, mf_ref, k_ref, *, Bn):
    for j in range(FFT_GROUP):
        X = jnp.dot(mf_ref[j], _fast_operand((p0_ref, p1_ref), j), preferred_element_type=F32)
        k_ref[0, j, 0] = X[:Bn]
        k_ref[0, j, 1] = X[Bn:]


def _fft_spec(p4, mf):
    S, Bn2, A, C = p4.shape
    Bn = Bn2 // 2
    G = FFT_GROUP
    return pl.pallas_call(
        functools.partial(_fft_spec_body, Bn=Bn), grid=(A // G, S),
        in_specs=_lane_half_specs((1, Bn2, G, C), lambda g, s: (s, 0, g, 0))
        + [pl.BlockSpec((G, 2 * Bn, 2 * Bn), lambda g, s: (g, 0, 0))],
        out_specs=pl.BlockSpec((1, G, 2, Bn, C), lambda g, s: (s, g, 0, 0, 0)),
        out_shape=jax.ShapeDtypeStruct((S, A, 2, Bn, C), F32),
        compiler_params=_cparams("parallel", "parallel"), name="hy_filter_spec",
    )(p4, p4, mf)


def _fft_mid_body(p0_ref, p1_ref, mf_ref, mi_ref, k_ref, q_ref, *, Bn):
    for j in range(FFT_GROUP):
        X = jnp.dot(mf_ref[j], _fast_operand((p0_ref, p1_ref), j), preferred_element_type=F32)
        xr, xi = X[:Bn], X[Bn:]
        kr, ki = k_ref[j, 0], k_ref[j, 1]
        Y = jnp.concatenate([xr * kr - xi * ki, xr * ki + xi * kr], axis=0).astype(BF16)
        Q = jnp.dot(mi_ref[j], Y, preferred_element_type=F32)
        q_ref[0, j, 0] = Q[:Bn]
        q_ref[0, j, 1] = Q[Bn:]


def _fft_mid(p4, mf, mi, kspec):
    S, Bn2, A, C = p4.shape
    Bn = Bn2 // 2
    G = FFT_GROUP
    mat = pl.BlockSpec((G, 2 * Bn, 2 * Bn), lambda g, s: (g, 0, 0))
    return pl.pallas_call(
        functools.partial(_fft_mid_body, Bn=Bn), grid=(A // G, S),
        in_specs=_lane_half_specs((1, Bn2, G, C), lambda g, s: (s, 0, g, 0))
        + [mat, mat, pl.BlockSpec((G, 2, Bn, C), lambda g, s: (g, 0, 0, 0))],
        out_specs=pl.BlockSpec((1, G, 2, Bn, C), lambda g, s: (s, g, 0, 0, 0)),
        out_shape=jax.ShapeDtypeStruct((S, A, 2, Bn, C), F32),
        compiler_params=_cparams("parallel", "parallel"), name="hy_dft_fast",
    )(p4, p4, mf, mi, kspec)


def _fft4_body(d_ref, q0_ref, q1_ref, z_ref, g_ref, dch_ref, o_ref, *, Bn):
    Ah = d_ref.shape[0]
    bt = q0_ref.shape[2]
    b0 = pl.program_id(1) * bt
    d = d_ref[...]
    for t in range(bt):
        qb = jnp.concatenate([q0_ref[0, :, t, :], q1_ref[0, :, t, :]], axis=1).astype(BF16)
        y = jnp.dot(d, qb, preferred_element_type=F32)
        rows = pl.ds(b0 + t, Ah, stride=Bn)
        for h in range(HY_HALVES):
            lanes = slice(h * LANES, (h + 1) * LANES)
            o_ref[h, rows, :] = g_ref[h, rows, :] * (y[:, lanes] + z_ref[h, rows, :] * dch_ref[:, lanes])


def _fft4(d4, q4, z, gate, dch):
    S, A2, Bn, C = q4.shape
    Ah = d4.shape[0]
    L = Ah * Bn
    bt = min(FFT_BT, Bn)
    sig = pl.BlockSpec((HY_HALVES, L, LANES), lambda s, j: (0, s, 0))
    return pl.pallas_call(
        functools.partial(_fft4_body, Bn=Bn), grid=(S, Bn // bt),
        in_specs=[pl.BlockSpec((Ah, A2), lambda s, j: (0, 0))]
        + _lane_half_specs((1, A2, bt, C), lambda s, j: (s, 0, j, 0))
        + [sig, sig, pl.BlockSpec((1, C), lambda s, j: (0, 0))],
        out_specs=sig,
        out_shape=jax.ShapeDtypeStruct((HY_HALVES, S * L, LANES), F32),
        compiler_params=_cparams("parallel", "arbitrary"), name="hy_idft_gate",
    )(d4.astype(BF16), q4, q4, z, gate, dch)


def _hyena(v, x1, x2, n_seq, L, p):
    C = HY_CH
    A, Bn = _fft_plan(L)
    Ah = A // 2
    d1, d4 = _dft_small(A, A * Bn)
    mf, mi = _dft_mid(A, Bn)
    mf = mf.reshape(A, 2 * Bn, 2, Bn).transpose(0, 1, 3, 2).reshape(A, 2 * Bn, 2 * Bn)
    d4 = d4.reshape(A, 2, A).transpose(0, 2, 1).reshape(A, 2 * A)
    kn = _hy_filters(L, p).reshape(L, -1, 2, C)
    n_order = kn.shape[1]
    zero = jnp.zeros((1, C), F32)
    k2 = jnp.concatenate([piece for o in range(n_order)
                          for piece in (kn[:, o, 0], zero, jnp.flip(kn[1:, o, 1], axis=0))], axis=0)
    k2 = k2.reshape(-1, HY_HALVES, LANES).transpose(1, 0, 2)
    kspec = _fft_spec(_fft1(d1, k2, n_order, Bn).reshape(n_order, 2 * Bn, A, C), mf)
    z = v
    for o, gate in enumerate((x1, x2)):
        P = _fft1(d1[:, :Ah], z, n_seq, Bn).reshape(n_seq, 2 * Bn, A, C)
        Q = _fft_mid(P, mf, mi, kspec[o]).reshape(n_seq, 2 * A, Bn, C)
        z = _fft4(d4[:Ah], Q, z, gate, p['hy_bias_d'][o][None])
    return z


def _ml_prep_body(x_ref, p_ref, n_ref, w_ref, b_ref, q_ref, kt_ref, *, regions):
    at_start, at_end = _seq_edges(pl.program_id(0), regions)
    y = _conv3(x_ref[...], p_ref[SUBLANES - 1:SUBLANES, :], n_ref[0:1, :], w_ref[...], b_ref[...], at_start, at_end)
    y = y * jax.nn.sigmoid(y)
    q_ref[...] = y[:, :ML_W].astype(BF16)
    kt_ref[...] = (y[:, ML_W:] * (ML_HD ** -0.5)).T


def _ml_prep(u_ml, w, b, regions):
    M = u_ml.shape[0]
    tm = ROW_TILE
    n = sum(r[1] for r in regions)
    return pl.pallas_call(
        functools.partial(_ml_prep_body, regions=regions), grid=(n,),
        in_specs=_halo_specs(tm, 2 * ML_W, 0, regions, M) + [pl.BlockSpec((3, 2 * ML_W), lambda i: (0, 0)),
                                                             pl.BlockSpec((1, 2 * ML_W), lambda i: (0, 0))],
        out_specs=[pl.BlockSpec((tm, ML_W), lambda i: (_seq_tile(i, regions), 0)),
                   pl.BlockSpec((ML_W, tm), lambda i: (0, _seq_tile(i, regions)))],
        out_shape=[jax.ShapeDtypeStruct((M, ML_W), BF16), jax.ShapeDtypeStruct((ML_W, M), F32)],
        compiler_params=_cparams("parallel"), name="ml_prep",
    )(u_ml, u_ml, u_ml, w, b[None])


def _ml_body(*refs):
    ins, outs, scr = refs[:8], refs[8:10], refs[10:]

    @pl.when(pl.program_id(1) == 0)
    def _():
        for ref in scr:
            ref[...] = jnp.zeros_like(ref)

    for d in range(2):
        _ml_chunk(d, *ins[4 * d:4 * d + 4], outs[d], *scr[2 * d:2 * d + 2])


def _ml_chunk(d, q_ref, kt_ref, v_ref, g_ref, o_ref, st_ref, m_ref):
    T = q_ref.shape[0]
    H, W = ML_HEADS, ML_W
    WA = W + LANES
    sgn = 1 - 2 * d
    g = g_ref[...]
    gs = g[2 * H * d:2 * H * (d + 1)]
    ig = gs[0:H]
    lf8 = -(jnp.maximum(-gs, 0.0) + jnp.log1p(jnp.exp(-jnp.abs(gs))))
    lf = lf8[H:2 * H]
    r_i = lax.broadcasted_iota(jnp.int32, (T, T), 0)
    c_i = lax.broadcasted_iota(jnp.int32, (T, T), 1)
    prec = ((c_i - r_i) * sgn) <= 0
    incl = jnp.where(((r_i - c_i) * sgn) <= 0, 1.0, 0.0).astype(BF16)
    after = jnp.where(((c_i - r_i) * sgn) < 0, 1.0, 0.0).astype(BF16)
    b_rows = _dot2_exact_rhs(lf8, incl)[H:2 * H]
    bL = jnp.sum(lf, axis=1, keepdims=True)
    a_row = bL - b_rows + ig
    m_loc = jnp.max(a_row, axis=1, keepdims=True)
    w_row = jnp.exp(a_row - m_loc)
    m0 = m_ref[0:H, 0:1]
    m_new = jnp.maximum(bL + m0, m_loc)
    s_old = jnp.exp(bL + m0 - m_new)
    s_loc = jnp.exp(m_loc - m_new)

    qb = q_ref[...]
    kt = kt_ref[...]
    row_head = lax.broadcasted_iota(jnp.int32, (W, T), 0) // ML_HD
    lane = lax.broadcasted_iota(jnp.int32, (1, WA), 1)
    lane_head = jnp.where(lane < W, lane // ML_HD, lane - W)
    v_aug = jnp.concatenate([v_ref[...], jnp.ones((T, LANES), F32)], axis=1)

    ps, vbd = [], []
    w_inter = jnp.zeros((T, WA), F32)
    e_m = jnp.zeros((T, WA), F32)
    for h in range(H):
        lf_h = lf[h:h + 1]
        Lf = jnp.where(prec, lf_h, 0.0)
        b_col = jnp.sum(Lf, axis=1, keepdims=True)
        E = _dot2_exact_rhs(Lf, after)
        Dm = jnp.where(prec, E + ig[h:h + 1], -jnp.inf)
        inter = b_col + m0[h:h + 1]
        m_col = jnp.maximum(inter, jnp.max(Dm, axis=1, keepdims=True))
        kth = jnp.where(row_head == h, kt, 0.0).astype(BF16)
        S = jnp.dot(qb, kth, preferred_element_type=F32)
        ps.append((jnp.exp(Dm - m_col) * S).astype(BF16))
        sel = lane_head == h
        vbd.append(jnp.where(sel, v_aug, 0.0).astype(BF16))
        w_inter = w_inter + jnp.where(sel, jnp.exp(inter - m_col), 0.0)
        e_m = e_m + jnp.where(sel, jnp.exp(-m_col), 0.0)
    nd = jnp.dot(jnp.concatenate(ps, axis=1), jnp.concatenate(vbd, axis=0), preferred_element_type=F32)
    nd = nd + w_inter * jnp.dot(qb, st_ref[...].astype(BF16), preferred_element_type=F32)
    den = jnp.zeros((T, W), F32)
    for h in range(H):
        den = den + jnp.where(lane_head[:, :W] == h, nd[:, W + h:W + h + 1], 0.0)
    o_ref[...] = nd[:, :W] / jnp.maximum(jnp.abs(den), e_m[:, :W])

    wk = jnp.zeros((W, T), F32)
    scol = jnp.zeros((1, WA), F32)
    ws = w_row * s_loc
    for h in range(H):
        wk = wk + jnp.where(row_head == h, ws[h:h + 1], 0.0)
        scol = scol + jnp.where(lane_head == h, s_old[h:h + 1], 0.0)
    st_loc = jnp.dot((kt * wk).astype(BF16), v_aug.astype(BF16), preferred_element_type=F32)
    diag = (lax.broadcasted_iota(jnp.int32, (W, WA), 0) // ML_HD) == lane_head
    st_ref[...] = jnp.where(diag, st_ref[...] * scol + st_loc, 0.0)
    m_ref[0:H, :] = jnp.broadcast_to(m_new, (H, LANES))


def _mlstm(q, kt, u_ml, g_t, B, L, Lc):
    M = q.shape[0]
    T = ML_CHUNK
    nC, nL = Lc // T, L // T
    lat0 = 0
    ctx0 = (B * L) // T

    def blk(d, b, i):
        cc = i if d == 0 else nC - 1 - i
        j = i - nC
        cl = j if d == 0 else nL - 1 - j
        return jnp.where(i < nC, ctx0 + b * nC + cc, lat0 + b * nL + cl)

    in_specs, out_specs = [], []
    for d in range(2):
        rows = functools.partial(lambda b, i, d, c: (blk(d, b, i), c), d=d)
        cols = functools.partial(lambda b, i, d: (0, blk(d, b, i)), d=d)
        in_specs += [pl.BlockSpec((T, ML_W), functools.partial(rows, c=0)), pl.BlockSpec((ML_W, T), cols),
                     pl.BlockSpec((T, ML_W), functools.partial(rows, c=2)), pl.BlockSpec((4 * ML_HEADS, T), cols)]
        out_specs.append(pl.BlockSpec((T, ML_W), functools.partial(rows, c=0)))
    state = [pltpu.VMEM((ML_W, ML_W + LANES), F32), pltpu.VMEM((SUBLANES, LANES), F32)]
    return pl.pallas_call(
        _ml_body, grid=(B, nC + nL), in_specs=in_specs, out_specs=out_specs,
        out_shape=[jax.ShapeDtypeStruct((M, ML_W), F32)] * 2,
        scratch_shapes=state * 2,
        compiler_params=_cparams("parallel", "arbitrary"), name="mlstm",
    )(*([q, kt, u_ml, g_t] * 2))


def _rope_tables(L):
    rows = L // GRID_W
    row = jnp.repeat(jnp.arange(rows, dtype=F32), GRID_W)
    col = jnp.tile(jnp.arange(GRID_W, dtype=F32), rows)
    half = DA_HD // 2
    inv = ROPE_THETA ** (-jnp.arange(0, half, 2, dtype=F32) / half)
    ang = jnp.stack([row, col], axis=-1)[:, :, None] * inv
    ang = jnp.stack([ang, ang], axis=-2).reshape(-1, DA_HD)
    ang = jnp.concatenate([ang, ang], axis=1)
    return jnp.cos(ang), jnp.sin(ang)


def _da_prep_body(u_ref, cos_ref, sin_ref, q_ref, k_ref, va_ref, *, n_lat_tiles):
    is_lat = pl.program_id(0) < n_lat_tiles
    u = u_ref[...]
    reps = DA_QK // cos_ref.shape[1]
    cs = jnp.concatenate([cos_ref[...]] * reps, axis=1)
    sn = jnp.concatenate([sin_ref[...]] * reps, axis=1)
    lane = lax.broadcasted_iota(jnp.int32, (1, DA_QK), 1)
    first = (lane % (DA_HD // 2)) < (DA_HD // 4)

    def rope(x):
        rot = jnp.where(first, -pltpu.roll(x, DA_QK - DA_HD // 4, 1), pltpu.roll(x, DA_HD // 4, 1))
        return jnp.where(is_lat, x * cs + rot * sn, x)

    q_ref[...] = (rope(u[:, :DA_QK]) * (DA_HD ** -0.5 * math.log2(math.e))).astype(BF16)
    k_ref[...] = rope(u[:, DA_QK:2 * DA_QK]).astype(BF16)
    va_ref[...] = u[:, 2 * DA_QK:].astype(BF16)


def _da_prep(u_da, cos, sin, n_lat_tiles, tiles_per_seq):
    M = u_da.shape[0]
    tm = ROW_TILE
    tab = pl.BlockSpec((tm, cos.shape[1]), lambda i: (jnp.where(i < n_lat_tiles, i % tiles_per_seq, 0), 0))
    out = pl.BlockSpec((tm, DA_QK), lambda i: (i, 0))
    return pl.pallas_call(
        functools.partial(_da_prep_body, n_lat_tiles=n_lat_tiles), grid=(M // tm,),
        in_specs=[pl.BlockSpec((tm, u_da.shape[1]), lambda i: (i, 0)), tab, tab],
        out_specs=[out] * 3,
        out_shape=[jax.ShapeDtypeStruct((M, DA_QK), BF16)] * 3,
        compiler_params=_cparams("parallel"), name="da_prep",
    )(u_da, cos, sin)


def _da_body(lam_ref, w_ref, q_ref, *rest, nseg, lam_init, sub):
    ks, vas, o_ref = rest[:nseg], rest[nseg:2 * nseg], rest[2 * nseg]
    lp = lam_ref[...]
    lam = (jnp.exp(jnp.sum(lp[0:1] * lp[1:2], axis=1, keepdims=True))
           - jnp.exp(jnp.sum(lp[2:3] * lp[3:4], axis=1, keepdims=True)) + lam_init)
    HW = 2 * DA_HD
    lane = lax.broadcasted_iota(jnp.int32, (1, HW), 1)
    n_sub = q_ref.shape[0] // sub
    def scores(i):
        t, m = divmod(i, 2)
        q = q_ref[t * sub:(t + 1) * sub, :]
        qm = jnp.where((lane // DA_HD) == m, q, jnp.zeros_like(q))
        return [lax.dot_general(qm, k[...], (((1,), (1,)), ((), ())), preferred_element_type=F32) for k in ks]

    def softmax(ss):
        mx = functools.reduce(jnp.maximum, [jnp.max(s, axis=1, keepdims=True) for s in ss])
        ps = [jnp.exp2(s - mx) for s in ss]
        den = functools.reduce(jnp.add, [jnp.sum(pr, axis=1, keepdims=True) for pr in ps])
        return [pr.astype(BF16) for pr in ps], den

    def values(ps, den):
        acc = functools.reduce(jnp.add, [jnp.dot(pr, va[...], preferred_element_type=F32)
                                         for pr, va in zip(ps, vas)])
        return acc / den

    n_streams = 2 * n_sub
    sss = {0: scores(0)}
    if n_streams > 1:
        sss[1] = scores(1)
    outs = []
    for i in range(n_streams):
        pd = softmax(sss.pop(i))
        if i + 2 < n_streams:
            sss[i + 2] = scores(i + 2)
        outs.append(values(*pd))
    for t in range(n_sub):
        o = outs[2 * t] - lam * outs[2 * t + 1]
        ms = jnp.mean(o * o, axis=1, keepdims=True)
        o_ref[t * sub:(t + 1) * sub, :] = (o * lax.rsqrt(ms + NORM_EPS) * w_ref[...]) * (1.0 - lam_init)


def _diff_attn(qa, ka, vaa, lam_p, subln_w, lam_init, n_batch, q_rows0, q_len, segs):
    tq = min(ATT_Q_TILE, q_len)
    HW = 2 * DA_HD
    nq = q_len // tq
    q0 = q_rows0 // tq
    k_specs = [pl.BlockSpec((n, HW), functools.partial(lambda b, h, i, f, n: (f // n + b, h), f=f, n=n))
               for (f, n) in segs]
    va_specs = k_specs
    return pl.pallas_call(
        functools.partial(_da_body, nseg=len(segs), lam_init=lam_init, sub=min(ATT_SUB_TILE, tq)),
        grid=(n_batch, DA_HEADS, nq),
        in_specs=[pl.BlockSpec(lam_p.shape, lambda b, h, i: (0, 0)), pl.BlockSpec((1, HW), lambda b, h, i: (0, 0)),
                  pl.BlockSpec((tq, HW), lambda b, h, i: (q0 + b * nq + i, h))] + k_specs + va_specs,
        out_specs=pl.BlockSpec((tq, HW), lambda b, h, i: (b * nq + i, h)),
        out_shape=jax.ShapeDtypeStruct((n_batch * q_len, DA_V), F32),
        compiler_params=_cparams("parallel", "parallel", "parallel"), name="diff_attn",
    )(lam_p, subln_w[None], qa, *([ka] * len(segs)), *([vaa] * len(segs)))


def _proj_out_body(x_ref, mod_ref, hy_ref, hf_ref, hb_ref, og_ref, da_ref, mw_ref, why_ref, wml_ref, wda_ref, o_ref):
    hs = hf_ref[...] + hb_ref[...]
    W = hs.shape[1]
    r = lax.broadcasted_iota(jnp.int32, (W, W), 0) // ML_HD
    c = lax.broadcasted_iota(jnp.int32, (W, W), 1) // ML_HD
    same_head = jnp.where(r == c, 1.0, 0.0).astype(BF16)
    ms = _dot2_exact_rhs(hs * hs, same_head) * (1.0 / ML_HD)
    y_ml = jax.nn.sigmoid(og_ref[...]) * (hs * lax.rsqrt(ms + NORM_EPS) * mw_ref[...])
    y_hy = jnp.concatenate([hy_ref[h] for h in range(HY_HALVES)], axis=1)
    y = (_dot(y_hy, why_ref[...]) + _dot(y_ml, wml_ref[...])) + _dot(da_ref[...], wda_ref[...])
    o_ref[...] = x_ref[...] + mod_ref[0, 2:3, :] * y


def _proj_out(xa, mod, y_hy, h_ml, u_ml, y_da, ml_norm_w, w_out, n_rows, n_lat_tiles, tiles_per_batch, n_batch):
    D = xa.shape[1]
    tm = ROW_TILE
    wb = w_out.astype(BF16)
    why, wml, wda = wb[:HY_CH], wb[HY_CH:HY_CH + ML_W], wb[HY_CH + ML_W:]
    full = lambda a: pl.BlockSpec(a.shape, lambda i: (0,) * a.ndim)
    row = lambda wd, cb=0: pl.BlockSpec((tm, wd), lambda i: (i, cb))
    mw = ml_norm_w[None]
    return pl.pallas_call(
        _proj_out_body, grid=(n_rows // tm,),
        in_specs=[row(D), pl.BlockSpec((1,) + mod.shape[1:],
                                       lambda i: (_mod_index(i, n_lat_tiles, tiles_per_batch, n_batch), 0, 0)),
                  pl.BlockSpec((HY_HALVES, tm, LANES), lambda i: (0, i, 0)),
                  row(ML_W), row(ML_W), row(ML_W, 3), row(DA_V),
                  full(mw), full(why), full(wml), full(wda)],
        out_specs=row(D),
        out_shape=jax.ShapeDtypeStruct((n_rows, D), F32),
        compiler_params=_cparams("parallel"), name="proj_out",
    )(xa, mod, y_hy, *h_ml, u_ml, y_da, mw, why, wml, wda)


ROUTE_LANE0 = MOE_GROUPS


def _router_body(x_ref, mod_ref, nw_ref, wr_ref, br_ref, h_ref, ri_ref, rf_ref, cnt_ref, run_ref):
    @pl.when(pl.program_id(0) == 0)
    def _():
        run_ref[...] = jnp.zeros_like(run_ref)

    h = _norm_mod(x_ref[...], nw_ref[...], mod_ref[0, 3:4, :], mod_ref[0, 4:5, :])
    h_ref[...] = h
    lg = _dot3(h, wr_ref[...]) + br_ref[...]
    tm = lg.shape[0]
    lane = lax.broadcasted_iota(jnp.int32, lg.shape, 1)
    neg = -jnp.inf
    is_g = lane < MOE_GROUPS
    gl = jnp.where(is_g, lg, neg)
    gmax = jnp.max(gl, axis=1, keepdims=True)
    gidx = jnp.min(jnp.where(gl == gmax, lane, LANES), axis=1, keepdims=True)
    gw = 1.0 / jnp.sum(jnp.where(is_g, jnp.exp(gl - gmax), 0.0), axis=1, keepdims=True)
    e_of = lane - ROUTE_LANE0
    in_grp = (e_of >= 0) & (e_of < MOE_EXPERTS) & ((e_of // MOE_PER_GROUP) == gidx)
    el = jnp.where(in_grp, lg, neg)
    t1 = jnp.max(el, axis=1, keepdims=True)
    i1 = jnp.min(jnp.where(el == t1, lane, LANES), axis=1, keepdims=True)
    el2 = jnp.where(lane == i1, neg, el)
    t2 = jnp.max(el2, axis=1, keepdims=True)
    i2 = jnp.min(jnp.where(el2 == t2, lane, LANES), axis=1, keepdims=True)
    ex = jnp.exp(t2 - t1)
    g1 = gw / (1.0 + ex)
    g2 = gw * ex / (1.0 + ex)
    oh = jnp.where((lane == i1) | (lane == i2), 1.0, 0.0)
    r_i = lax.broadcasted_iota(jnp.int32, (tm, tm), 0)
    c_i = lax.broadcasted_iota(jnp.int32, (tm, tm), 1)
    earlier = jnp.where(c_i < r_i, 1.0, 0.0).astype(BF16)
    cum = jnp.dot(earlier, oh.astype(BF16), preferred_element_type=F32) + run_ref[0:1, :]
    r1 = jnp.sum(jnp.where(lane == i1, cum, 0.0), axis=1, keepdims=True).astype(jnp.int32)
    r2 = jnp.sum(jnp.where(lane == i2, cum, 0.0), axis=1, keepdims=True).astype(jnp.int32)
    run = run_ref[0:1, :] + jnp.sum(oh, axis=0, keepdims=True)
    run_ref[...] = jnp.broadcast_to(run, run_ref.shape)
    cnt_ref[...] = jnp.broadcast_to(run, cnt_ref.shape)
    zi = jnp.zeros_like(lane)
    ri_ref[...] = jnp.where(lane == 0, i1 - ROUTE_LANE0, jnp.where(lane == 1, i2 - ROUTE_LANE0,
                            jnp.where(lane == 2, r1, jnp.where(lane == 3, r2, zi))))
    rf_ref[...] = jnp.where(lane == 0, g1, jnp.where(lane == 1, g2, 0.0))


def _router(xa, mod, nw, wg, bg, we, be, n_rows, n_lat_tiles, tiles_per_batch, n_batch):
    D = xa.shape[1]
    tm = ROW_TILE
    pad = LANES - MOE_GROUPS - MOE_EXPERTS
    wr = jnp.concatenate([wg, we, jnp.zeros((D, pad), F32)], axis=1)
    br = jnp.concatenate([bg, be, jnp.zeros((pad,), F32)])[None]
    row = lambda wd: pl.BlockSpec((tm, wd), lambda i: (i, 0))
    full = lambda a: pl.BlockSpec(a.shape, lambda i: (0,) * a.ndim)
    return pl.pallas_call(
        _router_body, grid=(n_rows // tm,),
        in_specs=[row(D), pl.BlockSpec((1,) + mod.shape[1:],
                                       lambda i: (_mod_index(i, n_lat_tiles, tiles_per_batch, n_batch), 0, 0)),
                  full(nw[None]), full(wr), full(br)],
        out_specs=[row(D), row(LANES), row(LANES), pl.BlockSpec((SUBLANES, LANES), lambda i: (0, 0))],
        out_shape=[jax.ShapeDtypeStruct((n_rows, D), F32), jax.ShapeDtypeStruct((n_rows, LANES), jnp.int32),
                   jax.ShapeDtypeStruct((n_rows, LANES), F32), jax.ShapeDtypeStruct((SUBLANES, LANES), F32)],
        scratch_shapes=[pltpu.VMEM((SUBLANES, LANES), F32)],
        compiler_params=_cparams("arbitrary"), name="moe_router",
    )(xa, mod, nw[None], wr, br)


def _plan_body(ri_ref, base_ref, d_ref):
    ri = ri_ref[...]
    lane = lax.broadcasted_iota(jnp.int32, ri.shape, 1)
    base = base_ref[...]
    d = []
    for j in range(2):
        b = jnp.sum(jnp.where(lane == ri[:, j:j + 1], base, 0.0), axis=1, keepdims=True)
        d.append(b + ri[:, 2 + j:3 + j].astype(F32))
    slots = jnp.where(lane == 0, d[0], jnp.where(lane == 1, d[1], 0.0))
    d_ref[0] = slots.T[0:SUBLANES, :].astype(jnp.int32)


def _plan(ri, base):
    n_rows = ri.shape[0]
    tm = ROW_TILE
    basef = jnp.pad(base.astype(F32), (0, LANES - base.shape[0]))[None]
    out = pl.pallas_call(
        _plan_body, grid=(n_rows // tm,),
        in_specs=[pl.BlockSpec((tm, LANES), lambda i: (i, 0)), pl.BlockSpec((1, LANES), lambda i: (0, 0))],
        out_specs=pl.BlockSpec((1, SUBLANES, tm), lambda i: (i, 0, 0)),
        out_shape=jax.ShapeDtypeStruct((n_rows // tm, SUBLANES, tm), jnp.int32),
        compiler_params=_cparams("parallel"), name="moe_plan",
    )(ri, basef)
    return out[:, 0, :], out[:, 1, :]


DMA_UNROLL = 8


def _dispatch_body(d0_ref, d1_ref, lb_ref, nb_ref, h_ref, xb_ref, zbuf, sem, zsem):
    i = pl.program_id(0)
    tm = h_ref.shape[0]
    TB = zbuf.shape[0]
    n_blk = xb_ref.shape[0] // TB

    def zero_copy(blk):
        return pltpu.make_async_copy(zbuf, xb_ref.at[pl.ds(pl.multiple_of(blk * TB, TB), TB)], zsem)

    @pl.when(i == 0)
    def _():
        zbuf[...] = jnp.zeros_like(zbuf)
        for phase in ("start", "wait"):
            def tail(blk, carry, phase=phase):
                getattr(zero_copy(blk), phase)()
                return carry

            for e in range(lb_ref.shape[0]):
                @pl.when(lb_ref[e] >= 0)
                def _(e=e, phase=phase):
                    getattr(zero_copy(lb_ref[e]), phase)()
            lax.fori_loop(nb_ref[0], n_blk, tail, 0)

    def issue(r, carry):
        for j, d_ref in enumerate((d0_ref, d1_ref)):
            pltpu.make_async_copy(h_ref.at[pl.ds(r, 1)], xb_ref.at[pl.ds(d_ref[i, r], 1)], sem).start()
        return carry

    lax.fori_loop(0, tm, issue, 0, unroll=DMA_UNROLL)
    pltpu.make_async_copy(xb_ref.at[pl.ds(0, 2 * tm)], xb_ref.at[pl.ds(0, 2 * tm)], sem).wait()


def _dispatch(d0, d1, last_blk, n_used, h2, n_slot_rows):
    n_rows, D = h2.shape
    tm = ROW_TILE
    return pl.pallas_call(
        _dispatch_body,
        grid_spec=pltpu.PrefetchScalarGridSpec(
            num_scalar_prefetch=4, grid=(n_rows // tm,),
            in_specs=[pl.BlockSpec((tm, D), lambda i, *_: (i, 0))],
            out_specs=pl.BlockSpec(memory_space=pl.ANY),
            scratch_shapes=[pltpu.VMEM((MOE_ROWS, D), F32), pltpu.SemaphoreType.DMA(()),
                            pltpu.SemaphoreType.DMA(())]),
        out_shape=jax.ShapeDtypeStruct((n_slot_rows, D), F32),
        compiler_params=pltpu.CompilerParams(dimension_semantics=("arbitrary",), vmem_limit_bytes=VMEM_LIMIT,
                                             disable_bounds_checks=True),
        name="moe_dispatch",
    )(d0, d1, last_blk, n_used, h2)


def _ffn_body(be_ref, nb_ref, x_ref, w1_ref, w3_ref, w2_ref, y_ref):
    del be_ref
    used = pl.program_id(0) < nb_ref[0]

    @pl.when(used)
    def _():
        xb = x_ref[...].astype(BF16)
        a = jnp.dot(xb, w1_ref[0, 0].astype(BF16), preferred_element_type=F32)
        b = jnp.dot(xb, w3_ref[0, 0].astype(BF16), preferred_element_type=F32)
        hmid = ((a * jax.nn.sigmoid(a)) * b).astype(BF16)
        y_ref[...] = jnp.dot(hmid, w2_ref[0, 0].astype(BF16), preferred_element_type=F32)

    @pl.when(jnp.logical_not(used))
    def _():
        y_ref[...] = jnp.zeros_like(y_ref)


def _expert_ffn(blk_e, n_used, xb, w1, w3, w2, layer):
    P, D = xb.shape
    F = w1.shape[3]
    TB = MOE_ROWS
    rows = pl.BlockSpec((TB, D), lambda i, be, nb: (i, 0))
    return pl.pallas_call(
        _ffn_body,
        grid_spec=pltpu.PrefetchScalarGridSpec(
            num_scalar_prefetch=2, grid=(P // TB,),
            in_specs=[rows, pl.BlockSpec((1, 1, D, F), lambda i, be, nb: (layer, be[i], 0, 0)),
                      pl.BlockSpec((1, 1, D, F), lambda i, be, nb: (layer, be[i], 0, 0)),
                      pl.BlockSpec((1, 1, F, D), lambda i, be, nb: (layer, be[i], 0, 0))],
            out_specs=rows),
        out_shape=jax.ShapeDtypeStruct((P, D), F32),
        compiler_params=_cparams("arbitrary"), name="moe_ffn",
    )(blk_e, n_used, xb, w1, w3, w2)


def _combine_body(d0_ref, d1_ref, x_ref, mod_ref, rf_ref, yb_ref, fw_ref, o_ref, buf, sem, *, final):
    i = pl.program_id(0)
    tm = x_ref.shape[0]

    def issue(r, carry):
        for j, d_ref in enumerate((d0_ref, d1_ref)):
            pltpu.make_async_copy(yb_ref.at[pl.ds(d_ref[i, r], 1)], buf.at[j, pl.ds(r, 1)], sem).start()
        return carry

    lax.fori_loop(0, tm, issue, 0, unroll=DMA_UNROLL)
    pltpu.make_async_copy(buf, buf, sem).wait()
    g = rf_ref[...]
    f = g[:, 0:1] * buf[0] + g[:, 1:2] * buf[1]
    xn = x_ref[...] + mod_ref[0, 5:6, :] * f
    if final:
        ms = jnp.mean(xn * xn, axis=-1, keepdims=True)
        xn = xn * lax.rsqrt(ms + NORM_EPS) * fw_ref[...]
    o_ref[...] = xn


def _combine(d0, d1, xa, mod, rf, yb, final_w, final, n_rows, n_lat_tiles, tiles_per_batch, n_batch):
    D = xa.shape[1]
    tm = ROW_TILE
    row = lambda wd: pl.BlockSpec((tm, wd), lambda i, *_: (i, 0))
    return pl.pallas_call(
        functools.partial(_combine_body, final=final),
        grid_spec=pltpu.PrefetchScalarGridSpec(
            num_scalar_prefetch=2, grid=(n_rows // tm,),
            in_specs=[row(D), pl.BlockSpec((1,) + mod.shape[1:],
                                           lambda i, *_: (_mod_index(i, n_lat_tiles, tiles_per_batch, n_batch), 0, 0)),
                      row(LANES), pl.BlockSpec(memory_space=pl.ANY), pl.BlockSpec((1, D), lambda i, *_: (0, 0))],
            out_specs=row(D),
            scratch_shapes=[pltpu.VMEM((2, tm, D), F32), pltpu.SemaphoreType.DMA(())]),
        out_shape=jax.ShapeDtypeStruct((n_rows, D), F32),
        compiler_params=pltpu.CompilerParams(dimension_semantics=("arbitrary",), vmem_limit_bytes=VMEM_LIMIT,
                                             disable_bounds_checks=True),
        name="moe_combine",
    )(d0, d1, xa, mod, rf, yb, final_w[None])


def _moe(xa, mod, p, expert_w, layer, final_w, final, n_rows, n_lat_tiles, tiles_per_batch, n_batch):
    TB = MOE_ROWS
    h2, ri, rf, cnt = _router(xa, mod, p['norm2_w'], p['moe_wg'], p['moe_bg'], p['moe_we'], p['moe_be'],
                              n_rows, n_lat_tiles, tiles_per_batch, n_batch)
    counts = cnt[0, ROUTE_LANE0:ROUTE_LANE0 + MOE_EXPERTS].astype(jnp.int32)
    pc = (counts + TB - 1) // TB * TB
    pend = jnp.cumsum(pc)
    base = pend - pc
    n_blk = -(-2 * n_rows // TB) + MOE_EXPERTS
    n_used = (pend[-1] // TB).astype(jnp.int32)
    blk = jnp.arange(n_blk, dtype=jnp.int32)
    blk_e = jnp.sum((pend[None, :] <= (jnp.minimum(blk, n_used - 1) * TB)[:, None]).astype(jnp.int32), axis=1)
    blk_e = jnp.minimum(blk_e, MOE_EXPERTS - 1).astype(jnp.int32)
    last_blk = jnp.where(pc > 0, pend // TB - 1, -1).astype(jnp.int32)
    d0, d1 = _plan(ri, base)
    xb = _dispatch(d0, d1, last_blk, n_used[None], h2, n_blk * TB)
    yb = _expert_ffn(blk_e, n_used[None], xb, *expert_w, layer)
    return _combine(d0, d1, xa, mod, rf, yb, final_w, final, n_rows, n_lat_tiles, tiles_per_batch, n_batch)


_LAYER_KEYS = ('ada_w', 'ada_b', 'norm1_w', 'norm2_w', 'w_in', 'b_in', 'w_out', 'hy_conv_w', 'hy_conv_b',
               'hy_filt_w1', 'hy_filt_b1', 'hy_filt_w2', 'hy_filt_b2', 'hy_filt_w3', 'hy_filt_b3', 'hy_sin_freq',
               'hy_bias_d', 'ml_conv_w', 'ml_conv_b', 'ml_norm_w', 'da_lambda', 'da_subln_w', 'moe_wg', 'moe_bg',
               'moe_we', 'moe_be', 'moe_w1', 'moe_w3', 'moe_w2')


def kernel(x, c, ctx, c_ctx, ada_w, ada_b, norm1_w, norm2_w, w_in, b_in, w_out, hy_conv_w, hy_conv_b, hy_filt_w1,
           hy_filt_b1, hy_filt_w2, hy_filt_b2, hy_filt_w3, hy_filt_b3, hy_sin_freq, hy_bias_d, ml_conv_w, ml_conv_b,
           ml_norm_w, da_lambda, da_subln_w, moe_wg, moe_bg, moe_we, moe_be, moe_w1, moe_w3, moe_w2, final_norm_w):
    stacked = dict(zip(_LAYER_KEYS, (ada_w, ada_b, norm1_w, norm2_w, w_in, b_in, w_out, hy_conv_w, hy_conv_b,
                                     hy_filt_w1, hy_filt_b1, hy_filt_w2, hy_filt_b2, hy_filt_w3, hy_filt_b3,
                                     hy_sin_freq, hy_bias_d, ml_conv_w, ml_conv_b, ml_norm_w, da_lambda, da_subln_w,
                                     moe_wg, moe_bg, moe_we, moe_be, moe_w1, moe_w3, moe_w2)))
    B, L, D = x.shape
    Lc = ctx.shape[1]
    depth = ada_w.shape[0]
    tm = ROW_TILE
    assert L % tm == 0 and Lc % tm == 0 and L % GRID_W == 0 and L % ML_CHUNK == 0 and Lc % ML_CHUNK == 0
    ML, MC = B * L, B * Lc
    M = ML + MC
    n_lat, n_ctx = ML // tm, MC // tm
    tpb = L // tm
    regions = ((0, n_lat, tpb), (n_lat, n_ctx, Lc // tm))

    xa = jnp.concatenate([x.reshape(ML, D), ctx.reshape(MC, D)], axis=0)
    R = -(-(B + 1) // SUBLANES) * SUBLANES
    cond = jnp.concatenate([c, c_ctx[None], jnp.zeros((R - B - 1, D), F32)], axis=0)
    cos, sin = _rope_tables(L)

    for l in range(depth):
        last = l == depth - 1
        p = {k: v[l] for k, v in stacked.items() if k not in ('moe_w1', 'moe_w3', 'moe_w2')}
        lam_init = 0.8 - 0.6 * math.exp(-0.3 * l)
        mod = _ada(cond, p['ada_w'], p['ada_b']).reshape(R, 6, D)
        u_hy, u_ml, u_da, g_t = _proj_in(xa, mod, p['norm1_w'], p['w_in'], p['b_in'], n_lat, tpb, B)

        y_hy = _hyena(*_hy_prep(u_hy, p['hy_conv_w'], p['hy_conv_b'], 0, n_lat, tpb), B, L, p)
        q_m, k_t = _ml_prep(u_ml, p['ml_conv_w'], p['ml_conv_b'], regions)
        h_ml = _mlstm(q_m, k_t, u_ml, g_t, B, L, Lc)
        qa, ka, va = _da_prep(u_da, cos, sin, n_lat, tpb)
        y_da = _diff_attn(qa, ka, va, p['da_lambda'], p['da_subln_w'], lam_init, B, 0, L, ((0, L), (ML, Lc)))
        n_rows = ML if last else M
        if not last:
            yc_hy = _hyena(*_hy_prep(u_hy, p['hy_conv_w'], p['hy_conv_b'], n_lat, n_ctx, Lc // tm), B, Lc, p)
            yc_da = _diff_attn(qa, ka, va, p['da_lambda'], p['da_subln_w'], lam_init, B, ML, Lc, ((ML, Lc),))
            y_hy = jnp.concatenate([y_hy, yc_hy], axis=1)
            y_da = jnp.concatenate([y_da, yc_da], axis=0)
        xa = _proj_out(xa, mod, y_hy, h_ml, u_ml, y_da, p['ml_norm_w'], p['w_out'], n_rows, n_lat, tpb, B)
        xa = _moe(xa, mod, p, (moe_w1, moe_w3, moe_w2), l, final_norm_w, last, n_rows, n_lat, tpb, B)
    return xa.reshape(B, L, D)
```

```python
import functools
import math

import numpy as np
import jax
import jax.numpy as jnp
from jax import lax
from jax.experimental import pallas as pl
from jax.experimental.pallas import tpu as pltpu

F32 = jnp.float32
BF16 = jnp.bfloat16

NORM_EPS = 1e-6
GRID_W = 64

HY_CH = 256
HY_HALVES = 2
HY_POS_EMB = 33
HY_FAST_DECAY = 0.3
HY_SLOW_DECAY = 1.5
HY_DECAY_TARGET = 1e-2

ML_HEADS = 4
ML_HD = 64
ML_W = ML_HEADS * ML_HD

DA_HEADS = 4
DA_HD = 64
DA_QK = DA_HEADS * 2 * DA_HD
DA_V = DA_HEADS * 2 * DA_HD
ROPE_THETA = 10000.0

HY_OFF = 0
ML_OFF = HY_OFF + 3 * HY_CH
GATE_OFF = ML_OFF + 4 * ML_W
DA_OFF = GATE_OFF + 2 * 2 * ML_HEADS

MOE_GROUPS = 4
MOE_PER_GROUP = 8
MOE_EXPERTS = MOE_GROUPS * MOE_PER_GROUP

LANES = 128
SUBLANES = 8
VMEM_BYTES_V7X = 64 * 1024 * 1024
VMEM_LIMIT = VMEM_BYTES_V7X * 7 // 8

ROW_TILE = 256
ML_CHUNK = 128
ATT_Q_TILE = 1024
ATT_SUB_TILE = 256
MOE_ROWS = 256


def _cparams(*sem):
    return pltpu.CompilerParams(dimension_semantics=tuple(sem), vmem_limit_bytes=VMEM_LIMIT)


def _dot(a, b):
    return jnp.dot(a.astype(BF16), b.astype(BF16), preferred_element_type=F32)


def _split(a):
    hi = a.astype(BF16)
    lo = (a - hi.astype(F32)).astype(BF16)
    return hi, lo


def _dot3(a, b):
    ah, al = _split(a)
    bh, bl = _split(b)
    d = functools.partial(jnp.dot, preferred_element_type=F32)
    return d(ah, bh) + (d(ah, bl) + d(al, bh))


def _dot2_exact_rhs(a, b_exact):
    ah, al = _split(a)
    d = functools.partial(jnp.dot, preferred_element_type=F32)
    return d(ah, b_exact) + d(al, b_exact)


def _ada_body(a_ref, w_ref, b_ref, o_ref):
    a = a_ref[...]
    a = a * jax.nn.sigmoid(a)
    o_ref[...] = _dot3(a, w_ref[...]) + b_ref[...]


def _ada(cond, w, b):
    R, D = cond.shape
    N = w.shape[1]
    tn = 1536
    return pl.pallas_call(
        _ada_body,
        grid=(N // tn,),
        in_specs=[pl.BlockSpec((R, D), lambda j: (0, 0)),
                  pl.BlockSpec((D, tn), lambda j: (0, j)),
                  pl.BlockSpec((1, tn), lambda j: (0, j))],
        out_specs=pl.BlockSpec((R, tn), lambda j: (0, j)),
        out_shape=jax.ShapeDtypeStruct((R, N), F32),
        compiler_params=_cparams("parallel"), name="ada_mod",
    )(cond, w, b[None])


def _mod_index(i, n_lat_tiles, tiles_per_batch, n_batch):
    return jnp.where(i < n_lat_tiles, i // tiles_per_batch, n_batch)


def _norm_mod(x, nw, shift, scale):
    ms = jnp.mean(x * x, axis=-1, keepdims=True)
    return (x * lax.rsqrt(ms + NORM_EPS) * nw) * (1.0 + scale) + shift


def _proj_in_body(x_ref, mod_ref, nw_ref, why_ref, wml_ref, wda_ref, wgt_ref, bhy_ref, bml_ref, bda_ref, bgt_ref,
                  ohy_ref, oml_ref, oda_ref, ogt_ref):
    h = _norm_mod(x_ref[...], nw_ref[...], mod_ref[0, 0:1, :], mod_ref[0, 1:2, :])
    hb = h.astype(BF16)
    d = functools.partial(jnp.dot, preferred_element_type=F32)
    ohy_ref[...] = d(hb, why_ref[...]) + bhy_ref[...]
    oml_ref[...] = d(hb, wml_ref[...]) + bml_ref[...]
    oda_ref[...] = d(hb, wda_ref[...]) + bda_ref[...]
    ogt_ref[...] = lax.dot_general(wgt_ref[...], hb, (((1,), (1,)), ((), ())),
                                   preferred_element_type=F32) + bgt_ref[...]


def _proj_in(xa, mod, nw, w_in, b_in, n_lat_tiles, tiles_per_batch, n_batch):
    M, D = xa.shape
    tm = ROW_TILE
    wb = w_in.astype(BF16)
    why, wml, wg, wda = wb[:, :ML_OFF], wb[:, ML_OFF:GATE_OFF], wb[:, GATE_OFF:DA_OFF], wb[:, DA_OFF:]
    bhy, bml, bg, bda = b_in[:ML_OFF], b_in[ML_OFF:GATE_OFF], b_in[GATE_OFF:DA_OFF], b_in[DA_OFF:]
    ng = wg.shape[1]
    full = lambda a: pl.BlockSpec(a.shape, lambda i: (0,) * a.ndim)
    args = (xa, mod, nw[None], why, wml, wda, wg.T, bhy[None], bml[None], bda[None], bg[:, None])
    in_specs = [pl.BlockSpec((tm, D), lambda i: (i, 0)),
                pl.BlockSpec((1,) + mod.shape[1:],
                             lambda i: (_mod_index(i, n_lat_tiles, tiles_per_batch, n_batch), 0, 0))]
    in_specs += [full(a) for a in args[2:]]
    widths = (why.shape[1], wml.shape[1], wda.shape[1])
    out_specs = [pl.BlockSpec((tm, wd), lambda i: (i, 0)) for wd in widths]
    out_specs.append(pl.BlockSpec((ng, tm), lambda i: (0, i)))
    out_shape = [jax.ShapeDtypeStruct((M, wd), F32) for wd in widths]
    out_shape.append(jax.ShapeDtypeStruct((ng, M), F32))
    return pl.pallas_call(
        _proj_in_body, grid=(M // tm,), in_specs=in_specs, out_specs=out_specs, out_shape=out_shape,
        compiler_params=_cparams("parallel"), name="proj_in",
    )(*args)


def _conv3(x, prev_row, next_row, w, b, at_start, at_end):
    T = x.shape[0]
    rows = lax.broadcasted_iota(jnp.int32, x.shape, 0)
    prev_row = jnp.where(at_start, 0.0, prev_row)
    next_row = jnp.where(at_end, 0.0, next_row)
    up = jnp.where(rows == 0, prev_row, pltpu.roll(x, 1, 0))
    dn = jnp.where(rows == T - 1, next_row, pltpu.roll(x, T - 1, 0))
    return up * w[0:1, :] + x * w[1:2, :] + dn * w[2:3, :] + b


def _seq_edges(i, regions):
    at_start = jnp.bool_(False)
    at_end = jnp.bool_(False)
    pos = 0
    for (_, n, tps) in regions:
        inside = (i >= pos) & (i < pos + n)
        r = (i - pos) % tps
        at_start = at_start | (inside & (r == 0))
        at_end = at_end | (inside & (r == tps - 1))
        pos += n
    return at_start, at_end


def _seq_tile(i, regions):
    pos = 0
    t = jnp.int32(0)
    for (first, n, _) in regions:
        t = jnp.where((i >= pos) & (i < pos + n), first + (i - pos), t)
        pos += n
    return t


def _halo_specs(tm, width, colblk, regions, n_rows):
    per = tm // SUBLANES
    last8 = n_rows // SUBLANES - 1
    cur = pl.BlockSpec((tm, width), lambda i: (_seq_tile(i, regions), colblk))
    prv = pl.BlockSpec((SUBLANES, width), lambda i: (jnp.maximum(_seq_tile(i, regions) * per - 1, 0), colblk))
    nxt = pl.BlockSpec((SUBLANES, width),
                       lambda i: (jnp.minimum((_seq_tile(i, regions) + 1) * per, last8), colblk))
    return [cur, prv, nxt]


def _hy_prep_body(x_ref, p_ref, n_ref, w_ref, b_ref, v_ref, x1_ref, x2_ref, *, regions):
    at_start, at_end = _seq_edges(pl.program_id(0), regions)
    y = _conv3(x_ref[...], p_ref[SUBLANES - 1:SUBLANES, :], n_ref[0:1, :], w_ref[...], b_ref[...], at_start, at_end)
    for k, ref in enumerate((v_ref, x1_ref, x2_ref)):
        for h in range(HY_HALVES):
            ref[h] = y[:, k * HY_CH + h * LANES:k * HY_CH + (h + 1) * LANES]


def _hy_prep(u_hy, w, b, first_tile, n_tiles, tiles_per_seq):
    M, W = u_hy.shape
    tm = ROW_TILE
    regions = ((first_tile, n_tiles, tiles_per_seq),)
    out = jax.ShapeDtypeStruct((HY_HALVES, n_tiles * tm, LANES), F32)
    return pl.pallas_call(
        functools.partial(_hy_prep_body, regions=regions),
        grid=(n_tiles,),
        in_specs=_halo_specs(tm, W, 0, regions, M) + [pl.BlockSpec((3, W), lambda i: (0, 0)),
                                                      pl.BlockSpec((1, W), lambda i: (0, 0))],
        out_specs=[pl.BlockSpec((HY_HALVES, tm, LANES), lambda i: (0, i, 0))] * 3,
        out_shape=[out] * 3,
        compiler_params=_cparams("parallel"), name="hy_prep",
    )(u_hy, u_hy, u_hy, w, b[None])


def _filt_body(z_ref, w1_ref, b1_ref, w2_ref, b2_ref, fr_ref, w3_ref, b3_ref, env_ref, o_ref):
    f = fr_ref[...]
    h = jnp.sin(f * (_dot3(z_ref[...], w1_ref[...]) + b1_ref[...]))
    h = jnp.sin(f * (_dot3(h, w2_ref[...]) + b2_ref[...]))
    k = _dot3(h, w3_ref[...]) + b3_ref[...]
    e = env_ref[...]
    kf = k[:, :HY_CH] * e
    kb = k[:, HY_CH:] * e
    s = (jnp.sum(jnp.abs(kf), axis=0, keepdims=True) + jnp.sum(jnp.abs(kb), axis=0, keepdims=True)
         - jnp.abs(kb[0:1, :]))
    o_ref[:, :HY_CH] = kf / s
    o_ref[:, HY_CH:] = kb / s


def _hy_filters(L, p):
    t = jnp.linspace(0.0, 1.0, L, dtype=F32)[:, None]
    bands = (HY_POS_EMB - 1) // 2
    w = (2.0 * math.pi / L) * jnp.arange(L, dtype=F32)[:, None]
    f = jnp.linspace(1e-4, bands - 1, bands, dtype=F32)[None, :]
    z = jnp.concatenate([t, jnp.cos(f * w), -jnp.sin(f * w)], axis=-1)
    zp = jnp.pad(z, ((0, 0), (0, LANES - HY_POS_EMB)))
    w1 = jnp.pad(p['hy_filt_w1'], ((0, LANES - HY_POS_EMB), (0, 0)))
    deltas = jnp.abs(jnp.linspace(math.log(HY_DECAY_TARGET) / HY_SLOW_DECAY,
                                  math.log(HY_DECAY_TARGET) / HY_FAST_DECAY, HY_CH, dtype=F32))
    env = jnp.exp(-t * deltas)
    hid = w1.shape[1]
    nout = p['hy_filt_w3'].shape[1]
    n_order = nout // (2 * HY_CH)
    c0 = lambda a: pl.BlockSpec(a.shape, lambda o: (0,) * a.ndim)
    args = (zp, w1, p['hy_filt_b1'][None], p['hy_filt_w2'], p['hy_filt_b2'][None], p['hy_sin_freq'][None],
            p['hy_filt_w3'], p['hy_filt_b3'][None], env)
    in_specs = [c0(a) for a in args[:6]]
    in_specs += [pl.BlockSpec((hid, 2 * HY_CH), lambda o: (0, o)), pl.BlockSpec((1, 2 * HY_CH), lambda o: (0, o)),
                 c0(env)]
    return pl.pallas_call(
        _filt_body, grid=(n_order,), in_specs=in_specs,
        out_specs=pl.BlockSpec((L, 2 * HY_CH), lambda o: (0, o)),
        out_shape=jax.ShapeDtypeStruct((L, nout), F32),
        compiler_params=_cparams("parallel"), name="hy_filter",
    )(*args)


def _fft_plan(L):
    N = 2 * L
    Bn = 128 if N >= 4096 else 16
    A = N // Bn
    assert A * Bn == N and A % 16 == 0
    return A, Bn


def _dft_small(A, N):
    ka = np.arange(A)[:, None]
    a = np.arange(A)[None, :]
    th = 2.0 * np.pi * ((ka * a) % A) / A
    d1 = np.concatenate([np.cos(th), -np.sin(th)], axis=0)
    d4 = np.concatenate([np.cos(th.T), -np.sin(th.T)], axis=1) / N
    return jnp.asarray(d1, F32), jnp.asarray(d4, F32)


def _dft_mid(A, Bn):
    N = A * Bn
    ka = jnp.arange(A, dtype=jnp.int32)[:, None, None]
    kb = jnp.arange(Bn, dtype=jnp.int32)[None, :, None]
    b = jnp.arange(Bn, dtype=jnp.int32)[None, None, :]
    m = (b * (kb * A + ka)) % N
    ph = m.astype(F32) * (2.0 * math.pi / N)
    c, s = jnp.cos(ph), jnp.sin(ph)
    mf = jnp.concatenate([jnp.concatenate([c, s], axis=2), jnp.concatenate([-s, c], axis=2)], axis=1)
    ct, st = jnp.swapaxes(c, 1, 2), jnp.swapaxes(s, 1, 2)
    mi = jnp.concatenate([jnp.concatenate([ct, -st], axis=2), jnp.concatenate([st, ct], axis=2)], axis=1)
    return mf.astype(BF16), mi.astype(BF16)


FFT_BT = 16
FFT_GROUP = 8


def _fft1_body(d_ref, x_ref, o_ref, *, Bn):
    _, A2, bt, C = o_ref.shape
    Ain = d_ref.shape[1] // bt
    b0 = pl.multiple_of(pl.program_id(1) * bt, bt)
    slabs = [jnp.concatenate([x_ref[h, pl.ds(a * Bn + b0, bt), :] for h in range(HY_HALVES)], axis=1)
             for a in range(Ain)]
    xblk = jnp.concatenate(slabs, axis=0).astype(BF16)
    o_ref[0] = jnp.dot(d_ref[...], xblk, preferred_element_type=F32).reshape(A2, bt, C)


def _fft1(d1, x, n_seq, Bn):
    A2, Ain = d1.shape
    rows = Ain * Bn
    bt = min(FFT_BT, Bn)
    dbig = jnp.kron(d1, jnp.eye(bt, dtype=F32)).astype(BF16)
    return pl.pallas_call(
        functools.partial(_fft1_body, Bn=Bn), grid=(n_seq, Bn // bt),
        in_specs=[pl.BlockSpec(dbig.shape, lambda s, j: (0, 0)),
                  pl.BlockSpec((HY_HALVES, rows, LANES), lambda s, j: (0, s, 0))],
        out_specs=pl.BlockSpec((1, A2, bt, HY_CH), lambda s, j: (s, 0, j, 0)),
        out_shape=jax.ShapeDtypeStruct((n_seq, A2, Bn, HY_CH), F32),
        compiler_params=_cparams("parallel", "arbitrary"), name="hy_dft_slow",
    )(dbig, x)


def _fast_operand(p_ref, j):
    return jnp.concatenate([p_ref[0, 0, j], p_ref[0, 1, j]], axis=0).astype(BF16)


def _fft_spec_body(p_ref, mf_ref, k_ref, *, Bn):
    for j in range(FFT_GROUP):
        X = jnp.dot(mf_ref[j], _fast_operand(p_ref, j), preferred_element_type=F32)
        k_ref[0, j, 0] = X[:Bn]
        k_ref[0, j, 1] = X[Bn:]


def _fft_spec(p5, mf):
    S, _, A, Bn, C = p5.shape
    G = FFT_GROUP
    return pl.pallas_call(
        functools.partial(_fft_spec_body, Bn=Bn), grid=(A // G, S),
        in_specs=[pl.BlockSpec((1, 2, G, Bn, C), lambda g, s: (s, 0, g, 0, 0)),
                  pl.BlockSpec((G, 2 * Bn, 2 * Bn), lambda g, s: (g, 0, 0))],
        out_specs=pl.BlockSpec((1, G, 2, Bn, C), lambda g, s: (s, g, 0, 0, 0)),
        out_shape=jax.ShapeDtypeStruct((S, A, 2, Bn, C), F32),
        compiler_params=_cparams("parallel", "parallel"), name="hy_filter_spec",
    )(p5, mf)


def _fft_mid_body(p_ref, mf_ref, mi_ref, k_ref, q_ref, *, Bn):
    for j in range(FFT_GROUP):
        X = jnp.dot(mf_ref[j], _fast_operand(p_ref, j), preferred_element_type=F32)
        xr, xi = X[:Bn], X[Bn:]
        kr, ki = k_ref[j, 0], k_ref[j, 1]
        Y = jnp.concatenate([xr * kr - xi * ki, xr * ki + xi * kr], axis=0).astype(BF16)
        Q = jnp.dot(mi_ref[j], Y, preferred_element_type=F32)
        q_ref[0, j, 0] = Q[:Bn]
        q_ref[0, j, 1] = Q[Bn:]


def _fft_mid(p5, mf, mi, kspec):
    S, _, A, Bn, C = p5.shape
    G = FFT_GROUP
    mat = pl.BlockSpec((G, 2 * Bn, 2 * Bn), lambda g, s: (g, 0, 0))
    return pl.pallas_call(
        functools.partial(_fft_mid_body, Bn=Bn), grid=(A // G, S),
        in_specs=[pl.BlockSpec((1, 2, G, Bn, C), lambda g, s: (s, 0, g, 0, 0)), mat, mat,
                  pl.BlockSpec((G, 2, Bn, C), lambda g, s: (g, 0, 0, 0))],
        out_specs=pl.BlockSpec((1, G, 2, Bn, C), lambda g, s: (s, g, 0, 0, 0)),
        out_shape=jax.ShapeDtypeStruct((S, A, 2, Bn, C), F32),
        compiler_params=_cparams("parallel", "parallel"), name="hy_dft_fast",
    )(p5, mf, mi, kspec)


def _fft4_body(d_ref, q_ref, z_ref, g_ref, dch_ref, o_ref, *, Bn):
    _, A2, bt, C = q_ref.shape
    Ah = d_ref.shape[0] // bt
    b0 = pl.multiple_of(pl.program_id(1) * bt, bt)
    qblk = q_ref[0].reshape(A2 * bt, C).astype(BF16)
    y = jnp.dot(d_ref[...], qblk, preferred_element_type=F32)
    dch = dch_ref[...]
    for a in range(Ah):
        rows = pl.ds(a * Bn + b0, bt)
        ya = y[a * bt:(a + 1) * bt]
        for h in range(HY_HALVES):
            lanes = slice(h * LANES, (h + 1) * LANES)
            o_ref[h, rows, :] = g_ref[h, rows, :] * (ya[:, lanes] + z_ref[h, rows, :] * dch[:, lanes])


def _fft4(d4, q4, z, gate, dch):
    S, A2, Bn, C = q4.shape
    Ah = d4.shape[0]
    L = Ah * Bn
    bt = min(FFT_BT, Bn)
    dbig = jnp.kron(d4, jnp.eye(bt, dtype=F32)).astype(BF16)
    sig = pl.BlockSpec((HY_HALVES, L, LANES), lambda s, j: (0, s, 0))
    return pl.pallas_call(
        functools.partial(_fft4_body, Bn=Bn), grid=(S, Bn // bt),
        in_specs=[pl.BlockSpec(dbig.shape, lambda s, j: (0, 0)),
                  pl.BlockSpec((1, A2, bt, C), lambda s, j: (s, 0, j, 0)), sig, sig,
                  pl.BlockSpec((1, C), lambda s, j: (0, 0))],
        out_specs=sig,
        out_shape=jax.ShapeDtypeStruct((HY_HALVES, S * L, LANES), F32),
        compiler_params=_cparams("parallel", "arbitrary"), name="hy_idft_gate",
    )(dbig, q4, z, gate, dch)


def _hyena(v, x1, x2, n_seq, L, p):
    C = HY_CH
    A, Bn = _fft_plan(L)
    Ah = A // 2
    d1, d4 = _dft_small(A, A * Bn)
    mf, mi = _dft_mid(A, Bn)
    d4 = d4.reshape(A, 2, A).transpose(0, 2, 1).reshape(A, 2 * A)
    kn = _hy_filters(L, p).reshape(L, -1, 2, C)
    n_order = kn.shape[1]
    zero = jnp.zeros((1, C), F32)
    k2 = jnp.concatenate([piece for o in range(n_order)
                          for piece in (kn[:, o, 0], zero, jnp.flip(kn[1:, o, 1], axis=0))], axis=0)
    k2 = k2.reshape(-1, HY_HALVES, LANES).transpose(1, 0, 2)
    kspec = _fft_spec(_fft1(d1, k2, n_order, Bn).reshape(n_order, 2, A, Bn, C), mf)
    z = v
    for o, gate in enumerate((x1, x2)):
        P = _fft1(d1[:, :Ah], z, n_seq, Bn).reshape(n_seq, 2, A, Bn, C)
        Q = _fft_mid(P, mf, mi, kspec[o]).reshape(n_seq, 2 * A, Bn, C)
        z = _fft4(d4[:Ah], Q, z, gate, p['hy_bias_d'][o][None])
    return z


def _ml_prep_body(x_ref, p_ref, n_ref, w_ref, b_ref, q_ref, kt_ref, *, regions):
    at_start, at_end = _seq_edges(pl.program_id(0), regions)
    y = _conv3(x_ref[...], p_ref[SUBLANES - 1:SUBLANES, :], n_ref[0:1, :], w_ref[...], b_ref[...], at_start, at_end)
    y = y * jax.nn.sigmoid(y)
    q_ref[...] = y[:, :ML_W].astype(BF16)
    kt_ref[...] = (y[:, ML_W:] * (ML_HD ** -0.5)).T


def _ml_prep(u_ml, w, b, regions):
    M = u_ml.shape[0]
    tm = ROW_TILE
    n = sum(r[1] for r in regions)
    return pl.pallas_call(
        functools.partial(_ml_prep_body, regions=regions), grid=(n,),
        in_specs=_halo_specs(tm, 2 * ML_W, 0, regions, M) + [pl.BlockSpec((3, 2 * ML_W), lambda i: (0, 0)),
                                                             pl.BlockSpec((1, 2 * ML_W), lambda i: (0, 0))],
        out_specs=[pl.BlockSpec((tm, ML_W), lambda i: (_seq_tile(i, regions), 0)),
                   pl.BlockSpec((ML_W, tm), lambda i: (0, _seq_tile(i, regions)))],
        out_shape=[jax.ShapeDtypeStruct((M, ML_W), BF16), jax.ShapeDtypeStruct((ML_W, M), F32)],
        compiler_params=_cparams("parallel"), name="ml_prep",
    )(u_ml, u_ml, u_ml, w, b[None])


def _ml_body(*refs):
    ins, outs, scr = refs[:8], refs[8:10], refs[10:]

    @pl.when(pl.program_id(1) == 0)
    def _():
        for ref in scr:
            ref[...] = jnp.zeros_like(ref)

    for d in range(2):
        _ml_chunk(d, *ins[4 * d:4 * d + 4], outs[d], *scr[2 * d:2 * d + 2])


def _ml_chunk(d, q_ref, kt_ref, v_ref, g_ref, o_ref, st_ref, m_ref):
    T = q_ref.shape[0]
    H, W = ML_HEADS, ML_W
    WA = W + LANES
    sgn = 1 - 2 * d
    g = g_ref[...]
    gs = g[2 * H * d:2 * H * (d + 1)]
    ig = gs[0:H]
    lf8 = -(jnp.maximum(-gs, 0.0) + jnp.log1p(jnp.exp(-jnp.abs(gs))))
    lf = lf8[H:2 * H]
    r_i = lax.broadcasted_iota(jnp.int32, (T, T), 0)
    c_i = lax.broadcasted_iota(jnp.int32, (T, T), 1)
    prec = ((c_i - r_i) * sgn) <= 0
    incl = jnp.where(((r_i - c_i) * sgn) <= 0, 1.0, 0.0).astype(BF16)
    after = jnp.where(((c_i - r_i) * sgn) < 0, 1.0, 0.0).astype(BF16)
    b_rows = _dot2_exact_rhs(lf8, incl)[H:2 * H]
    bL = jnp.sum(lf, axis=1, keepdims=True)
    a_row = bL - b_rows + ig
    m_loc = jnp.max(a_row, axis=1, keepdims=True)
    w_row = jnp.exp(a_row - m_loc)
    m0 = m_ref[0:H, 0:1]
    m_new = jnp.maximum(bL + m0, m_loc)
    s_old = jnp.exp(bL + m0 - m_new)
    s_loc = jnp.exp(m_loc - m_new)

    qb = q_ref[...]
    kt = kt_ref[...]
    row_head = lax.broadcasted_iota(jnp.int32, (W, T), 0) // ML_HD
    lane = lax.broadcasted_iota(jnp.int32, (1, WA), 1)
    lane_head = jnp.where(lane < W, lane // ML_HD, lane - W)
    v_aug = jnp.concatenate([v_ref[...], jnp.ones((T, LANES), F32)], axis=1)

    ps, vbd = [], []
    w_inter = jnp.zeros((T, WA), F32)
    e_m = jnp.zeros((T, WA), F32)
    for h in range(H):
        lf_h = lf[h:h + 1]
        Lf = jnp.where(prec, lf_h, 0.0)
        b_col = jnp.sum(Lf, axis=1, keepdims=True)
        E = _dot2_exact_rhs(Lf, after)
        Dm = jnp.where(prec, E + ig[h:h + 1], -jnp.inf)
        inter = b_col + m0[h:h + 1]
        m_col = jnp.maximum(inter, jnp.max(Dm, axis=1, keepdims=True))
        kth = jnp.where(row_head == h, kt, 0.0).astype(BF16)
        S = jnp.dot(qb, kth, preferred_element_type=F32)
        ps.append((jnp.exp(Dm - m_col) * S).astype(BF16))
        sel = lane_head == h
        vbd.append(jnp.where(sel, v_aug, 0.0).astype(BF16))
        w_inter = w_inter + jnp.where(sel, jnp.exp(inter - m_col), 0.0)
        e_m = e_m + jnp.where(sel, jnp.exp(-m_col), 0.0)
    nd = jnp.dot(jnp.concatenate(ps, axis=1), jnp.concatenate(vbd, axis=0), preferred_element_type=F32)
    nd = nd + w_inter * jnp.dot(qb, st_ref[...].astype(BF16), preferred_element_type=F32)
    den = jnp.zeros((T, W), F32)
    for h in range(H):
        den = den + jnp.where(lane_head[:, :W] == h, nd[:, W + h:W + h + 1], 0.0)
    o_ref[...] = nd[:, :W] / jnp.maximum(jnp.abs(den), e_m[:, :W])

    wk = jnp.zeros((W, T), F32)
    scol = jnp.zeros((1, WA), F32)
    ws = w_row * s_loc
    for h in range(H):
        wk = wk + jnp.where(row_head == h, ws[h:h + 1], 0.0)
        scol = scol + jnp.where(lane_head == h, s_old[h:h + 1], 0.0)
    st_loc = jnp.dot((kt * wk).astype(BF16), v_aug.astype(BF16), preferred_element_type=F32)
    diag = (lax.broadcasted_iota(jnp.int32, (W, WA), 0) // ML_HD) == lane_head
    st_ref[...] = jnp.where(diag, st_ref[...] * scol + st_loc, 0.0)
    m_ref[0:H, :] = jnp.broadcast_to(m_new, (H, LANES))


def _mlstm(q, kt, u_ml, g_t, B, L, Lc):
    M = q.shape[0]
    T = ML_CHUNK
    nC, nL = Lc // T, L // T
    lat0 = 0
    ctx0 = (B * L) // T

    def blk(d, b, i):
        cc = i if d == 0 else nC - 1 - i
        j = i - nC
        cl = j if d == 0 else nL - 1 - j
        return jnp.where(i < nC, ctx0 + b * nC + cc, lat0 + b * nL + cl)

    in_specs, out_specs = [], []
    for d in range(2):
        rows = functools.partial(lambda b, i, d, c: (blk(d, b, i), c), d=d)
        cols = functools.partial(lambda b, i, d: (0, blk(d, b, i)), d=d)
        in_specs += [pl.BlockSpec((T, ML_W), functools.partial(rows, c=0)), pl.BlockSpec((ML_W, T), cols),
                     pl.BlockSpec((T, ML_W), functools.partial(rows, c=2)), pl.BlockSpec((4 * ML_HEADS, T), cols)]
        out_specs.append(pl.BlockSpec((T, ML_W), functools.partial(rows, c=0)))
    state = [pltpu.VMEM((ML_W, ML_W + LANES), F32), pltpu.VMEM((SUBLANES, LANES), F32)]
    return pl.pallas_call(
        _ml_body, grid=(B, nC + nL), in_specs=in_specs, out_specs=out_specs,
        out_shape=[jax.ShapeDtypeStruct((M, ML_W), F32)] * 2,
        scratch_shapes=state * 2,
        compiler_params=_cparams("parallel", "arbitrary"), name="mlstm",
    )(*([q, kt, u_ml, g_t] * 2))


def _rope_tables(L):
    rows = L // GRID_W
    row = jnp.repeat(jnp.arange(rows, dtype=F32), GRID_W)
    col = jnp.tile(jnp.arange(GRID_W, dtype=F32), rows)
    half = DA_HD // 2
    inv = ROPE_THETA ** (-jnp.arange(0, half, 2, dtype=F32) / half)
    ang = jnp.stack([row, col], axis=-1)[:, :, None] * inv
    ang = jnp.stack([ang, ang], axis=-2).reshape(-1, DA_HD)
    ang = jnp.concatenate([ang, ang], axis=1)
    return jnp.cos(ang), jnp.sin(ang)


def _da_prep_body(u_ref, cos_ref, sin_ref, q_ref, k_ref, va_ref, *, n_lat_tiles):
    is_lat = pl.program_id(0) < n_lat_tiles
    u = u_ref[...]
    reps = DA_QK // cos_ref.shape[1]
    cs = jnp.concatenate([cos_ref[...]] * reps, axis=1)
    sn = jnp.concatenate([sin_ref[...]] * reps, axis=1)
    q4 = DA_HD // 4
    src = lax.broadcasted_iota(jnp.int32, (LANES, LANES), 0)
    dst = lax.broadcasted_iota(jnp.int32, (LANES, LANES), 1)
    first = (dst % (2 * q4)) < q4
    perm = jnp.where(first & (src == dst + q4), -1.0, jnp.where((~first) & (src == dst - q4), 1.0, 0.0)).astype(BF16)

    def rope(x):
        rot = jnp.concatenate([_dot2_exact_rhs(x[:, g * LANES:(g + 1) * LANES], perm)
                               for g in range(x.shape[1] // LANES)], axis=1)
        return jnp.where(is_lat, x * cs + rot * sn, x)

    q_ref[...] = (rope(u[:, :DA_QK]) * (DA_HD ** -0.5 * math.log2(math.e))).astype(BF16)
    k_ref[...] = rope(u[:, DA_QK:2 * DA_QK]).astype(BF16)
    va_ref[...] = u[:, 2 * DA_QK:].T.astype(BF16)


def _da_prep(u_da, cos, sin, n_lat_tiles, tiles_per_seq):
    M = u_da.shape[0]
    tm = ROW_TILE
    tab = pl.BlockSpec((tm, cos.shape[1]), lambda i: (jnp.where(i < n_lat_tiles, i % tiles_per_seq, 0), 0))
    out = pl.BlockSpec((tm, DA_QK), lambda i: (i, 0))
    return pl.pallas_call(
        functools.partial(_da_prep_body, n_lat_tiles=n_lat_tiles), grid=(M // tm,),
        in_specs=[pl.BlockSpec((tm, u_da.shape[1]), lambda i: (i, 0)), tab, tab],
        out_specs=[out, out, pl.BlockSpec((DA_V, tm), lambda i: (0, i))],
        out_shape=[jax.ShapeDtypeStruct((M, DA_QK), BF16)] * 2 + [jax.ShapeDtypeStruct((DA_V, M), BF16)],
        compiler_params=_cparams("parallel"), name="da_prep",
    )(u_da, cos, sin)


def _da_body(lam_ref, w_ref, q_ref, *rest, nseg, lam_init, sub):
    ks, vas, o_ref = rest[:nseg], rest[nseg:2 * nseg], rest[2 * nseg]
    lp = lam_ref[...]
    lam = (jnp.exp(jnp.sum(lp[0:1] * lp[1:2], axis=1, keepdims=True))
           - jnp.exp(jnp.sum(lp[2:3] * lp[3:4], axis=1, keepdims=True)) + lam_init)
    HW = 2 * DA_HD
    lane = lax.broadcasted_iota(jnp.int32, (1, HW), 1)
    n_sub = q_ref.shape[0] // sub
    def scores(i):
        t, m = divmod(i, 2)
        q = q_ref[t * sub:(t + 1) * sub, :]
        qm = jnp.where((lane // DA_HD) == m, q, jnp.zeros_like(q))
        return [lax.dot_general(k[...], qm, (((1,), (1,)), ((), ())), preferred_element_type=F32) for k in ks]

    def softmax(ss):
        mx = functools.reduce(jnp.maximum, [jnp.max(s, axis=0, keepdims=True) for s in ss])
        ps = [jnp.exp2(s - mx) for s in ss]
        den = functools.reduce(jnp.add, [jnp.sum(pr, axis=0, keepdims=True) for pr in ps])
        return [pr.astype(BF16) for pr in ps], den

    def values(ps, den):
        acc = functools.reduce(jnp.add, [jnp.dot(va[...], pr, preferred_element_type=F32)
                                         for pr, va in zip(ps, vas)])
        return acc / den

    n_streams = 2 * n_sub
    sss = {0: scores(0)}
    if n_streams > 1:
        sss[1] = scores(1)
    outs = []
    for i in range(n_streams):
        pd = softmax(sss.pop(i))
        if i + 2 < n_streams:
            sss[i + 2] = scores(i + 2)
        outs.append(values(*pd))
    for t in range(n_sub):
        o = outs[2 * t] - lam * outs[2 * t + 1]
        ms = jnp.mean(o * o, axis=0, keepdims=True)
        o_ref[t * sub:(t + 1) * sub, :] = ((o * lax.rsqrt(ms + NORM_EPS) * w_ref[...]) * (1.0 - lam_init)).T


def _diff_attn(qa, ka, vaa, lam_p, subln_w, lam_init, n_batch, q_rows0, q_len, segs):
    tq = min(ATT_Q_TILE, q_len)
    HW = 2 * DA_HD
    nq = q_len // tq
    q0 = q_rows0 // tq
    k_specs = [pl.BlockSpec((n, HW), functools.partial(lambda b, h, i, f, n: (f // n + b, h), f=f, n=n))
               for (f, n) in segs]
    va_specs = [pl.BlockSpec((HW, n), functools.partial(lambda b, h, i, f, n: (h, f // n + b), f=f, n=n))
                for (f, n) in segs]
    return pl.pallas_call(
        functools.partial(_da_body, nseg=len(segs), lam_init=lam_init, sub=min(ATT_SUB_TILE, tq)),
        grid=(n_batch, DA_HEADS, nq),
        in_specs=[pl.BlockSpec(lam_p.shape, lambda b, h, i: (0, 0)), pl.BlockSpec((HW, 1), lambda b, h, i: (0, 0)),
                  pl.BlockSpec((tq, HW), lambda b, h, i: (q0 + b * nq + i, h))] + k_specs + va_specs,
        out_specs=pl.BlockSpec((tq, HW), lambda b, h, i: (b * nq + i, h)),
        out_shape=jax.ShapeDtypeStruct((n_batch * q_len, DA_V), F32),
        compiler_params=_cparams("parallel", "parallel", "parallel"), name="diff_attn",
    )(lam_p, subln_w[:, None], qa, *([ka] * len(segs)), *([vaa] * len(segs)))


def _proj_out_body(x_ref, mod_ref, hy_ref, hf_ref, hb_ref, og_ref, da_ref, mw_ref, why_ref, wml_ref, wda_ref, o_ref):
    hs = hf_ref[...] + hb_ref[...]
    W = hs.shape[1]
    r = lax.broadcasted_iota(jnp.int32, (W, W), 0) // ML_HD
    c = lax.broadcasted_iota(jnp.int32, (W, W), 1) // ML_HD
    same_head = jnp.where(r == c, 1.0, 0.0).astype(BF16)
    ms = _dot2_exact_rhs(hs * hs, same_head) * (1.0 / ML_HD)
    y_ml = jax.nn.sigmoid(og_ref[...]) * (hs * lax.rsqrt(ms + NORM_EPS) * mw_ref[...])
    y_hy = jnp.concatenate([hy_ref[h] for h in range(HY_HALVES)], axis=1)
    y = (_dot(y_hy, why_ref[...]) + _dot(y_ml, wml_ref[...])) + _dot(da_ref[...], wda_ref[...])
    o_ref[...] = x_ref[...] + mod_ref[0, 2:3, :] * y


def _proj_out(xa, mod, y_hy, h_ml, u_ml, y_da, ml_norm_w, w_out, n_rows, n_lat_tiles, tiles_per_batch, n_batch):
    D = xa.shape[1]
    tm = ROW_TILE
    wb = w_out.astype(BF16)
    why, wml, wda = wb[:HY_CH], wb[HY_CH:HY_CH + ML_W], wb[HY_CH + ML_W:]
    full = lambda a: pl.BlockSpec(a.shape, lambda i: (0,) * a.ndim)
    row = lambda wd, cb=0: pl.BlockSpec((tm, wd), lambda i: (i, cb))
    mw = ml_norm_w[None]
    return pl.pallas_call(
        _proj_out_body, grid=(n_rows // tm,),
        in_specs=[row(D), pl.BlockSpec((1,) + mod.shape[1:],
                                       lambda i: (_mod_index(i, n_lat_tiles, tiles_per_batch, n_batch), 0, 0)),
                  pl.BlockSpec((HY_HALVES, tm, LANES), lambda i: (0, i, 0)),
                  row(ML_W), row(ML_W), row(ML_W, 3), row(DA_V),
                  full(mw), full(why), full(wml), full(wda)],
        out_specs=row(D),
        out_shape=jax.ShapeDtypeStruct((n_rows, D), F32),
        compiler_params=_cparams("parallel"), name="proj_out",
    )(xa, mod, y_hy, *h_ml, u_ml, y_da, mw, why, wml, wda)


ROUTE_LANE0 = MOE_GROUPS


def _router_body(x_ref, mod_ref, nw_ref, wr_ref, br_ref, h_ref, ri_ref, rf_ref, cnt_ref, run_ref):
    @pl.when(pl.program_id(0) == 0)
    def _():
        run_ref[...] = jnp.zeros_like(run_ref)

    h = _norm_mod(x_ref[...], nw_ref[...], mod_ref[0, 3:4, :], mod_ref[0, 4:5, :])
    h_ref[...] = h
    lg = _dot3(h, wr_ref[...]) + br_ref[...]
    tm = lg.shape[0]
    lane = lax.broadcasted_iota(jnp.int32, lg.shape, 1)
    neg = -jnp.inf
    is_g = lane < MOE_GROUPS
    gl = jnp.where(is_g, lg, neg)
    gmax = jnp.max(gl, axis=1, keepdims=True)
    gidx = jnp.min(jnp.where(gl == gmax, lane, LANES), axis=1, keepdims=True)
    gw = 1.0 / jnp.sum(jnp.where(is_g, jnp.exp(gl - gmax), 0.0), axis=1, keepdims=True)
    e_of = lane - ROUTE_LANE0
    in_grp = (e_of >= 0) & (e_of < MOE_EXPERTS) & ((e_of // MOE_PER_GROUP) == gidx)
    el = jnp.where(in_grp, lg, neg)
    t1 = jnp.max(el, axis=1, keepdims=True)
    i1 = jnp.min(jnp.where(el == t1, lane, LANES), axis=1, keepdims=True)
    el2 = jnp.where(lane == i1, neg, el)
    t2 = jnp.max(el2, axis=1, keepdims=True)
    i2 = jnp.min(jnp.where(el2 == t2, lane, LANES), axis=1, keepdims=True)
    ex = jnp.exp(t2 - t1)
    g1 = gw / (1.0 + ex)
    g2 = gw * ex / (1.0 + ex)
    oh = jnp.where((lane == i1) | (lane == i2), 1.0, 0.0)
    r_i = lax.broadcasted_iota(jnp.int32, (tm, tm), 0)
    c_i = lax.broadcasted_iota(jnp.int32, (tm, tm), 1)
    earlier = jnp.where(c_i < r_i, 1.0, 0.0).astype(BF16)
    cum = jnp.dot(earlier, oh.astype(BF16), preferred_element_type=F32) + run_ref[0:1, :]
    r1 = jnp.sum(jnp.where(lane == i1, cum, 0.0), axis=1, keepdims=True).astype(jnp.int32)
    r2 = jnp.sum(jnp.where(lane == i2, cum, 0.0), axis=1, keepdims=True).astype(jnp.int32)
    run = run_ref[0:1, :] + jnp.sum(oh, axis=0, keepdims=True)
    run_ref[...] = jnp.broadcast_to(run, run_ref.shape)
    cnt_ref[...] = jnp.broadcast_to(run, cnt_ref.shape)
    zi = jnp.zeros_like(lane)
    ri_ref[...] = jnp.where(lane == 0, i1 - ROUTE_LANE0, jnp.where(lane == 1, i2 - ROUTE_LANE0,
                            jnp.where(lane == 2, r1, jnp.where(lane == 3, r2, zi))))
    rf_ref[...] = jnp.where(lane == 0, g1, jnp.where(lane == 1, g2, 0.0))


def _router(xa, mod, nw, wg, bg, we, be, n_rows, n_lat_tiles, tiles_per_batch, n_batch):
    D = xa.shape[1]
    tm = ROW_TILE
    pad = LANES - MOE_GROUPS - MOE_EXPERTS
    wr = jnp.concatenate([wg, we, jnp.zeros((D, pad), F32)], axis=1)
    br = jnp.concatenate([bg, be, jnp.zeros((pad,), F32)])[None]
    row = lambda wd: pl.BlockSpec((tm, wd), lambda i: (i, 0))
    full = lambda a: pl.BlockSpec(a.shape, lambda i: (0,) * a.ndim)
    return pl.pallas_call(
        _router_body, grid=(n_rows // tm,),
        in_specs=[row(D), pl.BlockSpec((1,) + mod.shape[1:],
                                       lambda i: (_mod_index(i, n_lat_tiles, tiles_per_batch, n_batch), 0, 0)),
                  full(nw[None]), full(wr), full(br)],
        out_specs=[row(D), row(LANES), row(LANES), pl.BlockSpec((SUBLANES, LANES), lambda i: (0, 0))],
        out_shape=[jax.ShapeDtypeStruct((n_rows, D), F32), jax.ShapeDtypeStruct((n_rows, LANES), jnp.int32),
                   jax.ShapeDtypeStruct((n_rows, LANES), F32), jax.ShapeDtypeStruct((SUBLANES, LANES), F32)],
        scratch_shapes=[pltpu.VMEM((SUBLANES, LANES), F32)],
        compiler_params=_cparams("arbitrary"), name="moe_router",
    )(xa, mod, nw[None], wr, br)


def _plan_body(ri_ref, base_ref, d_ref):
    ri = ri_ref[...]
    lane = lax.broadcasted_iota(jnp.int32, ri.shape, 1)
    base = base_ref[...]
    d = []
    for j in range(2):
        b = jnp.sum(jnp.where(lane == ri[:, j:j + 1], base, 0.0), axis=1, keepdims=True)
        d.append(b + ri[:, 2 + j:3 + j].astype(F32))
    slots = jnp.where(lane == 0, d[0], jnp.where(lane == 1, d[1], 0.0))
    d_ref[0] = slots.T[0:SUBLANES, :].astype(jnp.int32)


def _plan(ri, base):
    n_rows = ri.shape[0]
    tm = ROW_TILE
    basef = jnp.pad(base.astype(F32), (0, LANES - base.shape[0]))[None]
    out = pl.pallas_call(
        _plan_body, grid=(n_rows // tm,),
        in_specs=[pl.BlockSpec((tm, LANES), lambda i: (i, 0)), pl.BlockSpec((1, LANES), lambda i: (0, 0))],
        out_specs=pl.BlockSpec((1, SUBLANES, tm), lambda i: (i, 0, 0)),
        out_shape=jax.ShapeDtypeStruct((n_rows // tm, SUBLANES, tm), jnp.int32),
        compiler_params=_cparams("parallel"), name="moe_plan",
    )(ri, basef)
    return out[:, 0, :], out[:, 1, :]


DMA_UNROLL = 8


def _dispatch_body(d0_ref, d1_ref, lb_ref, nb_ref, h_ref, xb_ref, zbuf, sem, zsem):
    i = pl.program_id(0)
    tm = h_ref.shape[0]
    TB = zbuf.shape[0]
    n_blk = xb_ref.shape[0] // TB

    def zero_copy(blk):
        return pltpu.make_async_copy(zbuf, xb_ref.at[pl.ds(pl.multiple_of(blk * TB, TB), TB)], zsem)

    @pl.when(i == 0)
    def _():
        zbuf[...] = jnp.zeros_like(zbuf)
        for phase in ("start", "wait"):
            def tail(blk, carry, phase=phase):
                getattr(zero_copy(blk), phase)()
                return carry

            for e in range(lb_ref.shape[0]):
                @pl.when(lb_ref[e] >= 0)
                def _(e=e, phase=phase):
                    getattr(zero_copy(lb_ref[e]), phase)()
            lax.fori_loop(nb_ref[0], n_blk, tail, 0)

    def issue(r, carry):
        for j, d_ref in enumerate((d0_ref, d1_ref)):
            pltpu.make_async_copy(h_ref.at[pl.ds(r, 1)], xb_ref.at[pl.ds(d_ref[i, r], 1)], sem).start()
        return carry

    lax.fori_loop(0, tm, issue, 0, unroll=DMA_UNROLL)
    pltpu.make_async_copy(xb_ref.at[pl.ds(0, 2 * tm)], xb_ref.at[pl.ds(0, 2 * tm)], sem).wait()


def _dispatch(d0, d1, last_blk, n_used, h2, n_slot_rows):
    n_rows, D = h2.shape
    tm = ROW_TILE
    return pl.pallas_call(
        _dispatch_body,
        grid_spec=pltpu.PrefetchScalarGridSpec(
            num_scalar_prefetch=4, grid=(n_rows // tm,),
            in_specs=[pl.BlockSpec((tm, D), lambda i, *_: (i, 0))],
            out_specs=pl.BlockSpec(memory_space=pl.ANY),
            scratch_shapes=[pltpu.VMEM((MOE_ROWS, D), F32), pltpu.SemaphoreType.DMA(()),
                            pltpu.SemaphoreType.DMA(())]),
        out_shape=jax.ShapeDtypeStruct((n_slot_rows, D), F32),
        compiler_params=pltpu.CompilerParams(dimension_semantics=("arbitrary",), vmem_limit_bytes=VMEM_LIMIT,
                                             disable_bounds_checks=True),
        name="moe_dispatch",
    )(d0, d1, last_blk, n_used, h2)


def _ffn_body(be_ref, nb_ref, x_ref, w1_ref, w3_ref, w2_ref, y_ref, w1b, w3b, w2b):
    i = pl.program_id(0)
    used = i < nb_ref[0]
    new_expert = (i == 0) | (be_ref[i] != be_ref[jnp.maximum(i - 1, 0)])

    @pl.when(used & new_expert)
    def _():
        w1b[...] = w1_ref[0, 0].astype(BF16)
        w3b[...] = w3_ref[0, 0].astype(BF16)
        w2b[...] = w2_ref[0, 0].astype(BF16)

    @pl.when(used)
    def _():
        xb = x_ref[...].astype(BF16)
        a = jnp.dot(xb, w1b[...], preferred_element_type=F32)
        b = jnp.dot(xb, w3b[...], preferred_element_type=F32)
        hmid = ((a * jax.nn.sigmoid(a)) * b).astype(BF16)
        y_ref[...] = jnp.dot(hmid, w2b[...], preferred_element_type=F32)

    @pl.when(jnp.logical_not(used))
    def _():
        y_ref[...] = jnp.zeros_like(y_ref)


def _expert_ffn(blk_e, n_used, xb, w1, w3, w2, layer):
    P, D = xb.shape
    F = w1.shape[3]
    TB = MOE_ROWS
    rows = pl.BlockSpec((TB, D), lambda i, be, nb: (i, 0))
    return pl.pallas_call(
        _ffn_body,
        grid_spec=pltpu.PrefetchScalarGridSpec(
            num_scalar_prefetch=2, grid=(P // TB,),
            in_specs=[rows, pl.BlockSpec((1, 1, D, F), lambda i, be, nb: (layer, be[i], 0, 0)),
                      pl.BlockSpec((1, 1, D, F), lambda i, be, nb: (layer, be[i], 0, 0)),
                      pl.BlockSpec((1, 1, F, D), lambda i, be, nb: (layer, be[i], 0, 0))],
            out_specs=rows,
            scratch_shapes=[pltpu.VMEM((D, F), BF16), pltpu.VMEM((D, F), BF16), pltpu.VMEM((F, D), BF16)]),
        out_shape=jax.ShapeDtypeStruct((P, D), F32),
        compiler_params=_cparams("arbitrary"), name="moe_ffn",
    )(blk_e, n_used, xb, w1, w3, w2)


def _combine_body(d0_ref, d1_ref, x_ref, mod_ref, rf_ref, yb_ref, fw_ref, o_ref, buf, sem, *, final):
    i = pl.program_id(0)
    tm = x_ref.shape[0]

    def issue(r, carry):
        for j, d_ref in enumerate((d0_ref, d1_ref)):
            pltpu.make_async_copy(yb_ref.at[pl.ds(d_ref[i, r], 1)], buf.at[j, pl.ds(r, 1)], sem).start()
        return carry

    lax.fori_loop(0, tm, issue, 0, unroll=DMA_UNROLL)
    pltpu.make_async_copy(buf, buf, sem).wait()
    g = rf_ref[...]
    f = g[:, 0:1] * buf[0] + g[:, 1:2] * buf[1]
    xn = x_ref[...] + mod_ref[0, 5:6, :] * f
    if final:
        ms = jnp.mean(xn * xn, axis=-1, keepdims=True)
        xn = xn * lax.rsqrt(ms + NORM_EPS) * fw_ref[...]
    o_ref[...] = xn


def _combine(d0, d1, xa, mod, rf, yb, final_w, final, n_rows, n_lat_tiles, tiles_per_batch, n_batch):
    D = xa.shape[1]
    tm = ROW_TILE
    row = lambda wd: pl.BlockSpec((tm, wd), lambda i, *_: (i, 0))
    return pl.pallas_call(
        functools.partial(_combine_body, final=final),
        grid_spec=pltpu.PrefetchScalarGridSpec(
            num_scalar_prefetch=2, grid=(n_rows // tm,),
            in_specs=[row(D), pl.BlockSpec((1,) + mod.shape[1:],
                                           lambda i, *_: (_mod_index(i, n_lat_tiles, tiles_per_batch, n_batch), 0, 0)),
                      row(LANES), pl.BlockSpec(memory_space=pl.ANY), pl.BlockSpec((1, D), lambda i, *_: (0, 0))],
            out_specs=row(D),
            scratch_shapes=[pltpu.VMEM((2, tm, D), F32), pltpu.SemaphoreType.DMA(())]),
        out_shape=jax.ShapeDtypeStruct((n_rows, D), F32),
        compiler_params=pltpu.CompilerParams(dimension_semantics=("arbitrary",), vmem_limit_bytes=VMEM_LIMIT,
                                             disable_bounds_checks=True),
        name="moe_combine",
    )(d0, d1, xa, mod, rf, yb, final_w[None])


def _moe(xa, mod, p, expert_w, layer, final_w, final, n_rows, n_lat_tiles, tiles_per_batch, n_batch):
    TB = MOE_ROWS
    h2, ri, rf, cnt = _router(xa, mod, p['norm2_w'], p['moe_wg'], p['moe_bg'], p['moe_we'], p['moe_be'],
                              n_rows, n_lat_tiles, tiles_per_batch, n_batch)
    counts = cnt[0, ROUTE_LANE0:ROUTE_LANE0 + MOE_EXPERTS].astype(jnp.int32)
    pc = (counts + TB - 1) // TB * TB
    pend = jnp.cumsum(pc)
    base = pend - pc
    n_blk = -(-2 * n_rows // TB) + MOE_EXPERTS
    n_used = (pend[-1] // TB).astype(jnp.int32)
    blk = jnp.arange(n_blk, dtype=jnp.int32)
    blk_e = jnp.sum((pend[None, :] <= (jnp.minimum(blk, n_used - 1) * TB)[:, None]).astype(jnp.int32), axis=1)
    blk_e = jnp.minimum(blk_e, MOE_EXPERTS - 1).astype(jnp.int32)
    last_blk = jnp.where(pc > 0, pend // TB - 1, -1).astype(jnp.int32)
    d0, d1 = _plan(ri, base)
    xb = _dispatch(d0, d1, last_blk, n_used[None], h2, n_blk * TB)
    yb = _expert_ffn(blk_e, n_used[None], xb, *expert_w, layer)
    return _combine(d0, d1, xa, mod, rf, yb, final_w, final, n_rows, n_lat_tiles, tiles_per_batch, n_batch)


_LAYER_KEYS = ('ada_w', 'ada_b', 'norm1_w', 'norm2_w', 'w_in', 'b_in', 'w_out', 'hy_conv_w', 'hy_conv_b',
               'hy_filt_w1', 'hy_filt_b1', 'hy_filt_w2', 'hy_filt_b2', 'hy_filt_w3', 'hy_filt_b3', 'hy_sin_freq',
               'hy_bias_d', 'ml_conv_w', 'ml_conv_b', 'ml_norm_w', 'da_lambda', 'da_subln_w', 'moe_wg', 'moe_bg',
               'moe_we', 'moe_be', 'moe_w1', 'moe_w3', 'moe_w2')


def kernel(x, c, ctx, c_ctx, ada_w, ada_b, norm1_w, norm2_w, w_in, b_in, w_out, hy_conv_w, hy_conv_b, hy_filt_w1,
           hy_filt_b1, hy_filt_w2, hy_filt_b2, hy_filt_w3, hy_filt_b3, hy_sin_freq, hy_bias_d, ml_conv_w, ml_conv_b,
           ml_norm_w, da_lambda, da_subln_w, moe_wg, moe_bg, moe_we, moe_be, moe_w1, moe_w3, moe_w2, final_norm_w):
    stacked = dict(zip(_LAYER_KEYS, (ada_w, ada_b, norm1_w, norm2_w, w_in, b_in, w_out, hy_conv_w, hy_conv_b,
                                     hy_filt_w1, hy_filt_b1, hy_filt_w2, hy_filt_b2, hy_filt_w3, hy_filt_b3,
                                     hy_sin_freq, hy_bias_d, ml_conv_w, ml_conv_b, ml_norm_w, da_lambda, da_subln_w,
                                     moe_wg, moe_bg, moe_we, moe_be, moe_w1, moe_w3, moe_w2)))
    B, L, D = x.shape
    Lc = ctx.shape[1]
    depth = ada_w.shape[0]
    tm = ROW_TILE
    assert L % tm == 0 and Lc % tm == 0 and L % GRID_W == 0 and L % ML_CHUNK == 0 and Lc % ML_CHUNK == 0
    ML, MC = B * L, B * Lc
    M = ML + MC
    n_lat, n_ctx = ML // tm, MC // tm
    tpb = L // tm
    regions = ((0, n_lat, tpb), (n_lat, n_ctx, Lc // tm))

    xa = jnp.concatenate([x.reshape(ML, D), ctx.reshape(MC, D)], axis=0)
    R = -(-(B + 1) // SUBLANES) * SUBLANES
    cond = jnp.concatenate([c, c_ctx[None], jnp.zeros((R - B - 1, D), F32)], axis=0)
    cos, sin = _rope_tables(L)

    for l in range(depth):
        last = l == depth - 1
        p = {k: v[l] for k, v in stacked.items() if k not in ('moe_w1', 'moe_w3', 'moe_w2')}
        lam_init = 0.8 - 0.6 * math.exp(-0.3 * l)
        mod = _ada(cond, p['ada_w'], p['ada_b']).reshape(R, 6, D)
        u_hy, u_ml, u_da, g_t = _proj_in(xa, mod, p['norm1_w'], p['w_in'], p['b_in'], n_lat, tpb, B)

        y_hy = _hyena(*_hy_prep(u_hy, p['hy_conv_w'], p['hy_conv_b'], 0, n_lat, tpb), B, L, p)
        q_m, k_t = _ml_prep(u_ml, p['ml_conv_w'], p['ml_conv_b'], regions)
        h_ml = _mlstm(q_m, k_t, u_ml, g_t, B, L, Lc)
        qa, ka, va = _da_prep(u_da, cos, sin, n_lat, tpb)
        y_da = _diff_attn(qa, ka, va, p['da_lambda'], p['da_subln_w'], lam_init, B, 0, L, ((0, L), (ML, Lc)))
        n_rows = ML if last else M
        if not last:
            yc_hy = _hyena(*_hy_prep(u_hy, p['hy_conv_w'], p['hy_conv_b'], n_lat, n_ctx, Lc // tm), B, Lc, p)
            yc_da = _diff_attn(qa, ka, va, p['da_lambda'], p['da_subln_w'], lam_init, B, ML, Lc, ((ML, Lc),))
            y_hy = jnp.concatenate([y_hy, yc_hy], axis=1)
            y_da = jnp.concatenate([y_da, yc_da], axis=0)
        xa = _proj_out(xa, mod, y_hy, h_ml, u_ml, y_da, p['ml_norm_w'], p['w_out'], n_rows, n_lat, tpb, B)
        xa = _moe(xa, mod, p, (moe_w1, moe_w3, moe_w2), l, final_norm_w, last, n_rows, n_lat, tpb, B)
    return xa.reshape(B, L, D)
```

```python
import functools
import math

import numpy as np
import jax
import jax.numpy as jnp
from jax import lax
from jax.experimental import pallas as pl
from jax.experimental.pallas import tpu as pltpu

F32 = jnp.float32
BF16 = jnp.bfloat16

NORM_EPS = 1e-6
GRID_W = 64

HY_CH = 256
HY_HALVES = 2
HY_POS_EMB = 33
HY_FAST_DECAY = 0.3
HY_SLOW_DECAY = 1.5
HY_DECAY_TARGET = 1e-2

ML_HEADS = 4
ML_HD = 64
ML_W = ML_HEADS * ML_HD

DA_HEADS = 4
DA_HD = 64
DA_QK = DA_HEADS * 2 * DA_HD
DA_V = DA_HEADS * 2 * DA_HD
ROPE_THETA = 10000.0

HY_OFF = 0
ML_OFF = HY_OFF + 3 * HY_CH
GATE_OFF = ML_OFF + 4 * ML_W
DA_OFF = GATE_OFF + 2 * 2 * ML_HEADS

MOE_GROUPS = 4
MOE_PER_GROUP = 8
MOE_EXPERTS = MOE_GROUPS * MOE_PER_GROUP

LANES = 128
SUBLANES = 8
VMEM_BYTES_V7X = 64 * 1024 * 1024
VMEM_LIMIT = VMEM_BYTES_V7X * 7 // 8

ROW_TILE = 256
ML_CHUNK = 128
ATT_Q_TILE = 1024
ATT_SUB_TILE = 256
MOE_ROWS = 256


def _cparams(*sem):
    return pltpu.CompilerParams(dimension_semantics=tuple(sem), vmem_limit_bytes=VMEM_LIMIT)


def _dot(a, b):
    return jnp.dot(a.astype(BF16), b.astype(BF16), preferred_element_type=F32)


def _split(a):
    hi = a.astype(BF16)
    lo = (a - hi.astype(F32)).astype(BF16)
    return hi, lo


def _dot3(a, b):
    ah, al = _split(a)
    bh, bl = _split(b)
    d = functools.partial(jnp.dot, preferred_element_type=F32)
    return d(ah, bh) + (d(ah, bl) + d(al, bh))


def _dot2_exact_rhs(a, b_exact):
    ah, al = _split(a)
    d = functools.partial(jnp.dot, preferred_element_type=F32)
    return d(ah, b_exact) + d(al, b_exact)


def _ada_body(a_ref, w_ref, b_ref, o_ref):
    a = a_ref[...]
    a = a * jax.nn.sigmoid(a)
    o_ref[...] = _dot3(a, w_ref[...]) + b_ref[...]


def _ada(cond, w, b):
    R, D = cond.shape
    N = w.shape[1]
    tn = 1536
    return pl.pallas_call(
        _ada_body,
        grid=(N // tn,),
        in_specs=[pl.BlockSpec((R, D), lambda j: (0, 0)),
                  pl.BlockSpec((D, tn), lambda j: (0, j)),
                  pl.BlockSpec((1, tn), lambda j: (0, j))],
        out_specs=pl.BlockSpec((R, tn), lambda j: (0, j)),
        out_shape=jax.ShapeDtypeStruct((R, N), F32),
        compiler_params=_cparams("parallel"), name="ada_mod",
    )(cond, w, b[None])


def _mod_index(i, n_lat_tiles, tiles_per_batch, n_batch):
    return jnp.where(i < n_lat_tiles, i // tiles_per_batch, n_batch)


def _norm_mod(x, nw, shift, scale):
    ms = jnp.mean(x * x, axis=-1, keepdims=True)
    return (x * lax.rsqrt(ms + NORM_EPS) * nw) * (1.0 + scale) + shift


def _proj_in_body(x_ref, mod_ref, nw_ref, why_ref, wml_ref, wda_ref, wgt_ref, bhy_ref, bml_ref, bda_ref, bgt_ref,
                  ohy_ref, oml_ref, oda_ref, ogt_ref):
    h = _norm_mod(x_ref[...], nw_ref[...], mod_ref[0, 0:1, :], mod_ref[0, 1:2, :])
    hb = h.astype(BF16)
    d = functools.partial(jnp.dot, preferred_element_type=F32)
    ohy_ref[...] = d(hb, why_ref[...]) + bhy_ref[...]
    oml_ref[...] = d(hb, wml_ref[...]) + bml_ref[...]
    oda_ref[...] = d(hb, wda_ref[...]) + bda_ref[...]
    ogt_ref[...] = lax.dot_general(wgt_ref[...], hb, (((1,), (1,)), ((), ())),
                                   preferred_element_type=F32) + bgt_ref[...]


def _proj_in(xa, mod, nw, w_in, b_in, n_lat_tiles, tiles_per_batch, n_batch):
    M, D = xa.shape
    tm = ROW_TILE
    wb = w_in.astype(BF16)
    why, wml, wg, wda = wb[:, :ML_OFF], wb[:, ML_OFF:GATE_OFF], wb[:, GATE_OFF:DA_OFF], wb[:, DA_OFF:]
    bhy, bml, bg, bda = b_in[:ML_OFF], b_in[ML_OFF:GATE_OFF], b_in[GATE_OFF:DA_OFF], b_in[DA_OFF:]
    ng = wg.shape[1]
    full = lambda a: pl.BlockSpec(a.shape, lambda i: (0,) * a.ndim)
    args = (xa, mod, nw[None], why, wml, wda, wg.T, bhy[None], bml[None], bda[None], bg[:, None])
    in_specs = [pl.BlockSpec((tm, D), lambda i: (i, 0)),
                pl.BlockSpec((1,) + mod.shape[1:],
                             lambda i: (_mod_index(i, n_lat_tiles, tiles_per_batch, n_batch), 0, 0))]
    in_specs += [full(a) for a in args[2:]]
    widths = (why.shape[1], wml.shape[1], wda.shape[1])
    out_specs = [pl.BlockSpec((tm, wd), lambda i: (i, 0)) for wd in widths]
    out_specs.append(pl.BlockSpec((ng, tm), lambda i: (0, i)))
    out_shape = [jax.ShapeDtypeStruct((M, wd), F32) for wd in widths]
    out_shape.append(jax.ShapeDtypeStruct((ng, M), F32))
    return pl.pallas_call(
        _proj_in_body, grid=(M // tm,), in_specs=in_specs, out_specs=out_specs, out_shape=out_shape,
        compiler_params=_cparams("parallel"), name="proj_in",
    )(*args)


def _conv3(x, prev_row, next_row, w, b, at_start, at_end):
    T = x.shape[0]
    rows = lax.broadcasted_iota(jnp.int32, x.shape, 0)
    prev_row = jnp.where(at_start, 0.0, prev_row)
    next_row = jnp.where(at_end, 0.0, next_row)
    up = jnp.where(rows == 0, prev_row, pltpu.roll(x, 1, 0))
    dn = jnp.where(rows == T - 1, next_row, pltpu.roll(x, T - 1, 0))
    return up * w[0:1, :] + x * w[1:2, :] + dn * w[2:3, :] + b


def _seq_edges(i, regions):
    at_start = jnp.bool_(False)
    at_end = jnp.bool_(False)
    pos = 0
    for (_, n, tps) in regions:
        inside = (i >= pos) & (i < pos + n)
        r = (i - pos) % tps
        at_start = at_start | (inside & (r == 0))
        at_end = at_end | (inside & (r == tps - 1))
        pos += n
    return at_start, at_end


def _seq_tile(i, regions):
    pos = 0
    t = jnp.int32(0)
    for (first, n, _) in regions:
        t = jnp.where((i >= pos) & (i < pos + n), first + (i - pos), t)
        pos += n
    return t


def _halo_specs(tm, width, colblk, regions, n_rows):
    per = tm // SUBLANES
    last8 = n_rows // SUBLANES - 1
    cur = pl.BlockSpec((tm, width), lambda i: (_seq_tile(i, regions), colblk))
    prv = pl.BlockSpec((SUBLANES, width), lambda i: (jnp.maximum(_seq_tile(i, regions) * per - 1, 0), colblk))
    nxt = pl.BlockSpec((SUBLANES, width),
                       lambda i: (jnp.minimum((_seq_tile(i, regions) + 1) * per, last8), colblk))
    return [cur, prv, nxt]


def _hy_prep_body(x_ref, p_ref, n_ref, w_ref, b_ref, v_ref, x1_ref, x2_ref, *, regions):
    at_start, at_end = _seq_edges(pl.program_id(0), regions)
    y = _conv3(x_ref[...], p_ref[SUBLANES - 1:SUBLANES, :], n_ref[0:1, :], w_ref[...], b_ref[...], at_start, at_end)
    for k, ref in enumerate((v_ref, x1_ref, x2_ref)):
        for h in range(HY_HALVES):
            ref[h] = y[:, k * HY_CH + h * LANES:k * HY_CH + (h + 1) * LANES]


def _hy_prep(u_hy, w, b, first_tile, n_tiles, tiles_per_seq):
    M, W = u_hy.shape
    tm = ROW_TILE
    regions = ((first_tile, n_tiles, tiles_per_seq),)
    out = jax.ShapeDtypeStruct((HY_HALVES, n_tiles * tm, LANES), F32)
    return pl.pallas_call(
        functools.partial(_hy_prep_body, regions=regions),
        grid=(n_tiles,),
        in_specs=_halo_specs(tm, W, 0, regions, M) + [pl.BlockSpec((3, W), lambda i: (0, 0)),
                                                      pl.BlockSpec((1, W), lambda i: (0, 0))],
        out_specs=[pl.BlockSpec((HY_HALVES, tm, LANES), lambda i: (0, i, 0))] * 3,
        out_shape=[out] * 3,
        compiler_params=_cparams("parallel"), name="hy_prep",
    )(u_hy, u_hy, u_hy, w, b[None])


def _filt_body(z_ref, w1_ref, b1_ref, w2_ref, b2_ref, fr_ref, w3_ref, b3_ref, env_ref, o_ref):
    f = fr_ref[...]
    h = jnp.sin(f * (_dot3(z_ref[...], w1_ref[...]) + b1_ref[...]))
    h = jnp.sin(f * (_dot3(h, w2_ref[...]) + b2_ref[...]))
    k = _dot3(h, w3_ref[...]) + b3_ref[...]
    e = env_ref[...]
    kf = k[:, :HY_CH] * e
    kb = k[:, HY_CH:] * e
    s = (jnp.sum(jnp.abs(kf), axis=0, keepdims=True) + jnp.sum(jnp.abs(kb), axis=0, keepdims=True)
         - jnp.abs(kb[0:1, :]))
    o_ref[:, :HY_CH] = kf / s
    o_ref[:, HY_CH:] = kb / s


def _hy_filters(L, p):
    t = jnp.linspace(0.0, 1.0, L, dtype=F32)[:, None]
    bands = (HY_POS_EMB - 1) // 2
    w = (2.0 * math.pi / L) * jnp.arange(L, dtype=F32)[:, None]
    f = jnp.linspace(1e-4, bands - 1, bands, dtype=F32)[None, :]
    z = jnp.concatenate([t, jnp.cos(f * w), -jnp.sin(f * w)], axis=-1)
    zp = jnp.pad(z, ((0, 0), (0, LANES - HY_POS_EMB)))
    w1 = jnp.pad(p['hy_filt_w1'], ((0, LANES - HY_POS_EMB), (0, 0)))
    deltas = jnp.abs(jnp.linspace(math.log(HY_DECAY_TARGET) / HY_SLOW_DECAY,
                                  math.log(HY_DECAY_TARGET) / HY_FAST_DECAY, HY_CH, dtype=F32))
    env = jnp.exp(-t * deltas)
    hid = w1.shape[1]
    nout = p['hy_filt_w3'].shape[1]
    n_order = nout // (2 * HY_CH)
    c0 = lambda a: pl.BlockSpec(a.shape, lambda o: (0,) * a.ndim)
    args = (zp, w1, p['hy_filt_b1'][None], p['hy_filt_w2'], p['hy_filt_b2'][None], p['hy_sin_freq'][None],
            p['hy_filt_w3'], p['hy_filt_b3'][None], env)
    in_specs = [c0(a) for a in args[:6]]
    in_specs += [pl.BlockSpec((hid, 2 * HY_CH), lambda o: (0, o)), pl.BlockSpec((1, 2 * HY_CH), lambda o: (0, o)),
                 c0(env)]
    return pl.pallas_call(
        _filt_body, grid=(n_order,), in_specs=in_specs,
        out_specs=pl.BlockSpec((L, 2 * HY_CH), lambda o: (0, o)),
        out_shape=jax.ShapeDtypeStruct((L, nout), F32),
        compiler_params=_cparams("parallel"), name="hy_filter",
    )(*args)


def _fft_plan(L):
    N = 2 * L
    Bn = 128 if N >= 4096 else 16
    A = N // Bn
    assert A * Bn == N and A % 16 == 0
    return A, Bn


def _dft_small(A, N):
    ka = np.arange(A)[:, None]
    a = np.arange(A)[None, :]
    th = 2.0 * np.pi * ((ka * a) % A) / A
    d1 = np.concatenate([np.cos(th), -np.sin(th)], axis=0)
    d4 = np.concatenate([np.cos(th.T), -np.sin(th.T)], axis=1) / N
    return jnp.asarray(d1, F32), jnp.asarray(d4, F32)


def _dft_mid(A, Bn):
    N = A * Bn
    ka = jnp.arange(A, dtype=jnp.int32)[:, None, None]
    kb = jnp.arange(Bn, dtype=jnp.int32)[None, :, None]
    b = jnp.arange(Bn, dtype=jnp.int32)[None, None, :]
    m = (b * (kb * A + ka)) % N
    ph = m.astype(F32) * (2.0 * math.pi / N)
    c, s = jnp.cos(ph), jnp.sin(ph)
    mf = jnp.concatenate([jnp.concatenate([c, s], axis=2), jnp.concatenate([-s, c], axis=2)], axis=1)
    ct, st = jnp.swapaxes(c, 1, 2), jnp.swapaxes(s, 1, 2)
    mi = jnp.concatenate([jnp.concatenate([ct, -st], axis=2), jnp.concatenate([st, ct], axis=2)], axis=1)
    return mf.astype(BF16), mi.astype(BF16)


FFT_BT = 16
FFT_GROUP = 8


def _fft1_body(d_ref, x_ref, o_ref, *, Bn):
    _, A2, bt, C = o_ref.shape
    Ain = d_ref.shape[1] // bt
    b0 = pl.multiple_of(pl.program_id(1) * bt, bt)
    slabs = [jnp.concatenate([x_ref[h, pl.ds(a * Bn + b0, bt), :] for h in range(HY_HALVES)], axis=1)
             for a in range(Ain)]
    xblk = jnp.concatenate(slabs, axis=0).astype(BF16)
    o_ref[0] = jnp.dot(d_ref[...], xblk, preferred_element_type=F32).reshape(A2, bt, C)


def _fft1(d1, x, n_seq, Bn):
    A2, Ain = d1.shape
    rows = Ain * Bn
    bt = min(FFT_BT, Bn)
    dbig = jnp.kron(d1, jnp.eye(bt, dtype=F32)).astype(BF16)
    return pl.pallas_call(
        functools.partial(_fft1_body, Bn=Bn), grid=(n_seq, Bn // bt),
        in_specs=[pl.BlockSpec(dbig.shape, lambda s, j: (0, 0)),
                  pl.BlockSpec((HY_HALVES, rows, LANES), lambda s, j: (0, s, 0))],
        out_specs=pl.BlockSpec((1, A2, bt, HY_CH), lambda s, j: (s, 0, j, 0)),
        out_shape=jax.ShapeDtypeStruct((n_seq, A2, Bn, HY_CH), F32),
        compiler_params=_cparams("parallel", "arbitrary"), name="hy_dft_slow",
    )(dbig, x)


def _fast_operand(p_ref, j):
    return jnp.concatenate([p_ref[0, 0, j], p_ref[0, 1, j]], axis=0).astype(BF16)


def _fft_spec_body(p_ref, mf_ref, k_ref, *, Bn):
    for j in range(FFT_GROUP):
        X = jnp.dot(mf_ref[j], _fast_operand(p_ref, j), preferred_element_type=F32)
        k_ref[0, j, 0] = X[:Bn]
        k_ref[0, j, 1] = X[Bn:]


def _fft_spec(p5, mf):
    S, _, A, Bn, C = p5.shape
    G = FFT_GROUP
    return pl.pallas_call(
        functools.partial(_fft_spec_body, Bn=Bn), grid=(A // G, S),
        in_specs=[pl.BlockSpec((1, 2, G, Bn, C), lambda g, s: (s, 0, g, 0, 0)),
                  pl.BlockSpec((G, 2 * Bn, 2 * Bn), lambda g, s: (g, 0, 0))],
        out_specs=pl.BlockSpec((1, G, 2, Bn, C), lambda g, s: (s, g, 0, 0, 0)),
        out_shape=jax.ShapeDtypeStruct((S, A, 2, Bn, C), F32),
        compiler_params=_cparams("parallel", "parallel"), name="hy_filter_spec",
    )(p5, mf)


def _fft_mid_body(p_ref, mf_ref, mi_ref, k_ref, q_ref, *, Bn):
    for j in range(FFT_GROUP):
        X = jnp.dot(mf_ref[j], _fast_operand(p_ref, j), preferred_element_type=F32)
        xr, xi = X[:Bn], X[Bn:]
        kr, ki = k_ref[j, 0], k_ref[j, 1]
        Y = jnp.concatenate([xr * kr - xi * ki, xr * ki + xi * kr], axis=0).astype(BF16)
        Q = jnp.dot(mi_ref[j], Y, preferred_element_type=F32)
        q_ref[0, j, 0] = Q[:Bn]
        q_ref[0, j, 1] = Q[Bn:]


def _fft_mid(p5, mf, mi, kspec):
    S, _, A, Bn, C = p5.shape
    G = FFT_GROUP
    mat = pl.BlockSpec((G, 2 * Bn, 2 * Bn), lambda g, s: (g, 0, 0))
    return pl.pallas_call(
        functools.partial(_fft_mid_body, Bn=Bn), grid=(A // G, S),
        in_specs=[pl.BlockSpec((1, 2, G, Bn, C), lambda g, s: (s, 0, g, 0, 0)), mat, mat,
                  pl.BlockSpec((G, 2, Bn, C), lambda g, s: (g, 0, 0, 0))],
        out_specs=pl.BlockSpec((1, G, 2, Bn, C), lambda g, s: (s, g, 0, 0, 0)),
        out_shape=jax.ShapeDtypeStruct((S, A, 2, Bn, C), F32),
        compiler_params=_cparams("parallel", "parallel"), name="hy_dft_fast",
    )(p5, mf, mi, kspec)


def _fft4_body(d_ref, q_ref, z_ref, g_ref, dch_ref, o_ref, *, Bn):
    _, A2, bt, C = q_ref.shape
    Ah = d_ref.shape[0] // bt
    b0 = pl.multiple_of(pl.program_id(1) * bt, bt)
    qblk = q_ref[0].reshape(A2 * bt, C).astype(BF16)
    y = jnp.dot(d_ref[...], qblk, preferred_element_type=F32)
    dch = dch_ref[...]
    for a in range(Ah):
        rows = pl.ds(a * Bn + b0, bt)
        ya = y[a * bt:(a + 1) * bt]
        for h in range(HY_HALVES):
            lanes = slice(h * LANES, (h + 1) * LANES)
            o_ref[h, rows, :] = g_ref[h, rows, :] * (ya[:, lanes] + z_ref[h, rows, :] * dch[:, lanes])


def _fft4(d4, q4, z, gate, dch):
    S, A2, Bn, C = q4.shape
    Ah = d4.shape[0]
    L = Ah * Bn
    bt = min(FFT_BT, Bn)
    dbig = jnp.kron(d4, jnp.eye(bt, dtype=F32)).astype(BF16)
    sig = pl.BlockSpec((HY_HALVES, L, LANES), lambda s, j: (0, s, 0))
    return pl.pallas_call(
        functools.partial(_fft4_body, Bn=Bn), grid=(S, Bn // bt),
        in_specs=[pl.BlockSpec(dbig.shape, lambda s, j: (0, 0)),
                  pl.BlockSpec((1, A2, bt, C), lambda s, j: (s, 0, j, 0)), sig, sig,
                  pl.BlockSpec((1, C), lambda s, j: (0, 0))],
        out_specs=sig,
        out_shape=jax.ShapeDtypeStruct((HY_HALVES, S * L, LANES), F32),
        compiler_params=_cparams("parallel", "arbitrary"), name="hy_idft_gate",
    )(dbig, q4, z, gate, dch)


def _hyena(v, x1, x2, n_seq, L, p):
    C = HY_CH
    A, Bn = _fft_plan(L)
    Ah = A // 2
    d1, d4 = _dft_small(A, A * Bn)
    mf, mi = _dft_mid(A, Bn)
    d4 = d4.reshape(A, 2, A).transpose(0, 2, 1).reshape(A, 2 * A)
    kn = _hy_filters(L, p).reshape(L, -1, 2, C)
    n_order = kn.shape[1]
    zero = jnp.zeros((1, C), F32)
    k2 = jnp.concatenate([piece for o in range(n_order)
                          for piece in (kn[:, o, 0], zero, jnp.flip(kn[1:, o, 1], axis=0))], axis=0)
    k2 = k2.reshape(-1, HY_HALVES, LANES).transpose(1, 0, 2)
    kspec = _fft_spec(_fft1(d1, k2, n_order, Bn).reshape(n_order, 2, A, Bn, C), mf)
    z = v
    for o, gate in enumerate((x1, x2)):
        P = _fft1(d1[:, :Ah], z, n_seq, Bn).reshape(n_seq, 2, A, Bn, C)
        Q = _fft_mid(P, mf, mi, kspec[o]).reshape(n_seq, 2 * A, Bn, C)
        z = _fft4(d4[:Ah], Q, z, gate, p['hy_bias_d'][o][None])
    return z


def _ml_prep_body(x_ref, p_ref, n_ref, w_ref, b_ref, q_ref, kt_ref, *, regions):
    at_start, at_end = _seq_edges(pl.program_id(0), regions)
    y = _conv3(x_ref[...], p_ref[SUBLANES - 1:SUBLANES, :], n_ref[0:1, :], w_ref[...], b_ref[...], at_start, at_end)
    y = y * jax.nn.sigmoid(y)
    q_ref[...] = y[:, :ML_W].astype(BF16)
    kt_ref[...] = (y[:, ML_W:] * (ML_HD ** -0.5)).T


def _ml_prep(u_ml, w, b, regions):
    M = u_ml.shape[0]
    tm = ROW_TILE
    n = sum(r[1] for r in regions)
    return pl.pallas_call(
        functools.partial(_ml_prep_body, regions=regions), grid=(n,),
        in_specs=_halo_specs(tm, 2 * ML_W, 0, regions, M) + [pl.BlockSpec((3, 2 * ML_W), lambda i: (0, 0)),
                                                             pl.BlockSpec((1, 2 * ML_W), lambda i: (0, 0))],
        out_specs=[pl.BlockSpec((tm, ML_W), lambda i: (_seq_tile(i, regions), 0)),
                   pl.BlockSpec((ML_W, tm), lambda i: (0, _seq_tile(i, regions)))],
        out_shape=[jax.ShapeDtypeStruct((M, ML_W), BF16), jax.ShapeDtypeStruct((ML_W, M), F32)],
        compiler_params=_cparams("parallel"), name="ml_prep",
    )(u_ml, u_ml, u_ml, w, b[None])


def _ml_body(*refs):
    ins, outs, scr = refs[:8], refs[8:10], refs[10:]

    @pl.when(pl.program_id(1) == 0)
    def _():
        for ref in scr:
            ref[...] = jnp.zeros_like(ref)

    for d in range(2):
        _ml_chunk(d, *ins[4 * d:4 * d + 4], outs[d], *scr[2 * d:2 * d + 2])


def _ml_chunk(d, q_ref, kt_ref, v_ref, g_ref, o_ref, st_ref, m_ref):
    T = q_ref.shape[0]
    H, W = ML_HEADS, ML_W
    WA = W + LANES
    sgn = 1 - 2 * d
    g = g_ref[...]
    gs = g[2 * H * d:2 * H * (d + 1)]
    ig = gs[0:H]
    lf8 = -(jnp.maximum(-gs, 0.0) + jnp.log1p(jnp.exp(-jnp.abs(gs))))
    lf = lf8[H:2 * H]
    r_i = lax.broadcasted_iota(jnp.int32, (T, T), 0)
    c_i = lax.broadcasted_iota(jnp.int32, (T, T), 1)
    prec = ((c_i - r_i) * sgn) <= 0
    incl = jnp.where(((r_i - c_i) * sgn) <= 0, 1.0, 0.0).astype(BF16)
    after = jnp.where(((c_i - r_i) * sgn) < 0, 1.0, 0.0).astype(BF16)
    b_rows = _dot2_exact_rhs(lf8, incl)[H:2 * H]
    bL = jnp.sum(lf, axis=1, keepdims=True)
    a_row = bL - b_rows + ig
    m_loc = jnp.max(a_row, axis=1, keepdims=True)
    w_row = jnp.exp(a_row - m_loc)
    m0 = m_ref[0:H, 0:1]
    m_new = jnp.maximum(bL + m0, m_loc)
    s_old = jnp.exp(bL + m0 - m_new)
    s_loc = jnp.exp(m_loc - m_new)

    qb = q_ref[...]
    kt = kt_ref[...]
    row_head = lax.broadcasted_iota(jnp.int32, (W, T), 0) // ML_HD
    lane = lax.broadcasted_iota(jnp.int32, (1, WA), 1)
    lane_head = jnp.where(lane < W, lane // ML_HD, lane - W)
    v_aug = jnp.concatenate([v_ref[...], jnp.ones((T, LANES), F32)], axis=1)

    Lfs = [jnp.where(prec, lf[h:h + 1], 0.0) for h in range(H)]
    E_all = _dot2_exact_rhs(jnp.concatenate(Lfs, axis=0), after)
    kth_all = jnp.concatenate([jnp.where(row_head == h, kt, 0.0) for h in range(H)], axis=1).astype(BF16)
    S_all = jnp.dot(qb, kth_all, preferred_element_type=F32)

    ps, vbd = [], []
    w_inter = jnp.zeros((T, WA), F32)
    e_m = jnp.zeros((T, WA), F32)
    for h in range(H):
        b_col = jnp.sum(Lfs[h], axis=1, keepdims=True)
        Dm = jnp.where(prec, E_all[h * T:(h + 1) * T] + ig[h:h + 1], -jnp.inf)
        inter = b_col + m0[h:h + 1]
        m_col = jnp.maximum(inter, jnp.max(Dm, axis=1, keepdims=True))
        S = S_all[:, h * T:(h + 1) * T]
        ps.append((jnp.exp(Dm - m_col) * S).astype(BF16))
        sel = lane_head == h
        vbd.append(jnp.where(sel, v_aug, 0.0).astype(BF16))
        w_inter = w_inter + jnp.where(sel, jnp.exp(inter - m_col), 0.0)
        e_m = e_m + jnp.where(sel, jnp.exp(-m_col), 0.0)
    nd = jnp.dot(jnp.concatenate(ps, axis=1), jnp.concatenate(vbd, axis=0), preferred_element_type=F32)
    nd = nd + w_inter * jnp.dot(qb, st_ref[...].astype(BF16), preferred_element_type=F32)
    den = jnp.zeros((T, W), F32)
    for h in range(H):
        den = den + jnp.where(lane_head[:, :W] == h, nd[:, W + h:W + h + 1], 0.0)
    o_ref[...] = nd[:, :W] / jnp.maximum(jnp.abs(den), e_m[:, :W])

    wk = jnp.zeros((W, T), F32)
    scol = jnp.zeros((1, WA), F32)
    ws = w_row * s_loc
    for h in range(H):
        wk = wk + jnp.where(row_head == h, ws[h:h + 1], 0.0)
        scol = scol + jnp.where(lane_head == h, s_old[h:h + 1], 0.0)
    st_loc = jnp.dot((kt * wk).astype(BF16), v_aug.astype(BF16), preferred_element_type=F32)
    diag = (lax.broadcasted_iota(jnp.int32, (W, WA), 0) // ML_HD) == lane_head
    st_ref[...] = jnp.where(diag, st_ref[...] * scol + st_loc, 0.0)
    m_ref[0:H, :] = jnp.broadcast_to(m_new, (H, LANES))


def _mlstm(q, kt, u_ml, g_t, B, L, Lc):
    M = q.shape[0]
    T = ML_CHUNK
    nC, nL = Lc // T, L // T
    lat0 = 0
    ctx0 = (B * L) // T

    def blk(d, b, i):
        cc = i if d == 0 else nC - 1 - i
        j = i - nC
        cl = j if d == 0 else nL - 1 - j
        return jnp.where(i < nC, ctx0 + b * nC + cc, lat0 + b * nL + cl)

    in_specs, out_specs = [], []
    for d in range(2):
        rows = functools.partial(lambda b, i, d, c: (blk(d, b, i), c), d=d)
        cols = functools.partial(lambda b, i, d: (0, blk(d, b, i)), d=d)
        in_specs += [pl.BlockSpec((T, ML_W), functools.partial(rows, c=0)), pl.BlockSpec((ML_W, T), cols),
                     pl.BlockSpec((T, ML_W), functools.partial(rows, c=2)), pl.BlockSpec((4 * ML_HEADS, T), cols)]
        out_specs.append(pl.BlockSpec((T, ML_W), functools.partial(rows, c=0)))
    state = [pltpu.VMEM((ML_W, ML_W + LANES), F32), pltpu.VMEM((SUBLANES, LANES), F32)]
    return pl.pallas_call(
        _ml_body, grid=(B, nC + nL), in_specs=in_specs, out_specs=out_specs,
        out_shape=[jax.ShapeDtypeStruct((M, ML_W), F32)] * 2,
        scratch_shapes=state * 2,
        compiler_params=_cparams("parallel", "arbitrary"), name="mlstm",
    )(*([q, kt, u_ml, g_t] * 2))


def _rope_tables(L):
    rows = L // GRID_W
    row = jnp.repeat(jnp.arange(rows, dtype=F32), GRID_W)
    col = jnp.tile(jnp.arange(GRID_W, dtype=F32), rows)
    half = DA_HD // 2
    inv = ROPE_THETA ** (-jnp.arange(0, half, 2, dtype=F32) / half)
    ang = jnp.stack([row, col], axis=-1)[:, :, None] * inv
    ang = jnp.stack([ang, ang], axis=-2).reshape(-1, DA_HD)
    ang = jnp.concatenate([ang, ang], axis=1)
    return jnp.cos(ang), jnp.sin(ang)


def _da_prep_body(u_ref, cos_ref, sin_ref, q_ref, k_ref, va_ref, *, n_lat_tiles):
    is_lat = pl.program_id(0) < n_lat_tiles
    u = u_ref[...]
    reps = DA_QK // cos_ref.shape[1]
    cs = jnp.concatenate([cos_ref[...]] * reps, axis=1)
    sn = jnp.concatenate([sin_ref[...]] * reps, axis=1)
    q4 = DA_HD // 4
    src = lax.broadcasted_iota(jnp.int32, (LANES, LANES), 0)
    dst = lax.broadcasted_iota(jnp.int32, (LANES, LANES), 1)
    first = (dst % (2 * q4)) < q4
    perm = jnp.where(first & (src == dst + q4), -1.0, jnp.where((~first) & (src == dst - q4), 1.0, 0.0)).astype(BF16)

    def rope(x):
        rot = jnp.concatenate([_dot2_exact_rhs(x[:, g * LANES:(g + 1) * LANES], perm)
                               for g in range(x.shape[1] // LANES)], axis=1)
        return jnp.where(is_lat, x * cs + rot * sn, x)

    q_ref[...] = (rope(u[:, :DA_QK]) * (DA_HD ** -0.5 * math.log2(math.e))).astype(BF16)
    k_ref[...] = rope(u[:, DA_QK:2 * DA_QK]).astype(BF16)
    va_ref[...] = u[:, 2 * DA_QK:].T.astype(BF16)


def _da_prep(u_da, cos, sin, n_lat_tiles, tiles_per_seq):
    M = u_da.shape[0]
    tm = ROW_TILE
    tab = pl.BlockSpec((tm, cos.shape[1]), lambda i: (jnp.where(i < n_lat_tiles, i % tiles_per_seq, 0), 0))
    out = pl.BlockSpec((tm, DA_QK), lambda i: (i, 0))
    return pl.pallas_call(
        functools.partial(_da_prep_body, n_lat_tiles=n_lat_tiles), grid=(M // tm,),
        in_specs=[pl.BlockSpec((tm, u_da.shape[1]), lambda i: (i, 0)), tab, tab],
        out_specs=[out, out, pl.BlockSpec((DA_V, tm), lambda i: (0, i))],
        out_shape=[jax.ShapeDtypeStruct((M, DA_QK), BF16)] * 2 + [jax.ShapeDtypeStruct((DA_V, M), BF16)],
        compiler_params=_cparams("parallel"), name="da_prep",
    )(u_da, cos, sin)


def _da_body(lam_ref, w_ref, q_ref, *rest, nseg, lam_init, sub):
    ks, vas, o_ref = rest[:nseg], rest[nseg:2 * nseg], rest[2 * nseg]
    lp = lam_ref[...]
    lam = (jnp.exp(jnp.sum(lp[0:1] * lp[1:2], axis=1, keepdims=True))
           - jnp.exp(jnp.sum(lp[2:3] * lp[3:4], axis=1, keepdims=True)) + lam_init)
    HW = 2 * DA_HD
    lane = lax.broadcasted_iota(jnp.int32, (1, HW), 1)
    n_sub = q_ref.shape[0] // sub
    def scores(i):
        t, m = divmod(i, 2)
        q = q_ref[t * sub:(t + 1) * sub, :]
        qm = jnp.where((lane // DA_HD) == m, q, jnp.zeros_like(q))
        return [lax.dot_general(k[...], qm, (((1,), (1,)), ((), ())), preferred_element_type=F32) for k in ks]

    def softmax(ss):
        mx = functools.reduce(jnp.maximum, [jnp.max(s, axis=0, keepdims=True) for s in ss])
        ps = [jnp.exp2(s - mx) for s in ss]
        den = functools.reduce(jnp.add, [jnp.sum(pr, axis=0, keepdims=True) for pr in ps])
        return [pr.astype(BF16) for pr in ps], den

    def values(ps, den):
        acc = functools.reduce(jnp.add, [jnp.dot(va[...], pr, preferred_element_type=F32)
                                         for pr, va in zip(ps, vas)])
        return acc / den

    n_streams = 2 * n_sub
    sss = {0: scores(0)}
    if n_streams > 1:
        sss[1] = scores(1)
    outs = []
    for i in range(n_streams):
        pd = softmax(sss.pop(i))
        if i + 2 < n_streams:
            sss[i + 2] = scores(i + 2)
        outs.append(values(*pd))
    for t in range(n_sub):
        o = outs[2 * t] - lam * outs[2 * t + 1]
        ms = jnp.mean(o * o, axis=0, keepdims=True)
        o_ref[t * sub:(t + 1) * sub, :] = ((o * lax.rsqrt(ms + NORM_EPS) * w_ref[...]) * (1.0 - lam_init)).T


def _diff_attn(qa, ka, vaa, lam_p, subln_w, lam_init, n_batch, q_rows0, q_len, segs):
    tq = min(ATT_Q_TILE, q_len)
    HW = 2 * DA_HD
    nq = q_len // tq
    q0 = q_rows0 // tq
    k_specs = [pl.BlockSpec((n, HW), functools.partial(lambda b, h, i, f, n: (f // n + b, h), f=f, n=n))
               for (f, n) in segs]
    va_specs = [pl.BlockSpec((HW, n), functools.partial(lambda b, h, i, f, n: (h, f // n + b), f=f, n=n))
                for (f, n) in segs]
    return pl.pallas_call(
        functools.partial(_da_body, nseg=len(segs), lam_init=lam_init, sub=min(ATT_SUB_TILE, tq)),
        grid=(n_batch, DA_HEADS, nq),
        in_specs=[pl.BlockSpec(lam_p.shape, lambda b, h, i: (0, 0)), pl.BlockSpec((HW, 1), lambda b, h, i: (0, 0)),
                  pl.BlockSpec((tq, HW), lambda b, h, i: (q0 + b * nq + i, h))] + k_specs + va_specs,
        out_specs=pl.BlockSpec((tq, HW), lambda b, h, i: (b * nq + i, h)),
        out_shape=jax.ShapeDtypeStruct((n_batch * q_len, DA_V), F32),
        compiler_params=_cparams("parallel", "parallel", "parallel"), name="diff_attn",
    )(lam_p, subln_w[:, None], qa, *([ka] * len(segs)), *([vaa] * len(segs)))


def _proj_out_body(x_ref, mod_ref, hy_ref, hf_ref, hb_ref, og_ref, da_ref, mw_ref, why_ref, wml_ref, wda_ref, o_ref):
    hs = hf_ref[...] + hb_ref[...]
    W = hs.shape[1]
    r = lax.broadcasted_iota(jnp.int32, (W, W), 0) // ML_HD
    c = lax.broadcasted_iota(jnp.int32, (W, W), 1) // ML_HD
    same_head = jnp.where(r == c, 1.0, 0.0).astype(BF16)
    ms = _dot2_exact_rhs(hs * hs, same_head) * (1.0 / ML_HD)
    y_ml = jax.nn.sigmoid(og_ref[...]) * (hs * lax.rsqrt(ms + NORM_EPS) * mw_ref[...])
    y_hy = jnp.concatenate([hy_ref[h] for h in range(HY_HALVES)], axis=1)
    y = (_dot(y_hy, why_ref[...]) + _dot(y_ml, wml_ref[...])) + _dot(da_ref[...], wda_ref[...])
    o_ref[...] = x_ref[...] + mod_ref[0, 2:3, :] * y


def _proj_out(xa, mod, y_hy, h_ml, u_ml, y_da, ml_norm_w, w_out, n_rows, n_lat_tiles, tiles_per_batch, n_batch):
    D = xa.shape[1]
    tm = ROW_TILE
    wb = w_out.astype(BF16)
    why, wml, wda = wb[:HY_CH], wb[HY_CH:HY_CH + ML_W], wb[HY_CH + ML_W:]
    full = lambda a: pl.BlockSpec(a.shape, lambda i: (0,) * a.ndim)
    row = lambda wd, cb=0: pl.BlockSpec((tm, wd), lambda i: (i, cb))
    mw = ml_norm_w[None]
    return pl.pallas_call(
        _proj_out_body, grid=(n_rows // tm,),
        in_specs=[row(D), pl.BlockSpec((1,) + mod.shape[1:],
                                       lambda i: (_mod_index(i, n_lat_tiles, tiles_per_batch, n_batch), 0, 0)),
                  pl.BlockSpec((HY_HALVES, tm, LANES), lambda i: (0, i, 0)),
                  row(ML_W), row(ML_W), row(ML_W, 3), row(DA_V),
                  full(mw), full(why), full(wml), full(wda)],
        out_specs=row(D),
        out_shape=jax.ShapeDtypeStruct((n_rows, D), F32),
        compiler_params=_cparams("parallel"), name="proj_out",
    )(xa, mod, y_hy, *h_ml, u_ml, y_da, mw, why, wml, wda)


ROUTE_LANE0 = MOE_GROUPS


def _router_body(x_ref, mod_ref, nw_ref, wr_ref, br_ref, h_ref, ri_ref, rf_ref, cnt_ref, run_ref):
    @pl.when(pl.program_id(0) == 0)
    def _():
        run_ref[...] = jnp.zeros_like(run_ref)

    h = _norm_mod(x_ref[...], nw_ref[...], mod_ref[0, 3:4, :], mod_ref[0, 4:5, :])
    h_ref[...] = h
    lg = _dot3(h, wr_ref[...]) + br_ref[...]
    tm = lg.shape[0]
    lane = lax.broadcasted_iota(jnp.int32, lg.shape, 1)
    neg = -jnp.inf
    is_g = lane < MOE_GROUPS
    gl = jnp.where(is_g, lg, neg)
    gmax = jnp.max(gl, axis=1, keepdims=True)
    gidx = jnp.min(jnp.where(gl == gmax, lane, LANES), axis=1, keepdims=True)
    gw = 1.0 / jnp.sum(jnp.where(is_g, jnp.exp(gl - gmax), 0.0), axis=1, keepdims=True)
    e_of = lane - ROUTE_LANE0
    in_grp = (e_of >= 0) & (e_of < MOE_EXPERTS) & ((e_of // MOE_PER_GROUP) == gidx)
    el = jnp.where(in_grp, lg, neg)
    t1 = jnp.max(el, axis=1, keepdims=True)
    i1 = jnp.min(jnp.where(el == t1, lane, LANES), axis=1, keepdims=True)
    el2 = jnp.where(lane == i1, neg, el)
    t2 = jnp.max(el2, axis=1, keepdims=True)
    i2 = jnp.min(jnp.where(el2 == t2, lane, LANES), axis=1, keepdims=True)
    ex = jnp.exp(t2 - t1)
    g1 = gw / (1.0 + ex)
    g2 = gw * ex / (1.0 + ex)
    oh = jnp.where((lane == i1) | (lane == i2), 1.0, 0.0)
    r_i = lax.broadcasted_iota(jnp.int32, (tm, tm), 0)
    c_i = lax.broadcasted_iota(jnp.int32, (tm, tm), 1)
    earlier = jnp.where(c_i < r_i, 1.0, 0.0).astype(BF16)
    cum = jnp.dot(earlier, oh.astype(BF16), preferred_element_type=F32) + run_ref[0:1, :]
    r1 = jnp.sum(jnp.where(lane == i1, cum, 0.0), axis=1, keepdims=True).astype(jnp.int32)
    r2 = jnp.sum(jnp.where(lane == i2, cum, 0.0), axis=1, keepdims=True).astype(jnp.int32)
    run = run_ref[0:1, :] + jnp.sum(oh, axis=0, keepdims=True)
    run_ref[...] = jnp.broadcast_to(run, run_ref.shape)
    cnt_ref[...] = jnp.broadcast_to(run, cnt_ref.shape)
    zi = jnp.zeros_like(lane)
    ri_ref[...] = jnp.where(lane == 0, i1 - ROUTE_LANE0, jnp.where(lane == 1, i2 - ROUTE_LANE0,
                            jnp.where(lane == 2, r1, jnp.where(lane == 3, r2, zi))))
    rf_ref[...] = jnp.where(lane == 0, g1, jnp.where(lane == 1, g2, 0.0))


def _router(xa, mod, nw, wg, bg, we, be, n_rows, n_lat_tiles, tiles_per_batch, n_batch):
    D = xa.shape[1]
    tm = ROW_TILE
    pad = LANES - MOE_GROUPS - MOE_EXPERTS
    wr = jnp.concatenate([wg, we, jnp.zeros((D, pad), F32)], axis=1)
    br = jnp.concatenate([bg, be, jnp.zeros((pad,), F32)])[None]
    row = lambda wd: pl.BlockSpec((tm, wd), lambda i: (i, 0))
    full = lambda a: pl.BlockSpec(a.shape, lambda i: (0,) * a.ndim)
    return pl.pallas_call(
        _router_body, grid=(n_rows // tm,),
        in_specs=[row(D), pl.BlockSpec((1,) + mod.shape[1:],
                                       lambda i: (_mod_index(i, n_lat_tiles, tiles_per_batch, n_batch), 0, 0)),
                  full(nw[None]), full(wr), full(br)],
        out_specs=[row(D), row(LANES), row(LANES), pl.BlockSpec((SUBLANES, LANES), lambda i: (0, 0))],
        out_shape=[jax.ShapeDtypeStruct((n_rows, D), F32), jax.ShapeDtypeStruct((n_rows, LANES), jnp.int32),
                   jax.ShapeDtypeStruct((n_rows, LANES), F32), jax.ShapeDtypeStruct((SUBLANES, LANES), F32)],
        scratch_shapes=[pltpu.VMEM((SUBLANES, LANES), F32)],
        compiler_params=_cparams("arbitrary"), name="moe_router",
    )(xa, mod, nw[None], wr, br)


def _plan_body(ri_ref, base_ref, d_ref):
    ri = ri_ref[...]
    lane = lax.broadcasted_iota(jnp.int32, ri.shape, 1)
    base = base_ref[...]
    d = []
    for j in range(2):
        b = jnp.sum(jnp.where(lane == ri[:, j:j + 1], base, 0.0), axis=1, keepdims=True)
        d.append(b + ri[:, 2 + j:3 + j].astype(F32))
    slots = jnp.where(lane == 0, d[0], jnp.where(lane == 1, d[1], 0.0))
    d_ref[0] = slots.T[0:SUBLANES, :].astype(jnp.int32)


def _plan(ri, base):
    n_rows = ri.shape[0]
    tm = ROW_TILE
    basef = jnp.pad(base.astype(F32), (0, LANES - base.shape[0]))[None]
    out = pl.pallas_call(
        _plan_body, grid=(n_rows // tm,),
        in_specs=[pl.BlockSpec((tm, LANES), lambda i: (i, 0)), pl.BlockSpec((1, LANES), lambda i: (0, 0))],
        out_specs=pl.BlockSpec((1, SUBLANES, tm), lambda i: (i, 0, 0)),
        out_shape=jax.ShapeDtypeStruct((n_rows // tm, SUBLANES, tm), jnp.int32),
        compiler_params=_cparams("parallel"), name="moe_plan",
    )(ri, basef)
    return out[:, 0, :], out[:, 1, :]


DMA_UNROLL = 8


def _dispatch_body(d0_ref, d1_ref, lb_ref, nb_ref, h_ref, xb_ref, zbuf, sem, zsem):
    i = pl.program_id(0)
    tm = h_ref.shape[0]
    TB = zbuf.shape[0]
    n_blk = xb_ref.shape[0] // TB

    def zero_copy(blk):
        return pltpu.make_async_copy(zbuf, xb_ref.at[pl.ds(pl.multiple_of(blk * TB, TB), TB)], zsem)

    @pl.when(i == 0)
    def _():
        zbuf[...] = jnp.zeros_like(zbuf)
        for phase in ("start", "wait"):
            def tail(blk, carry, phase=phase):
                getattr(zero_copy(blk), phase)()
                return carry

            for e in range(lb_ref.shape[0]):
                @pl.when(lb_ref[e] >= 0)
                def _(e=e, phase=phase):
                    getattr(zero_copy(lb_ref[e]), phase)()
            lax.fori_loop(nb_ref[0], n_blk, tail, 0)

    def issue(r, carry):
        for j, d_ref in enumerate((d0_ref, d1_ref)):
            pltpu.make_async_copy(h_ref.at[pl.ds(r, 1)], xb_ref.at[pl.ds(d_ref[i, r], 1)], sem).start()
        return carry

    lax.fori_loop(0, tm, issue, 0, unroll=DMA_UNROLL)
    pltpu.make_async_copy(xb_ref.at[pl.ds(0, 2 * tm)], xb_ref.at[pl.ds(0, 2 * tm)], sem).wait()


def _dispatch(d0, d1, last_blk, n_used, h2, n_slot_rows):
    n_rows, D = h2.shape
    tm = ROW_TILE
    return pl.pallas_call(
        _dispatch_body,
        grid_spec=pltpu.PrefetchScalarGridSpec(
            num_scalar_prefetch=4, grid=(n_rows // tm,),
            in_specs=[pl.BlockSpec((tm, D), lambda i, *_: (i, 0))],
            out_specs=pl.BlockSpec(memory_space=pl.ANY),
            scratch_shapes=[pltpu.VMEM((MOE_ROWS, D), F32), pltpu.SemaphoreType.DMA(()),
                            pltpu.SemaphoreType.DMA(())]),
        out_shape=jax.ShapeDtypeStruct((n_slot_rows, D), F32),
        compiler_params=pltpu.CompilerParams(dimension_semantics=("arbitrary",), vmem_limit_bytes=VMEM_LIMIT,
                                             disable_bounds_checks=True),
        name="moe_dispatch",
    )(d0, d1, last_blk, n_used, h2)


def _ffn_body(be_ref, nb_ref, x_ref, w1_ref, w3_ref, w2_ref, y_ref, w1b, w3b, w2b):
    i = pl.program_id(0)
    used = i < nb_ref[0]
    new_expert = (i == 0) | (be_ref[i] != be_ref[jnp.maximum(i - 1, 0)])

    @pl.when(used & new_expert)
    def _():
        w1b[...] = w1_ref[0, 0].astype(BF16)
        w3b[...] = w3_ref[0, 0].astype(BF16)
        w2b[...] = w2_ref[0, 0].astype(BF16)

    @pl.when(used)
    def _():
        xb = x_ref[...].astype(BF16)
        a = jnp.dot(xb, w1b[...], preferred_element_type=F32)
        b = jnp.dot(xb, w3b[...], preferred_element_type=F32)
        hmid = ((a * jax.nn.sigmoid(a)) * b).astype(BF16)
        y_ref[...] = jnp.dot(hmid, w2b[...], preferred_element_type=F32)

    @pl.when(jnp.logical_not(used))
    def _():
        y_ref[...] = jnp.zeros_like(y_ref)


def _expert_ffn(blk_e, n_used, xb, w1, w3, w2, layer):
    P, D = xb.shape
    F = w1.shape[3]
    TB = MOE_ROWS
    rows = pl.BlockSpec((TB, D), lambda i, be, nb: (i, 0))
    return pl.pallas_call(
        _ffn_body,
        grid_spec=pltpu.PrefetchScalarGridSpec(
            num_scalar_prefetch=2, grid=(P // TB,),
            in_specs=[rows, pl.BlockSpec((1, 1, D, F), lambda i, be, nb: (layer, be[i], 0, 0)),
                      pl.BlockSpec((1, 1, D, F), lambda i, be, nb: (layer, be[i], 0, 0)),
                      pl.BlockSpec((1, 1, F, D), lambda i, be, nb: (layer, be[i], 0, 0))],
            out_specs=rows,
            scratch_shapes=[pltpu.VMEM((D, F), BF16), pltpu.VMEM((D, F), BF16), pltpu.VMEM((F, D), BF16)]),
        out_shape=jax.ShapeDtypeStruct((P, D), F32),
        compiler_params=_cparams("arbitrary"), name="moe_ffn",
    )(blk_e, n_used, xb, w1, w3, w2)


def _combine_body(d0_ref, d1_ref, x_ref, mod_ref, rf_ref, yb_ref, fw_ref, o_ref, buf, sem, *, final):
    i = pl.program_id(0)
    n = pl.num_programs(0)
    tm = x_ref.shape[0]

    def gather(step, slot):
        def issue(r, carry):
            for j, d_ref in enumerate((d0_ref, d1_ref)):
                pltpu.make_async_copy(yb_ref.at[pl.ds(d_ref[step, r], 1)], buf.at[slot, j, pl.ds(r, 1)],
                                      sem.at[slot]).start()
            return carry
        lax.fori_loop(0, tm, issue, 0, unroll=DMA_UNROLL)

    @pl.when(i == 0)
    def _():
        gather(0, 0)

    slot = i % 2
    pltpu.make_async_copy(buf.at[slot], buf.at[slot], sem.at[slot]).wait()

    @pl.when(i + 1 < n)
    def _():
        gather(i + 1, 1 - slot)

    g = rf_ref[...]
    f = g[:, 0:1] * buf[slot, 0] + g[:, 1:2] * buf[slot, 1]
    xn = x_ref[...] + mod_ref[0, 5:6, :] * f
    if final:
        ms = jnp.mean(xn * xn, axis=-1, keepdims=True)
        xn = xn * lax.rsqrt(ms + NORM_EPS) * fw_ref[...]
    o_ref[...] = xn


def _combine(d0, d1, xa, mod, rf, yb, final_w, final, n_rows, n_lat_tiles, tiles_per_batch, n_batch):
    D = xa.shape[1]
    tm = ROW_TILE
    row = lambda wd: pl.BlockSpec((tm, wd), lambda i, *_: (i, 0))
    return pl.pallas_call(
        functools.partial(_combine_body, final=final),
        grid_spec=pltpu.PrefetchScalarGridSpec(
            num_scalar_prefetch=2, grid=(n_rows // tm,),
            in_specs=[row(D), pl.BlockSpec((1,) + mod.shape[1:],
                                           lambda i, *_: (_mod_index(i, n_lat_tiles, tiles_per_batch, n_batch), 0, 0)),
                      row(LANES), pl.BlockSpec(memory_space=pl.ANY), pl.BlockSpec((1, D), lambda i, *_: (0, 0))],
            out_specs=row(D),
            scratch_shapes=[pltpu.VMEM((2, 2, tm, D), F32), pltpu.SemaphoreType.DMA((2,))]),
        out_shape=jax.ShapeDtypeStruct((n_rows, D), F32),
        compiler_params=pltpu.CompilerParams(dimension_semantics=("arbitrary",), vmem_limit_bytes=VMEM_LIMIT,
                                             disable_bounds_checks=True),
        name="moe_combine",
    )(d0, d1, xa, mod, rf, yb, final_w[None])


def _moe(xa, mod, p, expert_w, layer, final_w, final, n_rows, n_lat_tiles, tiles_per_batch, n_batch):
    TB = MOE_ROWS
    h2, ri, rf, cnt = _router(xa, mod, p['norm2_w'], p['moe_wg'], p['moe_bg'], p['moe_we'], p['moe_be'],
                              n_rows, n_lat_tiles, tiles_per_batch, n_batch)
    counts = cnt[0, ROUTE_LANE0:ROUTE_LANE0 + MOE_EXPERTS].astype(jnp.int32)
    pc = (counts + TB - 1) // TB * TB
    pend = jnp.cumsum(pc)
    base = pend - pc
    n_blk = -(-2 * n_rows // TB) + MOE_EXPERTS
    n_used = (pend[-1] // TB).astype(jnp.int32)
    blk = jnp.arange(n_blk, dtype=jnp.int32)
    blk_e = jnp.sum((pend[None, :] <= (jnp.minimum(blk, n_used - 1) * TB)[:, None]).astype(jnp.int32), axis=1)
    blk_e = jnp.minimum(blk_e, MOE_EXPERTS - 1).astype(jnp.int32)
    last_blk = jnp.where(pc > 0, pend // TB - 1, -1).astype(jnp.int32)
    d0, d1 = _plan(ri, base)
    xb = _dispatch(d0, d1, last_blk, n_used[None], h2, n_blk * TB)
    yb = _expert_ffn(blk_e, n_used[None], xb, *expert_w, layer)
    return _combine(d0, d1, xa, mod, rf, yb, final_w, final, n_rows, n_lat_tiles, tiles_per_batch, n_batch)


_LAYER_KEYS = ('ada_w', 'ada_b', 'norm1_w', 'norm2_w', 'w_in', 'b_in', 'w_out', 'hy_conv_w', 'hy_conv_b',
               'hy_filt_w1', 'hy_filt_b1', 'hy_filt_w2', 'hy_filt_b2', 'hy_filt_w3', 'hy_filt_b3', 'hy_sin_freq',
               'hy_bias_d', 'ml_conv_w', 'ml_conv_b', 'ml_norm_w', 'da_lambda', 'da_subln_w', 'moe_wg', 'moe_bg',
               'moe_we', 'moe_be', 'moe_w1', 'moe_w3', 'moe_w2')


def kernel(x, c, ctx, c_ctx, ada_w, ada_b, norm1_w, norm2_w, w_in, b_in, w_out, hy_conv_w, hy_conv_b, hy_filt_w1,
           hy_filt_b1, hy_filt_w2, hy_filt_b2, hy_filt_w3, hy_filt_b3, hy_sin_freq, hy_bias_d, ml_conv_w, ml_conv_b,
           ml_norm_w, da_lambda, da_subln_w, moe_wg, moe_bg, moe_we, moe_be, moe_w1, moe_w3, moe_w2, final_norm_w):
    stacked = dict(zip(_LAYER_KEYS, (ada_w, ada_b, norm1_w, norm2_w, w_in, b_in, w_out, hy_conv_w, hy_conv_b,
                                     hy_filt_w1, hy_filt_b1, hy_filt_w2, hy_filt_b2, hy_filt_w3, hy_filt_b3,
                                     hy_sin_freq, hy_bias_d, ml_conv_w, ml_conv_b, ml_norm_w, da_lambda, da_subln_w,
                                     moe_wg, moe_bg, moe_we, moe_be, moe_w1, moe_w3, moe_w2)))
    B, L, D = x.shape
    Lc = ctx.shape[1]
    depth = ada_w.shape[0]
    tm = ROW_TILE
    assert L % tm == 0 and Lc % tm == 0 and L % GRID_W == 0 and L % ML_CHUNK == 0 and Lc % ML_CHUNK == 0
    ML, MC = B * L, B * Lc
    M = ML + MC
    n_lat, n_ctx = ML // tm, MC // tm
    tpb = L // tm
    regions = ((0, n_lat, tpb), (n_lat, n_ctx, Lc // tm))

    xa = jnp.concatenate([x.reshape(ML, D), ctx.reshape(MC, D)], axis=0)
    R = -(-(B + 1) // SUBLANES) * SUBLANES
    cond = jnp.concatenate([c, c_ctx[None], jnp.zeros((R - B - 1, D), F32)], axis=0)
    cos, sin = _rope_tables(L)

    for l in range(depth):
        last = l == depth - 1
        p = {k: v[l] for k, v in stacked.items() if k not in ('moe_w1', 'moe_w3', 'moe_w2')}
        lam_init = 0.8 - 0.6 * math.exp(-0.3 * l)
        mod = _ada(cond, p['ada_w'], p['ada_b']).reshape(R, 6, D)
        u_hy, u_ml, u_da, g_t = _proj_in(xa, mod, p['norm1_w'], p['w_in'], p['b_in'], n_lat, tpb, B)

        y_hy = _hyena(*_hy_prep(u_hy, p['hy_conv_w'], p['hy_conv_b'], 0, n_lat, tpb), B, L, p)
        q_m, k_t = _ml_prep(u_ml, p['ml_conv_w'], p['ml_conv_b'], regions)
        h_ml = _mlstm(q_m, k_t, u_ml, g_t, B, L, Lc)
        qa, ka, va = _da_prep(u_da, cos, sin, n_lat, tpb)
        y_da = _diff_attn(qa, ka, va, p['da_lambda'], p['da_subln_w'], lam_init, B, 0, L, ((0, L), (ML, Lc)))
        n_rows = ML if last else M
        if not last:
            yc_hy = _hyena(*_hy_prep(u_hy, p['hy_conv_w'], p['hy_conv_b'], n_lat, n_ctx, Lc // tm), B, Lc, p)
            yc_da = _diff_attn(qa, ka, va, p['da_lambda'], p['da_subln_w'], lam_init, B, ML, Lc, ((ML, Lc),))
            y_hy = jnp.concatenate([y_hy, yc_hy], axis=1)
            y_da = jnp.concatenate([y_da, yc_da], axis=0)
        xa = _proj_out(xa, mod, y_hy, h_ml, u_ml, y_da, p['ml_norm_w'], p['w_out'], n_rows, n_lat, tpb, B)
        xa = _moe(xa, mod, p, (moe_w1, moe_w3, moe_w2), l, final_norm_w, last, n_rows, n_lat, tpb, B)
    return xa.reshape(B, L, D)
```

```python
import functools
import math

import numpy as np
import jax
import jax.numpy as jnp
from jax import lax
from jax.experimental import pallas as pl
from jax.experimental.pallas import tpu as pltpu

F32 = jnp.float32
BF16 = jnp.bfloat16

NORM_EPS = 1e-6
GRID_W = 64

HY_CH = 256
HY_HALVES = 2
HY_POS_EMB = 33
HY_FAST_DECAY = 0.3
HY_SLOW_DECAY = 1.5
HY_DECAY_TARGET = 1e-2

ML_HEADS = 4
ML_HD = 64
ML_W = ML_HEADS * ML_HD

DA_HEADS = 4
DA_HD = 64
DA_QK = DA_HEADS * 2 * DA_HD
DA_V = DA_HEADS * 2 * DA_HD
ROPE_THETA = 10000.0

HY_OFF = 0
ML_OFF = HY_OFF + 3 * HY_CH
GATE_OFF = ML_OFF + 4 * ML_W
DA_OFF = GATE_OFF + 2 * 2 * ML_HEADS

MOE_GROUPS = 4
MOE_PER_GROUP = 8
MOE_EXPERTS = MOE_GROUPS * MOE_PER_GROUP

LANES = 128
SUBLANES = 8
VMEM_BYTES_V7X = 64 * 1024 * 1024
VMEM_LIMIT = VMEM_BYTES_V7X * 7 // 8

ROW_TILE = 256
ML_CHUNK = 128
ATT_Q_TILE = 1024
ATT_SUB_TILE = 256
MOE_ROWS = 256


def _cparams(*sem):
    return pltpu.CompilerParams(dimension_semantics=tuple(sem), vmem_limit_bytes=VMEM_LIMIT)


def _dot(a, b):
    return jnp.dot(a.astype(BF16), b.astype(BF16), preferred_element_type=F32)


def _split(a):
    hi = a.astype(BF16)
    lo = (a - hi.astype(F32)).astype(BF16)
    return hi, lo


def _dot3(a, b):
    ah, al = _split(a)
    bh, bl = _split(b)
    d = functools.partial(jnp.dot, preferred_element_type=F32)
    return d(ah, bh) + (d(ah, bl) + d(al, bh))


def _dot2_exact_rhs(a, b_exact):
    ah, al = _split(a)
    d = functools.partial(jnp.dot, preferred_element_type=F32)
    return d(ah, b_exact) + d(al, b_exact)


def _ada_body(a_ref, w_ref, b_ref, o_ref):
    a = a_ref[...]
    a = a * jax.nn.sigmoid(a)
    o_ref[...] = _dot3(a, w_ref[...]) + b_ref[...]


def _ada(cond, w, b):
    R, D = cond.shape
    N = w.shape[1]
    tn = 1536
    return pl.pallas_call(
        _ada_body,
        grid=(N // tn,),
        in_specs=[pl.BlockSpec((R, D), lambda j: (0, 0)),
                  pl.BlockSpec((D, tn), lambda j: (0, j)),
                  pl.BlockSpec((1, tn), lambda j: (0, j))],
        out_specs=pl.BlockSpec((R, tn), lambda j: (0, j)),
        out_shape=jax.ShapeDtypeStruct((R, N), F32),
        compiler_params=_cparams("parallel"), name="ada_mod",
    )(cond, w, b[None])


def _mod_index(i, n_lat_tiles, tiles_per_batch, n_batch):
    return jnp.where(i < n_lat_tiles, i // tiles_per_batch, n_batch)


def _norm_mod(x, nw, shift, scale):
    ms = jnp.mean(x * x, axis=-1, keepdims=True)
    return (x * lax.rsqrt(ms + NORM_EPS) * nw) * (1.0 + scale) + shift


def _proj_in_body(x_ref, mod_ref, nw_ref, why_ref, wml_ref, wda_ref, wgt_ref, bhy_ref, bml_ref, bda_ref, bgt_ref,
                  ohy_ref, oml_ref, oda_ref, ogt_ref):
    h = _norm_mod(x_ref[...], nw_ref[...], mod_ref[0, 0:1, :], mod_ref[0, 1:2, :])
    hb = h.astype(BF16)
    d = functools.partial(jnp.dot, preferred_element_type=F32)
    ohy_ref[...] = d(hb, why_ref[...]) + bhy_ref[...]
    oml_ref[...] = d(hb, wml_ref[...]) + bml_ref[...]
    oda_ref[...] = d(hb, wda_ref[...]) + bda_ref[...]
    ogt_ref[...] = lax.dot_general(wgt_ref[...], hb, (((1,), (1,)), ((), ())),
                                   preferred_element_type=F32) + bgt_ref[...]


def _proj_in(xa, mod, nw, w_in, b_in, n_lat_tiles, tiles_per_batch, n_batch):
    M, D = xa.shape
    tm = ROW_TILE
    wb = w_in.astype(BF16)
    why, wml, wg, wda = wb[:, :ML_OFF], wb[:, ML_OFF:GATE_OFF], wb[:, GATE_OFF:DA_OFF], wb[:, DA_OFF:]
    bhy, bml, bg, bda = b_in[:ML_OFF], b_in[ML_OFF:GATE_OFF], b_in[GATE_OFF:DA_OFF], b_in[DA_OFF:]
    ng = wg.shape[1]
    full = lambda a: pl.BlockSpec(a.shape, lambda i: (0,) * a.ndim)
    args = (xa, mod, nw[None], why, wml, wda, wg.T, bhy[None], bml[None], bda[None], bg[:, None])
    in_specs = [pl.BlockSpec((tm, D), lambda i: (i, 0)),
                pl.BlockSpec((1,) + mod.shape[1:],
                             lambda i: (_mod_index(i, n_lat_tiles, tiles_per_batch, n_batch), 0, 0))]
    in_specs += [full(a) for a in args[2:]]
    widths = (why.shape[1], wml.shape[1], wda.shape[1])
    out_specs = [pl.BlockSpec((tm, wd), lambda i: (i, 0)) for wd in widths]
    out_specs.append(pl.BlockSpec((ng, tm), lambda i: (0, i)))
    out_shape = [jax.ShapeDtypeStruct((M, wd), F32) for wd in widths]
    out_shape.append(jax.ShapeDtypeStruct((ng, M), F32))
    return pl.pallas_call(
        _proj_in_body, grid=(M // tm,), in_specs=in_specs, out_specs=out_specs, out_shape=out_shape,
        compiler_params=_cparams("parallel"), name="proj_in",
    )(*args)


def _conv3(x, prev_row, next_row, w, b, at_start, at_end):
    T = x.shape[0]
    rows = lax.broadcasted_iota(jnp.int32, x.shape, 0)
    prev_row = jnp.where(at_start, 0.0, prev_row)
    next_row = jnp.where(at_end, 0.0, next_row)
    up = jnp.where(rows == 0, prev_row, pltpu.roll(x, 1, 0))
    dn = jnp.where(rows == T - 1, next_row, pltpu.roll(x, T - 1, 0))
    return up * w[0:1, :] + x * w[1:2, :] + dn * w[2:3, :] + b


def _seq_edges(i, regions):
    at_start = jnp.bool_(False)
    at_end = jnp.bool_(False)
    pos = 0
    for (_, n, tps) in regions:
        inside = (i >= pos) & (i < pos + n)
        r = (i - pos) % tps
        at_start = at_start | (inside & (r == 0))
        at_end = at_end | (inside & (r == tps - 1))
        pos += n
    return at_start, at_end


def _seq_tile(i, regions):
    pos = 0
    t = jnp.int32(0)
    for (first, n, _) in regions:
        t = jnp.where((i >= pos) & (i < pos + n), first + (i - pos), t)
        pos += n
    return t


def _halo_specs(tm, width, colblk, regions, n_rows):
    per = tm // SUBLANES
    last8 = n_rows // SUBLANES - 1
    cur = pl.BlockSpec((tm, width), lambda i: (_seq_tile(i, regions), colblk))
    prv = pl.BlockSpec((SUBLANES, width), lambda i: (jnp.maximum(_seq_tile(i, regions) * per - 1, 0), colblk))
    nxt = pl.BlockSpec((SUBLANES, width),
                       lambda i: (jnp.minimum((_seq_tile(i, regions) + 1) * per, last8), colblk))
    return [cur, prv, nxt]


def _hy_prep_body(x_ref, p_ref, n_ref, w_ref, b_ref, v_ref, x1_ref, x2_ref, *, regions):
    at_start, at_end = _seq_edges(pl.program_id(0), regions)
    y = _conv3(x_ref[...], p_ref[SUBLANES - 1:SUBLANES, :], n_ref[0:1, :], w_ref[...], b_ref[...], at_start, at_end)
    for k, ref in enumerate((v_ref, x1_ref, x2_ref)):
        for h in range(HY_HALVES):
            ref[h] = y[:, k * HY_CH + h * LANES:k * HY_CH + (h + 1) * LANES]


def _hy_prep(u_hy, w, b, first_tile, n_tiles, tiles_per_seq):
    M, W = u_hy.shape
    tm = ROW_TILE
    regions = ((first_tile, n_tiles, tiles_per_seq),)
    out = jax.ShapeDtypeStruct((HY_HALVES, n_tiles * tm, LANES), F32)
    return pl.pallas_call(
        functools.partial(_hy_prep_body, regions=regions),
        grid=(n_tiles,),
        in_specs=_halo_specs(tm, W, 0, regions, M) + [pl.BlockSpec((3, W), lambda i: (0, 0)),
                                                      pl.BlockSpec((1, W), lambda i: (0, 0))],
        out_specs=[pl.BlockSpec((HY_HALVES, tm, LANES), lambda i: (0, i, 0))] * 3,
        out_shape=[out] * 3,
        compiler_params=_cparams("parallel"), name="hy_prep",
    )(u_hy, u_hy, u_hy, w, b[None])


def _filt_body(z_ref, w1_ref, b1_ref, w2_ref, b2_ref, fr_ref, w3_ref, b3_ref, env_ref, o_ref):
    f = fr_ref[...]
    h = jnp.sin(f * (_dot3(z_ref[...], w1_ref[...]) + b1_ref[...]))
    h = jnp.sin(f * (_dot3(h, w2_ref[...]) + b2_ref[...]))
    k = _dot3(h, w3_ref[...]) + b3_ref[...]
    e = env_ref[...]
    kf = k[:, :HY_CH] * e
    kb = k[:, HY_CH:] * e
    s = (jnp.sum(jnp.abs(kf), axis=0, keepdims=True) + jnp.sum(jnp.abs(kb), axis=0, keepdims=True)
         - jnp.abs(kb[0:1, :]))
    o_ref[:, :HY_CH] = kf / s
    o_ref[:, HY_CH:] = kb / s


def _hy_filters(L, p):
    t = jnp.linspace(0.0, 1.0, L, dtype=F32)[:, None]
    bands = (HY_POS_EMB - 1) // 2
    w = (2.0 * math.pi / L) * jnp.arange(L, dtype=F32)[:, None]
    f = jnp.linspace(1e-4, bands - 1, bands, dtype=F32)[None, :]
    z = jnp.concatenate([t, jnp.cos(f * w), -jnp.sin(f * w)], axis=-1)
    zp = jnp.pad(z, ((0, 0), (0, LANES - HY_POS_EMB)))
    w1 = jnp.pad(p['hy_filt_w1'], ((0, LANES - HY_POS_EMB), (0, 0)))
    deltas = jnp.abs(jnp.linspace(math.log(HY_DECAY_TARGET) / HY_SLOW_DECAY,
                                  math.log(HY_DECAY_TARGET) / HY_FAST_DECAY, HY_CH, dtype=F32))
    env = jnp.exp(-t * deltas)
    hid = w1.shape[1]
    nout = p['hy_filt_w3'].shape[1]
    n_order = nout // (2 * HY_CH)
    c0 = lambda a: pl.BlockSpec(a.shape, lambda o: (0,) * a.ndim)
    args = (zp, w1, p['hy_filt_b1'][None], p['hy_filt_w2'], p['hy_filt_b2'][None], p['hy_sin_freq'][None],
            p['hy_filt_w3'], p['hy_filt_b3'][None], env)
    in_specs = [c0(a) for a in args[:6]]
    in_specs += [pl.BlockSpec((hid, 2 * HY_CH), lambda o: (0, o)), pl.BlockSpec((1, 2 * HY_CH), lambda o: (0, o)),
                 c0(env)]
    return pl.pallas_call(
        _filt_body, grid=(n_order,), in_specs=in_specs,
        out_specs=pl.BlockSpec((L, 2 * HY_CH), lambda o: (0, o)),
        out_shape=jax.ShapeDtypeStruct((L, nout), F32),
        compiler_params=_cparams("parallel"), name="hy_filter",
    )(*args)


def _fft_plan(L):
    N = 2 * L
    Bn = 128 if N >= 4096 else 16
    A = N // Bn
    assert A * Bn == N and A % 16 == 0
    return A, Bn


def _dft_small(A, N):
    ka = np.arange(A)[:, None]
    a = np.arange(A)[None, :]
    th = 2.0 * np.pi * ((ka * a) % A) / A
    d1 = np.concatenate([np.cos(th), -np.sin(th)], axis=0)
    d4 = np.concatenate([np.cos(th.T), -np.sin(th.T)], axis=1) / N
    return jnp.asarray(d1, F32), jnp.asarray(d4, F32)


def _dft_mid(A, Bn):
    N = A * Bn
    ka = jnp.arange(A, dtype=jnp.int32)[:, None, None]
    kb = jnp.arange(Bn, dtype=jnp.int32)[None, :, None]
    b = jnp.arange(Bn, dtype=jnp.int32)[None, None, :]
    m = (b * (kb * A + ka)) % N
    ph = m.astype(F32) * (2.0 * math.pi / N)
    c, s = jnp.cos(ph), jnp.sin(ph)
    mf = jnp.concatenate([jnp.concatenate([c, s], axis=2), jnp.concatenate([-s, c], axis=2)], axis=1)
    ct, st = jnp.swapaxes(c, 1, 2), jnp.swapaxes(s, 1, 2)
    mi = jnp.concatenate([jnp.concatenate([ct, -st], axis=2), jnp.concatenate([st, ct], axis=2)], axis=1)
    return mf.astype(BF16), mi.astype(BF16)


FFT_BT = 16
FFT_GROUP = 8


def _fft1_body(d_ref, x_ref, o_ref, *, Bn):
    _, A2, bt, C = o_ref.shape
    Ain = d_ref.shape[1] // bt
    b0 = pl.multiple_of(pl.program_id(1) * bt, bt)
    slabs = [jnp.concatenate([x_ref[h, pl.ds(a * Bn + b0, bt), :] for h in range(HY_HALVES)], axis=1)
             for a in range(Ain)]
    xblk = jnp.concatenate(slabs, axis=0).astype(BF16)
    o_ref[0] = jnp.dot(d_ref[...], xblk, preferred_element_type=F32).reshape(A2, bt, C)


def _fft1(d1, x, n_seq, Bn):
    A2, Ain = d1.shape
    rows = Ain * Bn
    bt = min(FFT_BT, Bn)
    dbig = jnp.kron(d1, jnp.eye(bt, dtype=F32)).astype(BF16)
    return pl.pallas_call(
        functools.partial(_fft1_body, Bn=Bn), grid=(n_seq, Bn // bt),
        in_specs=[pl.BlockSpec(dbig.shape, lambda s, j: (0, 0)),
                  pl.BlockSpec((HY_HALVES, rows, LANES), lambda s, j: (0, s, 0))],
        out_specs=pl.BlockSpec((1, A2, bt, HY_CH), lambda s, j: (s, 0, j, 0)),
        out_shape=jax.ShapeDtypeStruct((n_seq, A2, Bn, HY_CH), F32),
        compiler_params=_cparams("parallel", "arbitrary"), name="hy_dft_slow",
    )(dbig, x)


def _fast_operand(p_ref, j):
    return jnp.concatenate([p_ref[0, 0, j], p_ref[0, 1, j]], axis=0).astype(BF16)


def _fft_spec_body(p_ref, mf_ref, k_ref, *, Bn):
    for j in range(FFT_GROUP):
        X = jnp.dot(mf_ref[j], _fast_operand(p_ref, j), preferred_element_type=F32)
        k_ref[0, j, 0] = X[:Bn]
        k_ref[0, j, 1] = X[Bn:]


def _fft_spec(p5, mf):
    S, _, A, Bn, C = p5.shape
    G = FFT_GROUP
    return pl.pallas_call(
        functools.partial(_fft_spec_body, Bn=Bn), grid=(A // G, S),
        in_specs=[pl.BlockSpec((1, 2, G, Bn, C), lambda g, s: (s, 0, g, 0, 0)),
                  pl.BlockSpec((G, 2 * Bn, 2 * Bn), lambda g, s: (g, 0, 0))],
        out_specs=pl.BlockSpec((1, G, 2, Bn, C), lambda g, s: (s, g, 0, 0, 0)),
        out_shape=jax.ShapeDtypeStruct((S, A, 2, Bn, C), F32),
        compiler_params=_cparams("parallel", "parallel"), name="hy_filter_spec",
    )(p5, mf)


def _fft_mid_body(p_ref, mf_ref, mi_ref, k_ref, q_ref, *, Bn):
    for j in range(FFT_GROUP):
        X = jnp.dot(mf_ref[j], _fast_operand(p_ref, j), preferred_element_type=F32)
        xr, xi = X[:Bn], X[Bn:]
        kr, ki = k_ref[j, 0], k_ref[j, 1]
        Y = jnp.concatenate([xr * kr - xi * ki, xr * ki + xi * kr], axis=0).astype(BF16)
        Q = jnp.dot(mi_ref[j], Y, preferred_element_type=F32)
        q_ref[0, j, 0] = Q[:Bn]
        q_ref[0, j, 1] = Q[Bn:]


def _fft_mid(p5, mf, mi, kspec):
    S, _, A, Bn, C = p5.shape
    G = FFT_GROUP
    mat = pl.BlockSpec((G, 2 * Bn, 2 * Bn), lambda g, s: (g, 0, 0))
    return pl.pallas_call(
        functools.partial(_fft_mid_body, Bn=Bn), grid=(A // G, S),
        in_specs=[pl.BlockSpec((1, 2, G, Bn, C), lambda g, s: (s, 0, g, 0, 0)), mat, mat,
                  pl.BlockSpec((G, 2, Bn, C), lambda g, s: (g, 0, 0, 0))],
        out_specs=pl.BlockSpec((1, G, 2, Bn, C), lambda g, s: (s, g, 0, 0, 0)),
        out_shape=jax.ShapeDtypeStruct((S, A, 2, Bn, C), F32),
        compiler_params=_cparams("parallel", "parallel"), name="hy_dft_fast",
    )(p5, mf, mi, kspec)


def _fft4_body(d_ref, q_ref, z_ref, g_ref, dch_ref, o_ref, *, Bn):
    _, A2, bt, C = q_ref.shape
    Ah = d_ref.shape[0] // bt
    b0 = pl.multiple_of(pl.program_id(1) * bt, bt)
    qblk = q_ref[0].reshape(A2 * bt, C).astype(BF16)
    y = jnp.dot(d_ref[...], qblk, preferred_element_type=F32)
    dch = dch_ref[...]
    for a in range(Ah):
        rows = pl.ds(a * Bn + b0, bt)
        ya = y[a * bt:(a + 1) * bt]
        for h in range(HY_HALVES):
            lanes = slice(h * LANES, (h + 1) * LANES)
            o_ref[h, rows, :] = g_ref[h, rows, :] * (ya[:, lanes] + z_ref[h, rows, :] * dch[:, lanes])


def _fft4(d4, q4, z, gate, dch):
    S, A2, Bn, C = q4.shape
    Ah = d4.shape[0]
    L = Ah * Bn
    bt = min(FFT_BT, Bn)
    dbig = jnp.kron(d4, jnp.eye(bt, dtype=F32)).astype(BF16)
    sig = pl.BlockSpec((HY_HALVES, L, LANES), lambda s, j: (0, s, 0))
    return pl.pallas_call(
        functools.partial(_fft4_body, Bn=Bn), grid=(S, Bn // bt),
        in_specs=[pl.BlockSpec(dbig.shape, lambda s, j: (0, 0)),
                  pl.BlockSpec((1, A2, bt, C), lambda s, j: (s, 0, j, 0)), sig, sig,
                  pl.BlockSpec((1, C), lambda s, j: (0, 0))],
        out_specs=sig,
        out_shape=jax.ShapeDtypeStruct((HY_HALVES, S * L, LANES), F32),
        compiler_params=_cparams("parallel", "arbitrary"), name="hy_idft_gate",
    )(dbig, q4, z, gate, dch)


def _hyena(v, x1, x2, n_seq, L, p):
    C = HY_CH
    A, Bn = _fft_plan(L)
    Ah = A // 2
    d1, d4 = _dft_small(A, A * Bn)
    mf, mi = _dft_mid(A, Bn)
    d4 = d4.reshape(A, 2, A).transpose(0, 2, 1).reshape(A, 2 * A)
    kn = _hy_filters(L, p).reshape(L, -1, 2, C)
    n_order = kn.shape[1]
    zero = jnp.zeros((1, C), F32)
    k2 = jnp.concatenate([piece for o in range(n_order)
                          for piece in (kn[:, o, 0], zero, jnp.flip(kn[1:, o, 1], axis=0))], axis=0)
    k2 = k2.reshape(-1, HY_HALVES, LANES).transpose(1, 0, 2)
    kspec = _fft_spec(_fft1(d1, k2, n_order, Bn).reshape(n_order, 2, A, Bn, C), mf)
    z = v
    for o, gate in enumerate((x1, x2)):
        P = _fft1(d1[:, :Ah], z, n_seq, Bn).reshape(n_seq, 2, A, Bn, C)
        Q = _fft_mid(P, mf, mi, kspec[o]).reshape(n_seq, 2 * A, Bn, C)
        z = _fft4(d4[:Ah], Q, z, gate, p['hy_bias_d'][o][None])
    return z


def _ml_prep_body(x_ref, p_ref, n_ref, w_ref, b_ref, q_ref, kt_ref, *, regions):
    at_start, at_end = _seq_edges(pl.program_id(0), regions)
    y = _conv3(x_ref[...], p_ref[SUBLANES - 1:SUBLANES, :], n_ref[0:1, :], w_ref[...], b_ref[...], at_start, at_end)
    y = y * jax.nn.sigmoid(y)
    q_ref[...] = y[:, :ML_W].astype(BF16)
    kt_ref[...] = (y[:, ML_W:] * (ML_HD ** -0.5)).T


def _ml_prep(u_ml, w, b, regions):
    M = u_ml.shape[0]
    tm = ROW_TILE
    n = sum(r[1] for r in regions)
    return pl.pallas_call(
        functools.partial(_ml_prep_body, regions=regions), grid=(n,),
        in_specs=_halo_specs(tm, 2 * ML_W, 0, regions, M) + [pl.BlockSpec((3, 2 * ML_W), lambda i: (0, 0)),
                                                             pl.BlockSpec((1, 2 * ML_W), lambda i: (0, 0))],
        out_specs=[pl.BlockSpec((tm, ML_W), lambda i: (_seq_tile(i, regions), 0)),
                   pl.BlockSpec((ML_W, tm), lambda i: (0, _seq_tile(i, regions)))],
        out_shape=[jax.ShapeDtypeStruct((M, ML_W), BF16), jax.ShapeDtypeStruct((ML_W, M), F32)],
        compiler_params=_cparams("parallel"), name="ml_prep",
    )(u_ml, u_ml, u_ml, w, b[None])


def _ml_body(*refs):
    ins, outs, scr = refs[:8], refs[8:10], refs[10:]

    @pl.when(pl.program_id(1) == 0)
    def _():
        for ref in scr:
            ref[...] = jnp.zeros_like(ref)

    for d in range(2):
        _ml_chunk(d, *ins[4 * d:4 * d + 4], outs[d], *scr[2 * d:2 * d + 2])


def _ml_chunk(d, q_ref, kt_ref, v_ref, g_ref, o_ref, st_ref, m_ref):
    T = q_ref.shape[0]
    H, W = ML_HEADS, ML_W
    WA = W + LANES
    sgn = 1 - 2 * d
    g = g_ref[...]
    gs = g[2 * H * d:2 * H * (d + 1)]
    ig = gs[0:H]
    lf8 = -(jnp.maximum(-gs, 0.0) + jnp.log1p(jnp.exp(-jnp.abs(gs))))
    lf = lf8[H:2 * H]
    r_i = lax.broadcasted_iota(jnp.int32, (T, T), 0)
    c_i = lax.broadcasted_iota(jnp.int32, (T, T), 1)
    prec = ((c_i - r_i) * sgn) <= 0
    incl = jnp.where(((r_i - c_i) * sgn) <= 0, 1.0, 0.0).astype(BF16)
    after = jnp.where(((c_i - r_i) * sgn) < 0, 1.0, 0.0).astype(BF16)
    b_rows = _dot2_exact_rhs(lf8, incl)[H:2 * H]
    bL = jnp.sum(lf, axis=1, keepdims=True)
    a_row = bL - b_rows + ig
    m_loc = jnp.max(a_row, axis=1, keepdims=True)
    w_row = jnp.exp(a_row - m_loc)
    m0 = m_ref[0:H, 0:1]
    m_new = jnp.maximum(bL + m0, m_loc)
    s_old = jnp.exp(bL + m0 - m_new)
    s_loc = jnp.exp(m_loc - m_new)

    qb = q_ref[...]
    kt = kt_ref[...]
    row_head = lax.broadcasted_iota(jnp.int32, (W, T), 0) // ML_HD
    lane = lax.broadcasted_iota(jnp.int32, (1, WA), 1)
    lane_head = jnp.where(lane < W, lane // ML_HD, lane - W)
    v_aug = jnp.concatenate([v_ref[...], jnp.ones((T, LANES), F32)], axis=1)

    Lfs = [jnp.where(prec, lf[h:h + 1], 0.0) for h in range(H)]
    E_all = _dot2_exact_rhs(jnp.concatenate(Lfs, axis=0), after)
    kth_all = jnp.concatenate([jnp.where(row_head == h, kt, 0.0) for h in range(H)], axis=1).astype(BF16)
    S_all = jnp.dot(qb, kth_all, preferred_element_type=F32)

    ps, vbd = [], []
    w_inter = jnp.zeros((T, WA), F32)
    e_m = jnp.zeros((T, WA), F32)
    for h in range(H):
        b_col = jnp.sum(Lfs[h], axis=1, keepdims=True)
        Dm = jnp.where(prec, E_all[h * T:(h + 1) * T] + ig[h:h + 1], -jnp.inf)
        inter = b_col + m0[h:h + 1]
        m_col = jnp.maximum(inter, jnp.max(Dm, axis=1, keepdims=True))
        S = S_all[:, h * T:(h + 1) * T]
        ps.append((jnp.exp(Dm - m_col) * S).astype(BF16))
        sel = lane_head == h
        vbd.append(jnp.where(sel, v_aug, 0.0).astype(BF16))
        w_inter = w_inter + jnp.where(sel, jnp.exp(inter - m_col), 0.0)
        e_m = e_m + jnp.where(sel, jnp.exp(-m_col), 0.0)
    nd = jnp.dot(jnp.concatenate(ps, axis=1), jnp.concatenate(vbd, axis=0), preferred_element_type=F32)
    nd = nd + w_inter * jnp.dot(qb, st_ref[...].astype(BF16), preferred_element_type=F32)
    den = jnp.zeros((T, W), F32)
    for h in range(H):
        den = den + jnp.where(lane_head[:, :W] == h, nd[:, W + h:W + h + 1], 0.0)
    o_ref[...] = nd[:, :W] / jnp.maximum(jnp.abs(den), e_m[:, :W])

    wk = jnp.zeros((W, T), F32)
    scol = jnp.zeros((1, WA), F32)
    ws = w_row * s_loc
    for h in range(H):
        wk = wk + jnp.where(row_head == h, ws[h:h + 1], 0.0)
        scol = scol + jnp.where(lane_head == h, s_old[h:h + 1], 0.0)
    st_loc = jnp.dot((kt * wk).astype(BF16), v_aug.astype(BF16), preferred_element_type=F32)
    diag = (lax.broadcasted_iota(jnp.int32, (W, WA), 0) // ML_HD) == lane_head
    st_ref[...] = jnp.where(diag, st_ref[...] * scol + st_loc, 0.0)
    m_ref[0:H, :] = jnp.broadcast_to(m_new, (H, LANES))


def _mlstm(q, kt, u_ml, g_t, B, L, Lc):
    M = q.shape[0]
    T = ML_CHUNK
    nC, nL = Lc // T, L // T
    lat0 = 0
    ctx0 = (B * L) // T

    def blk(d, b, i):
        cc = i if d == 0 else nC - 1 - i
        j = i - nC
        cl = j if d == 0 else nL - 1 - j
        return jnp.where(i < nC, ctx0 + b * nC + cc, lat0 + b * nL + cl)

    in_specs, out_specs = [], []
    for d in range(2):
        rows = functools.partial(lambda b, i, d, c: (blk(d, b, i), c), d=d)
        cols = functools.partial(lambda b, i, d: (0, blk(d, b, i)), d=d)
        in_specs += [pl.BlockSpec((T, ML_W), functools.partial(rows, c=0)), pl.BlockSpec((ML_W, T), cols),
                     pl.BlockSpec((T, ML_W), functools.partial(rows, c=2)), pl.BlockSpec((4 * ML_HEADS, T), cols)]
        out_specs.append(pl.BlockSpec((T, ML_W), functools.partial(rows, c=0)))
    state = [pltpu.VMEM((ML_W, ML_W + LANES), F32), pltpu.VMEM((SUBLANES, LANES), F32)]
    return pl.pallas_call(
        _ml_body, grid=(B, nC + nL), in_specs=in_specs, out_specs=out_specs,
        out_shape=[jax.ShapeDtypeStruct((M, ML_W), F32)] * 2,
        scratch_shapes=state * 2,
        compiler_params=_cparams("parallel", "arbitrary"), name="mlstm",
    )(*([q, kt, u_ml, g_t] * 2))


def _rope_tables(L):
    rows = L // GRID_W
    row = jnp.repeat(jnp.arange(rows, dtype=F32), GRID_W)
    col = jnp.tile(jnp.arange(GRID_W, dtype=F32), rows)
    half = DA_HD // 2
    inv = ROPE_THETA ** (-jnp.arange(0, half, 2, dtype=F32) / half)
    ang = jnp.stack([row, col], axis=-1)[:, :, None] * inv
    ang = jnp.stack([ang, ang], axis=-2).reshape(-1, DA_HD)
    ang = jnp.concatenate([ang, ang], axis=1)
    return jnp.cos(ang), jnp.sin(ang)


def _da_prep_body(u_ref, cos_ref, sin_ref, q_ref, k_ref, va_ref, *, n_lat_tiles):
    is_lat = pl.program_id(0) < n_lat_tiles
    u = u_ref[...]
    reps = DA_QK // cos_ref.shape[1]
    cs = jnp.concatenate([cos_ref[...]] * reps, axis=1)
    sn = jnp.concatenate([sin_ref[...]] * reps, axis=1)
    q4 = DA_HD // 4
    src = lax.broadcasted_iota(jnp.int32, (LANES, LANES), 0)
    dst = lax.broadcasted_iota(jnp.int32, (LANES, LANES), 1)
    first = (dst % (2 * q4)) < q4
    perm = jnp.where(first & (src == dst + q4), -1.0, jnp.where((~first) & (src == dst - q4), 1.0, 0.0)).astype(BF16)

    def rope(x):
        rot = jnp.concatenate([_dot2_exact_rhs(x[:, g * LANES:(g + 1) * LANES], perm)
                               for g in range(x.shape[1] // LANES)], axis=1)
        return jnp.where(is_lat, x * cs + rot * sn, x)

    q_ref[...] = (rope(u[:, :DA_QK]) * (DA_HD ** -0.5 * math.log2(math.e))).astype(BF16)
    k_ref[...] = rope(u[:, DA_QK:2 * DA_QK]).astype(BF16)
    va_ref[...] = u[:, 2 * DA_QK:].T.astype(BF16)


def _da_prep(u_da, cos, sin, n_lat_tiles, tiles_per_seq):
    M = u_da.shape[0]
    tm = ROW_TILE
    tab = pl.BlockSpec((tm, cos.shape[1]), lambda i: (jnp.where(i < n_lat_tiles, i % tiles_per_seq, 0), 0))
    out = pl.BlockSpec((tm, DA_QK), lambda i: (i, 0))
    return pl.pallas_call(
        functools.partial(_da_prep_body, n_lat_tiles=n_lat_tiles), grid=(M // tm,),
        in_specs=[pl.BlockSpec((tm, u_da.shape[1]), lambda i: (i, 0)), tab, tab],
        out_specs=[out, out, pl.BlockSpec((DA_V, tm), lambda i: (0, i))],
        out_shape=[jax.ShapeDtypeStruct((M, DA_QK), BF16)] * 2 + [jax.ShapeDtypeStruct((DA_V, M), BF16)],
        compiler_params=_cparams("parallel"), name="da_prep",
    )(u_da, cos, sin)


def _da_body(lam_ref, w_ref, q_ref, *rest, nseg, lam_init, sub):
    ks, vas, o_ref = rest[:nseg], rest[nseg:2 * nseg], rest[2 * nseg]
    lp = lam_ref[...]
    lam = (jnp.exp(jnp.sum(lp[0:1] * lp[1:2], axis=1, keepdims=True))
           - jnp.exp(jnp.sum(lp[2:3] * lp[3:4], axis=1, keepdims=True)) + lam_init)
    HW = 2 * DA_HD
    lane = lax.broadcasted_iota(jnp.int32, (1, HW), 1)
    n_sub = q_ref.shape[0] // sub
    def scores(i):
        t, m = divmod(i, 2)
        q = q_ref[t * sub:(t + 1) * sub, :]
        qm = jnp.where((lane // DA_HD) == m, q, jnp.zeros_like(q))
        return [lax.dot_general(k[...], qm, (((1,), (1,)), ((), ())), preferred_element_type=F32) for k in ks]

    def softmax(ss):
        mx = functools.reduce(jnp.maximum, [jnp.max(s, axis=0, keepdims=True) for s in ss])
        ps = [jnp.exp2(s - mx) for s in ss]
        den = functools.reduce(jnp.add, [jnp.sum(pr, axis=0, keepdims=True) for pr in ps])
        return [pr.astype(BF16) for pr in ps], den

    def values(ps, den):
        acc = functools.reduce(jnp.add, [jnp.dot(va[...], pr, preferred_element_type=F32)
                                         for pr, va in zip(ps, vas)])
        return acc / den

    n_streams = 2 * n_sub
    sss = {0: scores(0)}
    if n_streams > 1:
        sss[1] = scores(1)
    outs = []
    for i in range(n_streams):
        pd = softmax(sss.pop(i))
        if i + 2 < n_streams:
            sss[i + 2] = scores(i + 2)
        outs.append(values(*pd))
    for t in range(n_sub):
        o = outs[2 * t] - lam * outs[2 * t + 1]
        ms = jnp.mean(o * o, axis=0, keepdims=True)
        o_ref[t * sub:(t + 1) * sub, :] = ((o * lax.rsqrt(ms + NORM_EPS) * w_ref[...]) * (1.0 - lam_init)).T


def _diff_attn(qa, ka, vaa, lam_p, subln_w, lam_init, n_batch, q_rows0, q_len, segs):
    tq = min(ATT_Q_TILE, q_len)
    HW = 2 * DA_HD
    nq = q_len // tq
    q0 = q_rows0 // tq
    k_specs = [pl.BlockSpec((n, HW), functools.partial(lambda b, h, i, f, n: (f // n + b, h), f=f, n=n))
               for (f, n) in segs]
    va_specs = [pl.BlockSpec((HW, n), functools.partial(lambda b, h, i, f, n: (h, f // n + b), f=f, n=n))
                for (f, n) in segs]
    return pl.pallas_call(
        functools.partial(_da_body, nseg=len(segs), lam_init=lam_init, sub=min(ATT_SUB_TILE, tq)),
        grid=(n_batch, DA_HEADS, nq),
        in_specs=[pl.BlockSpec(lam_p.shape, lambda b, h, i: (0, 0)), pl.BlockSpec((HW, 1), lambda b, h, i: (0, 0)),
                  pl.BlockSpec((tq, HW), lambda b, h, i: (q0 + b * nq + i, h))] + k_specs + va_specs,
        out_specs=pl.BlockSpec((tq, HW), lambda b, h, i: (b * nq + i, h)),
        out_shape=jax.ShapeDtypeStruct((n_batch * q_len, DA_V), F32),
        compiler_params=_cparams("parallel", "parallel", "parallel"), name="diff_attn",
    )(lam_p, subln_w[:, None], qa, *([ka] * len(segs)), *([vaa] * len(segs)))


def _proj_out_body(x_ref, mod_ref, hy_ref, hf_ref, hb_ref, og_ref, da_ref, mw_ref, why_ref, wml_ref, wda_ref, o_ref):
    hs = hf_ref[...] + hb_ref[...]
    W = hs.shape[1]
    r = lax.broadcasted_iota(jnp.int32, (W, W), 0) // ML_HD
    c = lax.broadcasted_iota(jnp.int32, (W, W), 1) // ML_HD
    same_head = jnp.where(r == c, 1.0, 0.0).astype(BF16)
    ms = _dot2_exact_rhs(hs * hs, same_head) * (1.0 / ML_HD)
    y_ml = jax.nn.sigmoid(og_ref[...]) * (hs * lax.rsqrt(ms + NORM_EPS) * mw_ref[...])
    y_hy = jnp.concatenate([hy_ref[h] for h in range(HY_HALVES)], axis=1)
    y = (_dot(y_hy, why_ref[...]) + _dot(y_ml, wml_ref[...])) + _dot(da_ref[...], wda_ref[...])
    o_ref[...] = x_ref[...] + mod_ref[0, 2:3, :] * y


def _proj_out(xa, mod, y_hy, h_ml, u_ml, y_da, ml_norm_w, w_out, n_rows, n_lat_tiles, tiles_per_batch, n_batch):
    D = xa.shape[1]
    tm = ROW_TILE
    wb = w_out.astype(BF16)
    why, wml, wda = wb[:HY_CH], wb[HY_CH:HY_CH + ML_W], wb[HY_CH + ML_W:]
    full = lambda a: pl.BlockSpec(a.shape, lambda i: (0,) * a.ndim)
    row = lambda wd, cb=0: pl.BlockSpec((tm, wd), lambda i: (i, cb))
    mw = ml_norm_w[None]
    return pl.pallas_call(
        _proj_out_body, grid=(n_rows // tm,),
        in_specs=[row(D), pl.BlockSpec((1,) + mod.shape[1:],
                                       lambda i: (_mod_index(i, n_lat_tiles, tiles_per_batch, n_batch), 0, 0)),
                  pl.BlockSpec((HY_HALVES, tm, LANES), lambda i: (0, i, 0)),
                  row(ML_W), row(ML_W), row(ML_W, 3), row(DA_V),
                  full(mw), full(why), full(wml), full(wda)],
        out_specs=row(D),
        out_shape=jax.ShapeDtypeStruct((n_rows, D), F32),
        compiler_params=_cparams("parallel"), name="proj_out",
    )(xa, mod, y_hy, *h_ml, u_ml, y_da, mw, why, wml, wda)


ROUTE_LANE0 = MOE_GROUPS


def _router_body(x_ref, mod_ref, nw_ref, wr_ref, br_ref, h_ref, ri_ref, rf_ref, cnt_ref, run_ref):
    @pl.when(pl.program_id(0) == 0)
    def _():
        run_ref[...] = jnp.zeros_like(run_ref)

    h = _norm_mod(x_ref[...], nw_ref[...], mod_ref[0, 3:4, :], mod_ref[0, 4:5, :])
    h_ref[...] = h
    lg = _dot3(h, wr_ref[...]) + br_ref[...]
    tm = lg.shape[0]
    lane = lax.broadcasted_iota(jnp.int32, lg.shape, 1)
    neg = -jnp.inf
    is_g = lane < MOE_GROUPS
    gl = jnp.where(is_g, lg, neg)
    gmax = jnp.max(gl, axis=1, keepdims=True)
    gidx = jnp.min(jnp.where(gl == gmax, lane, LANES), axis=1, keepdims=True)
    gw = 1.0 / jnp.sum(jnp.where(is_g, jnp.exp(gl - gmax), 0.0), axis=1, keepdims=True)
    e_of = lane - ROUTE_LANE0
    in_grp = (e_of >= 0) & (e_of < MOE_EXPERTS) & ((e_of // MOE_PER_GROUP) == gidx)
    el = jnp.where(in_grp, lg, neg)
    t1 = jnp.max(el, axis=1, keepdims=True)
    i1 = jnp.min(jnp.where(el == t1, lane, LANES), axis=1, keepdims=True)
    el2 = jnp.where(lane == i1, neg, el)
    t2 = jnp.max(el2, axis=1, keepdims=True)
    i2 = jnp.min(jnp.where(el2 == t2, lane, LANES), axis=1, keepdims=True)
    ex = jnp.exp(t2 - t1)
    g1 = gw / (1.0 + ex)
    g2 = gw * ex / (1.0 + ex)
    oh = jnp.where((lane == i1) | (lane == i2), 1.0, 0.0)
    r_i = lax.broadcasted_iota(jnp.int32, (tm, tm), 0)
    c_i = lax.broadcasted_iota(jnp.int32, (tm, tm), 1)
    earlier = jnp.where(c_i < r_i, 1.0, 0.0).astype(BF16)
    cum = jnp.dot(earlier, oh.astype(BF16), preferred_element_type=F32) + run_ref[0:1, :]
    r1 = jnp.sum(jnp.where(lane == i1, cum, 0.0), axis=1, keepdims=True).astype(jnp.int32)
    r2 = jnp.sum(jnp.where(lane == i2, cum, 0.0), axis=1, keepdims=True).astype(jnp.int32)
    run = run_ref[0:1, :] + jnp.sum(oh, axis=0, keepdims=True)
    run_ref[...] = jnp.broadcast_to(run, run_ref.shape)
    cnt_ref[...] = jnp.broadcast_to(run, cnt_ref.shape)
    zi = jnp.zeros_like(lane)
    ri_ref[...] = jnp.where(lane == 0, i1 - ROUTE_LANE0, jnp.where(lane == 1, i2 - ROUTE_LANE0,
                            jnp.where(lane == 2, r1, jnp.where(lane == 3, r2, zi))))
    rf_ref[...] = jnp.where(lane == 0, g1, jnp.where(lane == 1, g2, 0.0))


def _router(xa, mod, nw, wg, bg, we, be, n_rows, n_lat_tiles, tiles_per_batch, n_batch):
    D = xa.shape[1]
    tm = ROW_TILE
    pad = LANES - MOE_GROUPS - MOE_EXPERTS
    wr = jnp.concatenate([wg, we, jnp.zeros((D, pad), F32)], axis=1)
    br = jnp.concatenate([bg, be, jnp.zeros((pad,), F32)])[None]
    row = lambda wd: pl.BlockSpec((tm, wd), lambda i: (i, 0))
    full = lambda a: pl.BlockSpec(a.shape, lambda i: (0,) * a.ndim)
    return pl.pallas_call(
        _router_body, grid=(n_rows // tm,),
        in_specs=[row(D), pl.BlockSpec((1,) + mod.shape[1:],
                                       lambda i: (_mod_index(i, n_lat_tiles, tiles_per_batch, n_batch), 0, 0)),
                  full(nw[None]), full(wr), full(br)],
        out_specs=[row(D), row(LANES), row(LANES), pl.BlockSpec((SUBLANES, LANES), lambda i: (0, 0))],
        out_shape=[jax.ShapeDtypeStruct((n_rows, D), F32), jax.ShapeDtypeStruct((n_rows, LANES), jnp.int32),
                   jax.ShapeDtypeStruct((n_rows, LANES), F32), jax.ShapeDtypeStruct((SUBLANES, LANES), F32)],
        scratch_shapes=[pltpu.VMEM((SUBLANES, LANES), F32)],
        compiler_params=_cparams("arbitrary"), name="moe_router",
    )(xa, mod, nw[None], wr, br)


def _plan_body(ri_ref, base_ref, d_ref):
    ri = ri_ref[...]
    lane = lax.broadcasted_iota(jnp.int32, ri.shape, 1)
    base = base_ref[...]
    d = []
    for j in range(2):
        b = jnp.sum(jnp.where(lane == ri[:, j:j + 1], base, 0.0), axis=1, keepdims=True)
        d.append(b + ri[:, 2 + j:3 + j].astype(F32))
    slots = jnp.where(lane == 0, d[0], jnp.where(lane == 1, d[1], 0.0))
    d_ref[0] = slots.T[0:SUBLANES, :].astype(jnp.int32)


def _plan(ri, base):
    n_rows = ri.shape[0]
    tm = ROW_TILE
    basef = jnp.pad(base.astype(F32), (0, LANES - base.shape[0]))[None]
    out = pl.pallas_call(
        _plan_body, grid=(n_rows // tm,),
        in_specs=[pl.BlockSpec((tm, LANES), lambda i: (i, 0)), pl.BlockSpec((1, LANES), lambda i: (0, 0))],
        out_specs=pl.BlockSpec((1, SUBLANES, tm), lambda i: (i, 0, 0)),
        out_shape=jax.ShapeDtypeStruct((n_rows // tm, SUBLANES, tm), jnp.int32),
        compiler_params=_cparams("parallel"), name="moe_plan",
    )(ri, basef)
    return out[:, 0, :], out[:, 1, :]


def _dispatch_body(d0_ref, d1_ref, lb_ref, nb_ref, h_ref, xb_ref, zbuf, sem, zsem):
    i = pl.program_id(0)
    tm = h_ref.shape[0]
    TB = zbuf.shape[0]
    n_blk = xb_ref.shape[0] // TB

    def zero_copy(blk):
        return pltpu.make_async_copy(zbuf, xb_ref.at[pl.ds(pl.multiple_of(blk * TB, TB), TB)], zsem)

    @pl.when(i == 0)
    def _():
        zbuf[...] = jnp.zeros_like(zbuf)
        for phase in ("start", "wait"):
            def tail(blk, carry, phase=phase):
                getattr(zero_copy(blk), phase)()
                return carry

            for e in range(lb_ref.shape[0]):
                @pl.when(lb_ref[e] >= 0)
                def _(e=e, phase=phase):
                    getattr(zero_copy(lb_ref[e]), phase)()
            lax.fori_loop(nb_ref[0], n_blk, tail, 0)

    for r in range(tm):
        for d_ref in (d0_ref, d1_ref):
            pltpu.make_async_copy(h_ref.at[pl.ds(r, 1)], xb_ref.at[pl.ds(d_ref[i, r], 1)], sem).start()
    pltpu.make_async_copy(xb_ref.at[pl.ds(0, 2 * tm)], xb_ref.at[pl.ds(0, 2 * tm)], sem).wait()


def _dispatch(d0, d1, last_blk, n_used, h2, n_slot_rows):
    n_rows, D = h2.shape
    tm = ROW_TILE
    return pl.pallas_call(
        _dispatch_body,
        grid_spec=pltpu.PrefetchScalarGridSpec(
            num_scalar_prefetch=4, grid=(n_rows // tm,),
            in_specs=[pl.BlockSpec((tm, D), lambda i, *_: (i, 0))],
            out_specs=pl.BlockSpec(memory_space=pl.ANY),
            scratch_shapes=[pltpu.VMEM((MOE_ROWS, D), F32), pltpu.SemaphoreType.DMA(()),
                            pltpu.SemaphoreType.DMA(())]),
        out_shape=jax.ShapeDtypeStruct((n_slot_rows, D), F32),
        compiler_params=pltpu.CompilerParams(dimension_semantics=("arbitrary",), vmem_limit_bytes=VMEM_LIMIT,
                                             disable_bounds_checks=True),
        name="moe_dispatch",
    )(d0, d1, last_blk, n_used, h2)


def _ffn_body(be_ref, nb_ref, x_ref, w1_ref, w3_ref, w2_ref, y_ref, w1b, w3b, w2b):
    i = pl.program_id(0)
    used = i < nb_ref[0]
    new_expert = (i == 0) | (be_ref[i] != be_ref[jnp.maximum(i - 1, 0)])

    @pl.when(used & new_expert)
    def _():
        w1b[...] = w1_ref[0, 0].astype(BF16)
        w3b[...] = w3_ref[0, 0].astype(BF16)
        w2b[...] = w2_ref[0, 0].astype(BF16)

    @pl.when(used)
    def _():
        xb = x_ref[...].astype(BF16)
        a = jnp.dot(xb, w1b[...], preferred_element_type=F32)
        b = jnp.dot(xb, w3b[...], preferred_element_type=F32)
        hmid = ((a * jax.nn.sigmoid(a)) * b).astype(BF16)
        y_ref[...] = jnp.dot(hmid, w2b[...], preferred_element_type=F32)

    @pl.when(jnp.logical_not(used))
    def _():
        y_ref[...] = jnp.zeros_like(y_ref)


def _expert_ffn(blk_e, n_used, xb, w1, w3, w2, layer):
    P, D = xb.shape
    F = w1.shape[3]
    TB = MOE_ROWS
    rows = pl.BlockSpec((TB, D), lambda i, be, nb: (i, 0))
    return pl.pallas_call(
        _ffn_body,
        grid_spec=pltpu.PrefetchScalarGridSpec(
            num_scalar_prefetch=2, grid=(P // TB,),
            in_specs=[rows, pl.BlockSpec((1, 1, D, F), lambda i, be, nb: (layer, be[i], 0, 0)),
                      pl.BlockSpec((1, 1, D, F), lambda i, be, nb: (layer, be[i], 0, 0)),
                      pl.BlockSpec((1, 1, F, D), lambda i, be, nb: (layer, be[i], 0, 0))],
            out_specs=rows,
            scratch_shapes=[pltpu.VMEM((D, F), BF16), pltpu.VMEM((D, F), BF16), pltpu.VMEM((F, D), BF16)]),
        out_shape=jax.ShapeDtypeStruct((P, D), F32),
        compiler_params=_cparams("arbitrary"), name="moe_ffn",
    )(blk_e, n_used, xb, w1, w3, w2)


def _combine_body(d0_ref, d1_ref, x_ref, mod_ref, rf_ref, yb_ref, fw_ref, o_ref, buf, sem, *, final):
    i = pl.program_id(0)
    n = pl.num_programs(0)
    tm = x_ref.shape[0]

    def gather(step, slot):
        for r in range(tm):
            for j, d_ref in enumerate((d0_ref, d1_ref)):
                pltpu.make_async_copy(yb_ref.at[pl.ds(d_ref[step, r], 1)], buf.at[slot, j, pl.ds(r, 1)],
                                      sem.at[slot]).start()

    @pl.when(i == 0)
    def _():
        gather(0, 0)

    slot = i % 2
    pltpu.make_async_copy(buf.at[slot], buf.at[slot], sem.at[slot]).wait()

    @pl.when(i + 1 < n)
    def _():
        gather(i + 1, 1 - slot)

    g = rf_ref[...]
    f = g[:, 0:1] * buf[slot, 0] + g[:, 1:2] * buf[slot, 1]
    xn = x_ref[...] + mod_ref[0, 5:6, :] * f
    if final:
        ms = jnp.mean(xn * xn, axis=-1, keepdims=True)
        xn = xn * lax.rsqrt(ms + NORM_EPS) * fw_ref[...]
    o_ref[...] = xn


def _combine(d0, d1, xa, mod, rf, yb, final_w, final, n_rows, n_lat_tiles, tiles_per_batch, n_batch):
    D = xa.shape[1]
    tm = ROW_TILE
    row = lambda wd: pl.BlockSpec((tm, wd), lambda i, *_: (i, 0))
    return pl.pallas_call(
        functools.partial(_combine_body, final=final),
        grid_spec=pltpu.PrefetchScalarGridSpec(
            num_scalar_prefetch=2, grid=(n_rows // tm,),
            in_specs=[row(D), pl.BlockSpec((1,) + mod.shape[1:],
                                           lambda i, *_: (_mod_index(i, n_lat_tiles, tiles_per_batch, n_batch), 0, 0)),
                      row(LANES), pl.BlockSpec(memory_space=pl.ANY), pl.BlockSpec((1, D), lambda i, *_: (0, 0))],
            out_specs=row(D),
            scratch_shapes=[pltpu.VMEM((2, 2, tm, D), F32), pltpu.SemaphoreType.DMA((2,))]),
        out_shape=jax.ShapeDtypeStruct((n_rows, D), F32),
        compiler_params=pltpu.CompilerParams(dimension_semantics=("arbitrary",), vmem_limit_bytes=VMEM_LIMIT,
                                             disable_bounds_checks=True),
        name="moe_combine",
    )(d0, d1, xa, mod, rf, yb, final_w[None])


def _moe(xa, mod, p, expert_w, layer, final_w, final, n_rows, n_lat_tiles, tiles_per_batch, n_batch):
    TB = MOE_ROWS
    h2, ri, rf, cnt = _router(xa, mod, p['norm2_w'], p['moe_wg'], p['moe_bg'], p['moe_we'], p['moe_be'],
                              n_rows, n_lat_tiles, tiles_per_batch, n_batch)
    counts = cnt[0, ROUTE_LANE0:ROUTE_LANE0 + MOE_EXPERTS].astype(jnp.int32)
    pc = (counts + TB - 1) // TB * TB
    pend = jnp.cumsum(pc)
    base = pend - pc
    n_blk = -(-2 * n_rows // TB) + MOE_EXPERTS
    n_used = (pend[-1] // TB).astype(jnp.int32)
    blk = jnp.arange(n_blk, dtype=jnp.int32)
    blk_e = jnp.sum((pend[None, :] <= (jnp.minimum(blk, n_used - 1) * TB)[:, None]).astype(jnp.int32), axis=1)
    blk_e = jnp.minimum(blk_e, MOE_EXPERTS - 1).astype(jnp.int32)
    last_blk = jnp.where(pc > 0, pend // TB - 1, -1).astype(jnp.int32)
    d0, d1 = _plan(ri, base)
    xb = _dispatch(d0, d1, last_blk, n_used[None], h2, n_blk * TB)
    yb = _expert_ffn(blk_e, n_used[None], xb, *expert_w, layer)
    return _combine(d0, d1, xa, mod, rf, yb, final_w, final, n_rows, n_lat_tiles, tiles_per_batch, n_batch)


_LAYER_KEYS = ('ada_w', 'ada_b', 'norm1_w', 'norm2_w', 'w_in', 'b_in', 'w_out', 'hy_conv_w', 'hy_conv_b',
               'hy_filt_w1', 'hy_filt_b1', 'hy_filt_w2', 'hy_filt_b2', 'hy_filt_w3', 'hy_filt_b3', 'hy_sin_freq',
               'hy_bias_d', 'ml_conv_w', 'ml_conv_b', 'ml_norm_w', 'da_lambda', 'da_subln_w', 'moe_wg', 'moe_bg',
               'moe_we', 'moe_be', 'moe_w1', 'moe_w3', 'moe_w2')


def kernel(x, c, ctx, c_ctx, ada_w, ada_b, norm1_w, norm2_w, w_in, b_in, w_out, hy_conv_w, hy_conv_b, hy_filt_w1,
           hy_filt_b1, hy_filt_w2, hy_filt_b2, hy_filt_w3, hy_filt_b3, hy_sin_freq, hy_bias_d, ml_conv_w, ml_conv_b,
           ml_norm_w, da_lambda, da_subln_w, moe_wg, moe_bg, moe_we, moe_be, moe_w1, moe_w3, moe_w2, final_norm_w):
    stacked = dict(zip(_LAYER_KEYS, (ada_w, ada_b, norm1_w, norm2_w, w_in, b_in, w_out, hy_conv_w, hy_conv_b,
                                     hy_filt_w1, hy_filt_b1, hy_filt_w2, hy_filt_b2, hy_filt_w3, hy_filt_b3,
                                     hy_sin_freq, hy_bias_d, ml_conv_w, ml_conv_b, ml_norm_w, da_lambda, da_subln_w,
                                     moe_wg, moe_bg, moe_we, moe_be, moe_w1, moe_w3, moe_w2)))
    B, L, D = x.shape
    Lc = ctx.shape[1]
    depth = ada_w.shape[0]
    tm = ROW_TILE
    assert L % tm == 0 and Lc % tm == 0 and L % GRID_W == 0 and L % ML_CHUNK == 0 and Lc % ML_CHUNK == 0
    ML, MC = B * L, B * Lc
    M = ML + MC
    n_lat, n_ctx = ML // tm, MC // tm
    tpb = L // tm
    regions = ((0, n_lat, tpb), (n_lat, n_ctx, Lc // tm))

    xa = jnp.concatenate([x.reshape(ML, D), ctx.reshape(MC, D)], axis=0)
    R = -(-(B + 1) // SUBLANES) * SUBLANES
    cond = jnp.concatenate([c, c_ctx[None], jnp.zeros((R - B - 1, D), F32)], axis=0)
    cos, sin = _rope_tables(L)

    for l in range(depth):
        last = l == depth - 1
        p = {k: v[l] for k, v in stacked.items() if k not in ('moe_w1', 'moe_w3', 'moe_w2')}
        lam_init = 0.8 - 0.6 * math.exp(-0.3 * l)
        mod = _ada(cond, p['ada_w'], p['ada_b']).reshape(R, 6, D)
        u_hy, u_ml, u_da, g_t = _proj_in(xa, mod, p['norm1_w'], p['w_in'], p['b_in'], n_lat, tpb, B)

        y_hy = _hyena(*_hy_prep(u_hy, p['hy_conv_w'], p['hy_conv_b'], 0, n_lat, tpb), B, L, p)
        q_m, k_t = _ml_prep(u_ml, p['ml_conv_w'], p['ml_conv_b'], regions)
        h_ml = _mlstm(q_m, k_t, u_ml, g_t, B, L, Lc)
        qa, ka, va = _da_prep(u_da, cos, sin, n_lat, tpb)
        y_da = _diff_attn(qa, ka, va, p['da_lambda'], p['da_subln_w'], lam_init, B, 0, L, ((0, L), (ML, Lc)))
        n_rows = ML if last else M
        if not last:
            yc_hy = _hyena(*_hy_prep(u_hy, p['hy_conv_w'], p['hy_conv_b'], n_lat, n_ctx, Lc // tm), B, Lc, p)
            yc_da = _diff_attn(qa, ka, va, p['da_lambda'], p['da_subln_w'], lam_init, B, ML, Lc, ((ML, Lc),))
            y_hy = jnp.concatenate([y_hy, yc_hy], axis=1)
            y_da = jnp.concatenate([y_da, yc_da], axis=0)
        xa = _proj_out(xa, mod, y_hy, h_ml, u_ml, y_da, p['ml_norm_w'], p['w_out'], n_rows, n_lat, tpb, B)
        xa = _moe(xa, mod, p, (moe_w1, moe_w3, moe_w2), l, final_norm_w, last, n_rows, n_lat, tpb, B)
    return xa.reshape(B, L, D)
```

```python
import functools
import math

import numpy as np
import jax
import jax.numpy as jnp
from jax import lax
from jax.experimental import pallas as pl
from jax.experimental.pallas import tpu as pltpu

F32 = jnp.float32
BF16 = jnp.bfloat16

NORM_EPS = 1e-6
GRID_W = 64

HY_CH = 256
HY_HALVES = 2
HY_POS_EMB = 33
HY_FAST_DECAY = 0.3
HY_SLOW_DECAY = 1.5
HY_DECAY_TARGET = 1e-2

ML_HEADS = 4
ML_HD = 64
ML_W = ML_HEADS * ML_HD

DA_HEADS = 4
DA_HD = 64
DA_QK = DA_HEADS * 2 * DA_HD
DA_V = DA_HEADS * 2 * DA_HD
ROPE_THETA = 10000.0

HY_OFF = 0
ML_OFF = HY_OFF + 3 * HY_CH
GATE_OFF = ML_OFF + 4 * ML_W
DA_OFF = GATE_OFF + 2 * 2 * ML_HEADS

MOE_GROUPS = 4
MOE_PER_GROUP = 8
MOE_EXPERTS = MOE_GROUPS * MOE_PER_GROUP

LANES = 128
SUBLANES = 8
VMEM_BYTES_V7X = 64 * 1024 * 1024
VMEM_LIMIT = VMEM_BYTES_V7X * 7 // 8

ROW_TILE = 256
ML_CHUNK = 128
ATT_Q_TILE = 1024
ATT_SUB_TILE = 256
MOE_ROWS = 256


def _cparams(*sem):
    return pltpu.CompilerParams(dimension_semantics=tuple(sem), vmem_limit_bytes=VMEM_LIMIT)


def _dot(a, b):
    return jnp.dot(a.astype(BF16), b.astype(BF16), preferred_element_type=F32)


def _split(a):
    hi = a.astype(BF16)
    lo = (a - hi.astype(F32)).astype(BF16)
    return hi, lo


def _dot3(a, b):
    ah, al = _split(a)
    bh, bl = _split(b)
    d = functools.partial(jnp.dot, preferred_element_type=F32)
    return d(ah, bh) + (d(ah, bl) + d(al, bh))


def _dot2_exact_rhs(a, b_exact):
    ah, al = _split(a)
    d = functools.partial(jnp.dot, preferred_element_type=F32)
    return d(ah, b_exact) + d(al, b_exact)


def _ada_body(a_ref, w_ref, b_ref, o_ref):
    a = a_ref[...]
    a = a * jax.nn.sigmoid(a)
    o_ref[...] = _dot3(a, w_ref[...]) + b_ref[...]


def _ada(cond, w, b):
    R, D = cond.shape
    N = w.shape[1]
    tn = 1536
    return pl.pallas_call(
        _ada_body,
        grid=(N // tn,),
        in_specs=[pl.BlockSpec((R, D), lambda j: (0, 0)),
                  pl.BlockSpec((D, tn), lambda j: (0, j)),
                  pl.BlockSpec((1, tn), lambda j: (0, j))],
        out_specs=pl.BlockSpec((R, tn), lambda j: (0, j)),
        out_shape=jax.ShapeDtypeStruct((R, N), F32),
        compiler_params=_cparams("parallel"), name="ada_mod",
    )(cond, w, b[None])


def _mod_index(i, n_lat_tiles, tiles_per_batch, n_batch):
    return jnp.where(i < n_lat_tiles, i // tiles_per_batch, n_batch)


def _norm_mod(x, nw, shift, scale):
    ms = jnp.mean(x * x, axis=-1, keepdims=True)
    return (x * lax.rsqrt(ms + NORM_EPS) * nw) * (1.0 + scale) + shift


def _proj_in_body(x_ref, mod_ref, cos_ref, sin_ref, nw_ref, why_ref, wml_ref, wqk_ref, wvt_ref, wgt_ref,
                  bhy_ref, bml_ref, bqk_ref, bvt_ref, bgt_ref,
                  ohy_ref, oml_ref, oq_ref, ok_ref, ovt_ref, ogt_ref, *, n_lat_tiles):
    h = _norm_mod(x_ref[...], nw_ref[...], mod_ref[0, 0:1, :], mod_ref[0, 1:2, :])
    hb = h.astype(BF16)
    d = functools.partial(jnp.dot, preferred_element_type=F32)
    nt = functools.partial(lax.dot_general, dimension_numbers=(((1,), (1,)), ((), ())), preferred_element_type=F32)
    ohy_ref[...] = d(hb, why_ref[...]) + bhy_ref[...]
    oml_ref[...] = d(hb, wml_ref[...]) + bml_ref[...]
    q, k = _rope_qk(d(hb, wqk_ref[...]) + bqk_ref[...], cos_ref, sin_ref, pl.program_id(0) < n_lat_tiles)
    oq_ref[...] = q
    ok_ref[...] = k
    ovt_ref[...] = (nt(wvt_ref[...], hb) + bvt_ref[...]).astype(BF16)
    ogt_ref[...] = nt(wgt_ref[...], hb) + bgt_ref[...]


def _proj_in(xa, mod, nw, w_in, b_in, cos, sin, n_lat_tiles, tiles_per_batch, n_batch):
    M, D = xa.shape
    tm = ROW_TILE
    wb = w_in.astype(BF16)
    v_off = DA_OFF + 2 * DA_QK
    why, wml, wg = wb[:, :ML_OFF], wb[:, ML_OFF:GATE_OFF], wb[:, GATE_OFF:DA_OFF]
    wqk, wv = wb[:, DA_OFF:v_off], wb[:, v_off:]
    bhy, bml, bg = b_in[:ML_OFF], b_in[ML_OFF:GATE_OFF], b_in[GATE_OFF:DA_OFF]
    bqk, bv = b_in[DA_OFF:v_off], b_in[v_off:]
    ng = wg.shape[1]
    full = lambda a: pl.BlockSpec(a.shape, lambda i: (0,) * a.ndim)
    tab = pl.BlockSpec((tm, cos.shape[1]), lambda i: (jnp.where(i < n_lat_tiles, i % tiles_per_batch, 0), 0))
    consts = (nw[None], why, wml, wqk, wv.T, wg.T, bhy[None], bml[None], bqk[None], bv[:, None], bg[:, None])
    in_specs = [pl.BlockSpec((tm, D), lambda i: (i, 0)),
                pl.BlockSpec((1,) + mod.shape[1:],
                             lambda i: (_mod_index(i, n_lat_tiles, tiles_per_batch, n_batch), 0, 0)), tab, tab]
    in_specs += [full(a) for a in consts]
    rows = lambda wd: pl.BlockSpec((tm, wd), lambda i: (i, 0))
    cols = lambda ht: pl.BlockSpec((ht, tm), lambda i: (0, i))
    out_specs = [rows(why.shape[1]), rows(wml.shape[1]), rows(DA_QK), rows(DA_QK), cols(DA_V), cols(ng)]
    out_shape = [jax.ShapeDtypeStruct((M, why.shape[1]), F32), jax.ShapeDtypeStruct((M, wml.shape[1]), F32),
                 jax.ShapeDtypeStruct((M, DA_QK), BF16), jax.ShapeDtypeStruct((M, DA_QK), BF16),
                 jax.ShapeDtypeStruct((DA_V, M), BF16), jax.ShapeDtypeStruct((ng, M), F32)]
    return pl.pallas_call(
        functools.partial(_proj_in_body, n_lat_tiles=n_lat_tiles), grid=(M // tm,),
        in_specs=in_specs, out_specs=out_specs, out_shape=out_shape,
        compiler_params=_cparams("parallel"), name="proj_in",
    )(xa, mod, cos, sin, *consts)


def _conv3(x, prev_row, next_row, w, b, at_start, at_end):
    T = x.shape[0]
    rows = lax.broadcasted_iota(jnp.int32, x.shape, 0)
    prev_row = jnp.where(at_start, 0.0, prev_row)
    next_row = jnp.where(at_end, 0.0, next_row)
    up = jnp.where(rows == 0, prev_row, pltpu.roll(x, 1, 0))
    dn = jnp.where(rows == T - 1, next_row, pltpu.roll(x, T - 1, 0))
    return up * w[0:1, :] + x * w[1:2, :] + dn * w[2:3, :] + b


def _seq_edges(i, regions):
    at_start = jnp.bool_(False)
    at_end = jnp.bool_(False)
    pos = 0
    for (_, n, tps) in regions:
        inside = (i >= pos) & (i < pos + n)
        r = (i - pos) % tps
        at_start = at_start | (inside & (r == 0))
        at_end = at_end | (inside & (r == tps - 1))
        pos += n
    return at_start, at_end


def _seq_tile(i, regions):
    pos = 0
    t = jnp.int32(0)
    for (first, n, _) in regions:
        t = jnp.where((i >= pos) & (i < pos + n), first + (i - pos), t)
        pos += n
    return t


def _halo_specs(tm, width, colblk, regions, n_rows):
    per = tm // SUBLANES
    last8 = n_rows // SUBLANES - 1
    cur = pl.BlockSpec((tm, width), lambda i: (_seq_tile(i, regions), colblk))
    prv = pl.BlockSpec((SUBLANES, width), lambda i: (jnp.maximum(_seq_tile(i, regions) * per - 1, 0), colblk))
    nxt = pl.BlockSpec((SUBLANES, width),
                       lambda i: (jnp.minimum((_seq_tile(i, regions) + 1) * per, last8), colblk))
    return [cur, prv, nxt]


def _hy_prep_body(x_ref, p_ref, n_ref, w_ref, b_ref, v_ref, x1_ref, x2_ref, *, regions):
    at_start, at_end = _seq_edges(pl.program_id(0), regions)
    y = _conv3(x_ref[...], p_ref[SUBLANES - 1:SUBLANES, :], n_ref[0:1, :], w_ref[...], b_ref[...], at_start, at_end)
    for k, ref in enumerate((v_ref, x1_ref, x2_ref)):
        for h in range(HY_HALVES):
            ref[h] = y[:, k * HY_CH + h * LANES:k * HY_CH + (h + 1) * LANES]


def _hy_prep(u_hy, w, b, first_tile, n_tiles, tiles_per_seq):
    M, W = u_hy.shape
    tm = ROW_TILE
    regions = ((first_tile, n_tiles, tiles_per_seq),)
    out = jax.ShapeDtypeStruct((HY_HALVES, n_tiles * tm, LANES), F32)
    return pl.pallas_call(
        functools.partial(_hy_prep_body, regions=regions),
        grid=(n_tiles,),
        in_specs=_halo_specs(tm, W, 0, regions, M) + [pl.BlockSpec((3, W), lambda i: (0, 0)),
                                                      pl.BlockSpec((1, W), lambda i: (0, 0))],
        out_specs=[pl.BlockSpec((HY_HALVES, tm, LANES), lambda i: (0, i, 0))] * 3,
        out_shape=[out] * 3,
        compiler_params=_cparams("parallel"), name="hy_prep",
    )(u_hy, u_hy, u_hy, w, b[None])


def _filt_body(z_ref, w1_ref, b1_ref, w2_ref, b2_ref, fr_ref, w3_ref, b3_ref, env_ref, o_ref):
    f = fr_ref[...]
    h = jnp.sin(f * (_dot3(z_ref[...], w1_ref[...]) + b1_ref[...]))
    h = jnp.sin(f * (_dot3(h, w2_ref[...]) + b2_ref[...]))
    k = _dot3(h, w3_ref[...]) + b3_ref[...]
    e = env_ref[...]
    kf = k[:, :HY_CH] * e
    kb = k[:, HY_CH:] * e
    s = (jnp.sum(jnp.abs(kf), axis=0, keepdims=True) + jnp.sum(jnp.abs(kb), axis=0, keepdims=True)
         - jnp.abs(kb[0:1, :]))
    o_ref[:, :HY_CH] = kf / s
    o_ref[:, HY_CH:] = kb / s


def _hy_filters(L, p):
    t = jnp.linspace(0.0, 1.0, L, dtype=F32)[:, None]
    bands = (HY_POS_EMB - 1) // 2
    w = (2.0 * math.pi / L) * jnp.arange(L, dtype=F32)[:, None]
    f = jnp.linspace(1e-4, bands - 1, bands, dtype=F32)[None, :]
    z = jnp.concatenate([t, jnp.cos(f * w), -jnp.sin(f * w)], axis=-1)
    zp = jnp.pad(z, ((0, 0), (0, LANES - HY_POS_EMB)))
    w1 = jnp.pad(p['hy_filt_w1'], ((0, LANES - HY_POS_EMB), (0, 0)))
    deltas = jnp.abs(jnp.linspace(math.log(HY_DECAY_TARGET) / HY_SLOW_DECAY,
                                  math.log(HY_DECAY_TARGET) / HY_FAST_DECAY, HY_CH, dtype=F32))
    env = jnp.exp(-t * deltas)
    hid = w1.shape[1]
    nout = p['hy_filt_w3'].shape[1]
    n_order = nout // (2 * HY_CH)
    c0 = lambda a: pl.BlockSpec(a.shape, lambda o: (0,) * a.ndim)
    args = (zp, w1, p['hy_filt_b1'][None], p['hy_filt_w2'], p['hy_filt_b2'][None], p['hy_sin_freq'][None],
            p['hy_filt_w3'], p['hy_filt_b3'][None], env)
    in_specs = [c0(a) for a in args[:6]]
    in_specs += [pl.BlockSpec((hid, 2 * HY_CH), lambda o: (0, o)), pl.BlockSpec((1, 2 * HY_CH), lambda o: (0, o)),
                 c0(env)]
    return pl.pallas_call(
        _filt_body, grid=(n_order,), in_specs=in_specs,
        out_specs=pl.BlockSpec((L, 2 * HY_CH), lambda o: (0, o)),
        out_shape=jax.ShapeDtypeStruct((L, nout), F32),
        compiler_params=_cparams("parallel"), name="hy_filter",
    )(*args)


def _fft_plan(L):
    N = 2 * L
    Bn = 128 if N >= 4096 else 16
    A = N // Bn
    assert A * Bn == N and A % 16 == 0
    return A, Bn


def _dft_small(A, N):
    ka = np.arange(A)[:, None]
    a = np.arange(A)[None, :]
    th = 2.0 * np.pi * ((ka * a) % A) / A
    d1 = np.concatenate([np.cos(th), -np.sin(th)], axis=0)
    d4 = np.concatenate([np.cos(th.T), -np.sin(th.T)], axis=1) / N
    return jnp.asarray(d1, F32), jnp.asarray(d4, F32)


def _dft_mid(A, Bn):
    N = A * Bn
    ka = jnp.arange(A, dtype=jnp.int32)[:, None, None]
    kb = jnp.arange(Bn, dtype=jnp.int32)[None, :, None]
    b = jnp.arange(Bn, dtype=jnp.int32)[None, None, :]
    m = (b * (kb * A + ka)) % N
    ph = m.astype(F32) * (2.0 * math.pi / N)
    c, s = jnp.cos(ph), jnp.sin(ph)
    mf = jnp.concatenate([jnp.concatenate([c, s], axis=2), jnp.concatenate([-s, c], axis=2)], axis=1)
    ct, st = jnp.swapaxes(c, 1, 2), jnp.swapaxes(s, 1, 2)
    mi = jnp.concatenate([jnp.concatenate([ct, -st], axis=2), jnp.concatenate([st, ct], axis=2)], axis=1)
    return mf.astype(BF16), mi.astype(BF16)


FFT_BT = 16
FFT_GROUP = 8


def _fft1_body(d_ref, x_ref, o_ref, *, Bn):
    _, A2, bt, C = o_ref.shape
    Ain = d_ref.shape[1] // bt
    b0 = pl.multiple_of(pl.program_id(1) * bt, bt)
    slabs = [jnp.concatenate([x_ref[h, pl.ds(a * Bn + b0, bt), :] for h in range(HY_HALVES)], axis=1)
             for a in range(Ain)]
    xblk = jnp.concatenate(slabs, axis=0).astype(BF16)
    o_ref[0] = jnp.dot(d_ref[...], xblk, preferred_element_type=F32).reshape(A2, bt, C)


def _fft1(d1, x, n_seq, Bn):
    A2, Ain = d1.shape
    rows = Ain * Bn
    bt = min(FFT_BT, Bn)
    dbig = jnp.kron(d1, jnp.eye(bt, dtype=F32)).astype(BF16)
    return pl.pallas_call(
        functools.partial(_fft1_body, Bn=Bn), grid=(n_seq, Bn // bt),
        in_specs=[pl.BlockSpec(dbig.shape, lambda s, j: (0, 0)),
                  pl.BlockSpec((HY_HALVES, rows, LANES), lambda s, j: (0, s, 0))],
        out_specs=pl.BlockSpec((1, A2, bt, HY_CH), lambda s, j: (s, 0, j, 0)),
        out_shape=jax.ShapeDtypeStruct((n_seq, A2, Bn, HY_CH), F32),
        compiler_params=_cparams("parallel", "arbitrary"), name="hy_dft_slow",
    )(dbig, x)


def _fast_operand(p_ref, j):
    return jnp.concatenate([p_ref[0, 0, j], p_ref[0, 1, j]], axis=0).astype(BF16)


def _fft_spec_body(p_ref, mf_ref, k_ref, *, Bn):
    for j in range(FFT_GROUP):
        X = jnp.dot(mf_ref[j], _fast_operand(p_ref, j), preferred_element_type=F32)
        k_ref[0, j, 0] = X[:Bn]
        k_ref[0, j, 1] = X[Bn:]


def _fft_spec(p5, mf):
    S, _, A, Bn, C = p5.shape
    G = FFT_GROUP
    return pl.pallas_call(
        functools.partial(_fft_spec_body, Bn=Bn), grid=(A // G, S),
        in_specs=[pl.BlockSpec((1, 2, G, Bn, C), lambda g, s: (s, 0, g, 0, 0)),
                  pl.BlockSpec((G, 2 * Bn, 2 * Bn), lambda g, s: (g, 0, 0))],
        out_specs=pl.BlockSpec((1, G, 2, Bn, C), lambda g, s: (s, g, 0, 0, 0)),
        out_shape=jax.ShapeDtypeStruct((S, A, 2, Bn, C), F32),
        compiler_params=_cparams("parallel", "parallel"), name="hy_filter_spec",
    )(p5, mf)


def _fft_mid_body(p_ref, mf_ref, mi_ref, k_ref, q_ref, *, Bn):
    for j in range(FFT_GROUP):
        X = jnp.dot(mf_ref[j], _fast_operand(p_ref, j), preferred_element_type=F32)
        xr, xi = X[:Bn], X[Bn:]
        kr, ki = k_ref[j, 0], k_ref[j, 1]
        Y = jnp.concatenate([xr * kr - xi * ki, xr * ki + xi * kr], axis=0).astype(BF16)
        Q = jnp.dot(mi_ref[j], Y, preferred_element_type=F32)
        q_ref[0, j, 0] = Q[:Bn]
        q_ref[0, j, 1] = Q[Bn:]


def _fft_mid(p5, mf, mi, kspec):
    S, _, A, Bn, C = p5.shape
    G = FFT_GROUP
    mat = pl.BlockSpec((G, 2 * Bn, 2 * Bn), lambda g, s: (g, 0, 0))
    return pl.pallas_call(
        functools.partial(_fft_mid_body, Bn=Bn), grid=(A // G, S),
        in_specs=[pl.BlockSpec((1, 2, G, Bn, C), lambda g, s: (s, 0, g, 0, 0)), mat, mat,
                  pl.BlockSpec((G, 2, Bn, C), lambda g, s: (g, 0, 0, 0))],
        out_specs=pl.BlockSpec((1, G, 2, Bn, C), lambda g, s: (s, g, 0, 0, 0)),
        out_shape=jax.ShapeDtypeStruct((S, A, 2, Bn, C), F32),
        compiler_params=_cparams("parallel", "parallel"), name="hy_dft_fast",
    )(p5, mf, mi, kspec)


def _fft4_body(d_ref, q_ref, z_ref, g_ref, dch_ref, o_ref, *, Bn):
    _, A2, bt, C = q_ref.shape
    Ah = d_ref.shape[0] // bt
    b0 = pl.multiple_of(pl.program_id(1) * bt, bt)
    qblk = q_ref[0].reshape(A2 * bt, C).astype(BF16)
    y = jnp.dot(d_ref[...], qblk, preferred_element_type=F32)
    dch = dch_ref[...]
    for a in range(Ah):
        rows = pl.ds(a * Bn + b0, bt)
        ya = y[a * bt:(a + 1) * bt]
        for h in range(HY_HALVES):
            lanes = slice(h * LANES, (h + 1) * LANES)
            o_ref[h, rows, :] = g_ref[h, rows, :] * (ya[:, lanes] + z_ref[h, rows, :] * dch[:, lanes])


def _fft4(d4, q4, z, gate, dch):
    S, A2, Bn, C = q4.shape
    Ah = d4.shape[0]
    L = Ah * Bn
    bt = min(FFT_BT, Bn)
    dbig = jnp.kron(d4, jnp.eye(bt, dtype=F32)).astype(BF16)
    sig = pl.BlockSpec((HY_HALVES, L, LANES), lambda s, j: (0, s, 0))
    return pl.pallas_call(
        functools.partial(_fft4_body, Bn=Bn), grid=(S, Bn // bt),
        in_specs=[pl.BlockSpec(dbig.shape, lambda s, j: (0, 0)),
                  pl.BlockSpec((1, A2, bt, C), lambda s, j: (s, 0, j, 0)), sig, sig,
                  pl.BlockSpec((1, C), lambda s, j: (0, 0))],
        out_specs=sig,
        out_shape=jax.ShapeDtypeStruct((HY_HALVES, S * L, LANES), F32),
        compiler_params=_cparams("parallel", "arbitrary"), name="hy_idft_gate",
    )(dbig, q4, z, gate, dch)


def _hyena(v, x1, x2, n_seq, L, p):
    C = HY_CH
    A, Bn = _fft_plan(L)
    Ah = A // 2
    d1, d4 = _dft_small(A, A * Bn)
    mf, mi = _dft_mid(A, Bn)
    d4 = d4.reshape(A, 2, A).transpose(0, 2, 1).reshape(A, 2 * A)
    kn = _hy_filters(L, p).reshape(L, -1, 2, C)
    n_order = kn.shape[1]
    zero = jnp.zeros((1, C), F32)
    k2 = jnp.concatenate([piece for o in range(n_order)
                          for piece in (kn[:, o, 0], zero, jnp.flip(kn[1:, o, 1], axis=0))], axis=0)
    k2 = k2.reshape(-1, HY_HALVES, LANES).transpose(1, 0, 2)
    kspec = _fft_spec(_fft1(d1, k2, n_order, Bn).reshape(n_order, 2, A, Bn, C), mf)
    z = v
    for o, gate in enumerate((x1, x2)):
        P = _fft1(d1[:, :Ah], z, n_seq, Bn).reshape(n_seq, 2, A, Bn, C)
        Q = _fft_mid(P, mf, mi, kspec[o]).reshape(n_seq, 2 * A, Bn, C)
        z = _fft4(d4[:Ah], Q, z, gate, p['hy_bias_d'][o][None])
    return z


def _ml_prep_body(x_ref, p_ref, n_ref, w_ref, b_ref, q_ref, kt_ref, *, regions):
    at_start, at_end = _seq_edges(pl.program_id(0), regions)
    y = _conv3(x_ref[...], p_ref[SUBLANES - 1:SUBLANES, :], n_ref[0:1, :], w_ref[...], b_ref[...], at_start, at_end)
    y = y * jax.nn.sigmoid(y)
    q_ref[...] = y[:, :ML_W].astype(BF16)
    kt_ref[...] = (y[:, ML_W:] * (ML_HD ** -0.5)).T


def _ml_prep(u_ml, w, b, regions):
    M = u_ml.shape[0]
    tm = ROW_TILE
    n = sum(r[1] for r in regions)
    return pl.pallas_call(
        functools.partial(_ml_prep_body, regions=regions), grid=(n,),
        in_specs=_halo_specs(tm, 2 * ML_W, 0, regions, M) + [pl.BlockSpec((3, 2 * ML_W), lambda i: (0, 0)),
                                                             pl.BlockSpec((1, 2 * ML_W), lambda i: (0, 0))],
        out_specs=[pl.BlockSpec((tm, ML_W), lambda i: (_seq_tile(i, regions), 0)),
                   pl.BlockSpec((ML_W, tm), lambda i: (0, _seq_tile(i, regions)))],
        out_shape=[jax.ShapeDtypeStruct((M, ML_W), BF16), jax.ShapeDtypeStruct((ML_W, M), F32)],
        compiler_params=_cparams("parallel"), name="ml_prep",
    )(u_ml, u_ml, u_ml, w, b[None])


def _ml_body(*refs):
    ins, outs, scr = refs[:8], refs[8:10], refs[10:]

    @pl.when(pl.program_id(1) == 0)
    def _():
        for ref in scr:
            ref[...] = jnp.zeros_like(ref)

    for d in range(2):
        _ml_chunk(d, *ins[4 * d:4 * d + 4], outs[d], *scr[2 * d:2 * d + 2])


def _ml_chunk(d, q_ref, kt_ref, v_ref, g_ref, o_ref, st_ref, m_ref):
    T = q_ref.shape[0]
    H, W = ML_HEADS, ML_W
    WA = W + LANES
    sgn = 1 - 2 * d
    g = g_ref[...]
    gs = g[2 * H * d:2 * H * (d + 1)]
    ig = gs[0:H]
    lf8 = -(jnp.maximum(-gs, 0.0) + jnp.log1p(jnp.exp(-jnp.abs(gs))))
    lf = lf8[H:2 * H]
    r_i = lax.broadcasted_iota(jnp.int32, (T, T), 0)
    c_i = lax.broadcasted_iota(jnp.int32, (T, T), 1)
    prec = ((c_i - r_i) * sgn) <= 0
    incl = jnp.where(((r_i - c_i) * sgn) <= 0, 1.0, 0.0).astype(BF16)
    after = jnp.where(((c_i - r_i) * sgn) < 0, 1.0, 0.0).astype(BF16)
    b_rows = _dot2_exact_rhs(lf8, incl)[H:2 * H]
    bL = jnp.sum(lf, axis=1, keepdims=True)
    a_row = bL - b_rows + ig
    m_loc = jnp.max(a_row, axis=1, keepdims=True)
    w_row = jnp.exp(a_row - m_loc)
    m0 = m_ref[0:H, 0:1]
    m_new = jnp.maximum(bL + m0, m_loc)
    s_old = jnp.exp(bL + m0 - m_new)
    s_loc = jnp.exp(m_loc - m_new)

    qb = q_ref[...]
    kt = kt_ref[...]
    row_head = lax.broadcasted_iota(jnp.int32, (W, T), 0) // ML_HD
    lane = lax.broadcasted_iota(jnp.int32, (1, WA), 1)
    lane_head = jnp.where(lane < W, lane // ML_HD, lane - W)
    v_aug = jnp.concatenate([v_ref[...], jnp.ones((T, LANES), F32)], axis=1)

    Lfs = [jnp.where(prec, lf[h:h + 1], 0.0) for h in range(H)]
    E_all = _dot2_exact_rhs(jnp.concatenate(Lfs, axis=0), after)
    kth_all = jnp.concatenate([jnp.where(row_head == h, kt, 0.0) for h in range(H)], axis=1).astype(BF16)
    S_all = jnp.dot(qb, kth_all, preferred_element_type=F32)

    ps, vbd = [], []
    w_inter = jnp.zeros((T, WA), F32)
    e_m = jnp.zeros((T, WA), F32)
    for h in range(H):
        b_col = jnp.sum(Lfs[h], axis=1, keepdims=True)
        Dm = jnp.where(prec, E_all[h * T:(h + 1) * T] + ig[h:h + 1], -jnp.inf)
        inter = b_col + m0[h:h + 1]
        m_col = jnp.maximum(inter, jnp.max(Dm, axis=1, keepdims=True))
        S = S_all[:, h * T:(h + 1) * T]
        ps.append((jnp.exp(Dm - m_col) * S).astype(BF16))
        sel = lane_head == h
        vbd.append(jnp.where(sel, v_aug, 0.0).astype(BF16))
        w_inter = w_inter + jnp.where(sel, jnp.exp(inter - m_col), 0.0)
        e_m = e_m + jnp.where(sel, jnp.exp(-m_col), 0.0)
    nd = jnp.dot(jnp.concatenate(ps, axis=1), jnp.concatenate(vbd, axis=0), preferred_element_type=F32)
    nd = nd + w_inter * jnp.dot(qb, st_ref[...].astype(BF16), preferred_element_type=F32)
    den = jnp.zeros((T, W), F32)
    for h in range(H):
        den = den + jnp.where(lane_head[:, :W] == h, nd[:, W + h:W + h + 1], 0.0)
    o_ref[...] = nd[:, :W] / jnp.maximum(jnp.abs(den), e_m[:, :W])

    wk = jnp.zeros((W, T), F32)
    scol = jnp.zeros((1, WA), F32)
    ws = w_row * s_loc
    for h in range(H):
        wk = wk + jnp.where(row_head == h, ws[h:h + 1], 0.0)
        scol = scol + jnp.where(lane_head == h, s_old[h:h + 1], 0.0)
    st_loc = jnp.dot((kt * wk).astype(BF16), v_aug.astype(BF16), preferred_element_type=F32)
    diag = (lax.broadcasted_iota(jnp.int32, (W, WA), 0) // ML_HD) == lane_head
    st_ref[...] = jnp.where(diag, st_ref[...] * scol + st_loc, 0.0)
    m_ref[0:H, :] = jnp.broadcast_to(m_new, (H, LANES))


def _mlstm(q, kt, u_ml, g_t, B, L, Lc):
    M = q.shape[0]
    T = ML_CHUNK
    nC, nL = Lc // T, L // T
    lat0 = 0
    ctx0 = (B * L) // T

    def blk(d, b, i):
        cc = i if d == 0 else nC - 1 - i
        j = i - nC
        cl = j if d == 0 else nL - 1 - j
        return jnp.where(i < nC, ctx0 + b * nC + cc, lat0 + b * nL + cl)

    in_specs, out_specs = [], []
    for d in range(2):
        rows = functools.partial(lambda b, i, d, c: (blk(d, b, i), c), d=d)
        cols = functools.partial(lambda b, i, d: (0, blk(d, b, i)), d=d)
        in_specs += [pl.BlockSpec((T, ML_W), functools.partial(rows, c=0)), pl.BlockSpec((ML_W, T), cols),
                     pl.BlockSpec((T, ML_W), functools.partial(rows, c=2)), pl.BlockSpec((4 * ML_HEADS, T), cols)]
        out_specs.append(pl.BlockSpec((T, ML_W), functools.partial(rows, c=0)))
    state = [pltpu.VMEM((ML_W, ML_W + LANES), F32), pltpu.VMEM((SUBLANES, LANES), F32)]
    return pl.pallas_call(
        _ml_body, grid=(B, nC + nL), in_specs=in_specs, out_specs=out_specs,
        out_shape=[jax.ShapeDtypeStruct((M, ML_W), F32)] * 2,
        scratch_shapes=state * 2,
        compiler_params=_cparams("parallel", "arbitrary"), name="mlstm",
    )(*([q, kt, u_ml, g_t] * 2))


def _rope_tables(L):
    rows = L // GRID_W
    row = jnp.repeat(jnp.arange(rows, dtype=F32), GRID_W)
    col = jnp.tile(jnp.arange(GRID_W, dtype=F32), rows)
    half = DA_HD // 2
    inv = ROPE_THETA ** (-jnp.arange(0, half, 2, dtype=F32) / half)
    ang = jnp.stack([row, col], axis=-1)[:, :, None] * inv
    ang = jnp.stack([ang, ang], axis=-2).reshape(-1, DA_HD)
    ang = jnp.concatenate([ang, ang], axis=1)
    return jnp.cos(ang), jnp.sin(ang)


def _rope_qk(u, cos_ref, sin_ref, is_lat):
    reps = DA_QK // cos_ref.shape[1]
    cs = jnp.concatenate([cos_ref[...]] * reps, axis=1)
    sn = jnp.concatenate([sin_ref[...]] * reps, axis=1)
    q4 = DA_HD // 4
    src = lax.broadcasted_iota(jnp.int32, (LANES, LANES), 0)
    dst = lax.broadcasted_iota(jnp.int32, (LANES, LANES), 1)
    first = (dst % (2 * q4)) < q4
    perm = jnp.where(first & (src == dst + q4), -1.0, jnp.where((~first) & (src == dst - q4), 1.0, 0.0)).astype(BF16)

    def rope(x):
        rot = jnp.concatenate([_dot2_exact_rhs(x[:, g * LANES:(g + 1) * LANES], perm)
                               for g in range(x.shape[1] // LANES)], axis=1)
        return jnp.where(is_lat, x * cs + rot * sn, x)

    q = (rope(u[:, :DA_QK]) * (DA_HD ** -0.5 * math.log2(math.e))).astype(BF16)
    return q, rope(u[:, DA_QK:2 * DA_QK]).astype(BF16)


def _da_body(lam_ref, w_ref, q_ref, *rest, nseg, lam_init, sub):
    ks, vas, o_ref = rest[:nseg], rest[nseg:2 * nseg], rest[2 * nseg]
    lp = lam_ref[...]
    lam = (jnp.exp(jnp.sum(lp[0:1] * lp[1:2], axis=1, keepdims=True))
           - jnp.exp(jnp.sum(lp[2:3] * lp[3:4], axis=1, keepdims=True)) + lam_init)
    HW = 2 * DA_HD
    lane = lax.broadcasted_iota(jnp.int32, (1, HW), 1)
    n_sub = q_ref.shape[0] // sub
    def scores(i):
        t, m = divmod(i, 2)
        q = q_ref[t * sub:(t + 1) * sub, :]
        qm = jnp.where((lane // DA_HD) == m, q, jnp.zeros_like(q))
        return [lax.dot_general(k[...], qm, (((1,), (1,)), ((), ())), preferred_element_type=F32) for k in ks]

    def softmax(ss):
        mx = functools.reduce(jnp.maximum, [jnp.max(s, axis=0, keepdims=True) for s in ss])
        ps = [jnp.exp2(s - mx) for s in ss]
        den = functools.reduce(jnp.add, [jnp.sum(pr, axis=0, keepdims=True) for pr in ps])
        return [pr.astype(BF16) for pr in ps], den

    def values(ps, den):
        acc = functools.reduce(jnp.add, [jnp.dot(va[...], pr, preferred_element_type=F32)
                                         for pr, va in zip(ps, vas)])
        return acc / den

    n_streams = 2 * n_sub
    sss = {0: scores(0)}
    if n_streams > 1:
        sss[1] = scores(1)
    outs = []
    for i in range(n_streams):
        pd = softmax(sss.pop(i))
        if i + 2 < n_streams:
            sss[i + 2] = scores(i + 2)
        outs.append(values(*pd))
    for t in range(n_sub):
        o = outs[2 * t] - lam * outs[2 * t + 1]
        ms = jnp.mean(o * o, axis=0, keepdims=True)
        o_ref[t * sub:(t + 1) * sub, :] = ((o * lax.rsqrt(ms + NORM_EPS) * w_ref[...]) * (1.0 - lam_init)).T


def _diff_attn(qa, ka, vaa, lam_p, subln_w, lam_init, n_batch, q_rows0, q_len, segs):
    tq = min(ATT_Q_TILE, q_len)
    HW = 2 * DA_HD
    nq = q_len // tq
    q0 = q_rows0 // tq
    k_specs = [pl.BlockSpec((n, HW), functools.partial(lambda b, h, i, f, n: (f // n + b, h), f=f, n=n))
               for (f, n) in segs]
    va_specs = [pl.BlockSpec((HW, n), functools.partial(lambda b, h, i, f, n: (h, f // n + b), f=f, n=n))
                for (f, n) in segs]
    return pl.pallas_call(
        functools.partial(_da_body, nseg=len(segs), lam_init=lam_init, sub=min(ATT_SUB_TILE, tq)),
        grid=(n_batch, DA_HEADS, nq),
        in_specs=[pl.BlockSpec(lam_p.shape, lambda b, h, i: (0, 0)), pl.BlockSpec((HW, 1), lambda b, h, i: (0, 0)),
                  pl.BlockSpec((tq, HW), lambda b, h, i: (q0 + b * nq + i, h))] + k_specs + va_specs,
        out_specs=pl.BlockSpec((tq, HW), lambda b, h, i: (b * nq + i, h)),
        out_shape=jax.ShapeDtypeStruct((n_batch * q_len, DA_V), F32),
        compiler_params=_cparams("parallel", "parallel", "parallel"), name="diff_attn",
    )(lam_p, subln_w[:, None], qa, *([ka] * len(segs)), *([vaa] * len(segs)))


def _proj_out_body(x_ref, mod_ref, hy_ref, hf_ref, hb_ref, og_ref, da_ref, mw_ref, why_ref, wml_ref, wda_ref, o_ref):
    hs = hf_ref[...] + hb_ref[...]
    W = hs.shape[1]
    r = lax.broadcasted_iota(jnp.int32, (W, W), 0) // ML_HD
    c = lax.broadcasted_iota(jnp.int32, (W, W), 1) // ML_HD
    same_head = jnp.where(r == c, 1.0, 0.0).astype(BF16)
    ms = _dot2_exact_rhs(hs * hs, same_head) * (1.0 / ML_HD)
    y_ml = jax.nn.sigmoid(og_ref[...]) * (hs * lax.rsqrt(ms + NORM_EPS) * mw_ref[...])
    y_hy = jnp.concatenate([hy_ref[h] for h in range(HY_HALVES)], axis=1)
    y = (_dot(y_hy, why_ref[...]) + _dot(y_ml, wml_ref[...])) + _dot(da_ref[...], wda_ref[...])
    o_ref[...] = x_ref[...] + mod_ref[0, 2:3, :] * y


def _proj_out(xa, mod, y_hy, h_ml, u_ml, y_da, ml_norm_w, w_out, n_rows, n_lat_tiles, tiles_per_batch, n_batch):
    D = xa.shape[1]
    tm = ROW_TILE
    wb = w_out.astype(BF16)
    why, wml, wda = wb[:HY_CH], wb[HY_CH:HY_CH + ML_W], wb[HY_CH + ML_W:]
    full = lambda a: pl.BlockSpec(a.shape, lambda i: (0,) * a.ndim)
    row = lambda wd, cb=0: pl.BlockSpec((tm, wd), lambda i: (i, cb))
    mw = ml_norm_w[None]
    return pl.pallas_call(
        _proj_out_body, grid=(n_rows // tm,),
        in_specs=[row(D), pl.BlockSpec((1,) + mod.shape[1:],
                                       lambda i: (_mod_index(i, n_lat_tiles, tiles_per_batch, n_batch), 0, 0)),
                  pl.BlockSpec((HY_HALVES, tm, LANES), lambda i: (0, i, 0)),
                  row(ML_W), row(ML_W), row(ML_W, 3), row(DA_V),
                  full(mw), full(why), full(wml), full(wda)],
        out_specs=row(D),
        out_shape=jax.ShapeDtypeStruct((n_rows, D), F32),
        compiler_params=_cparams("parallel"), name="proj_out",
    )(xa, mod, y_hy, *h_ml, u_ml, y_da, mw, why, wml, wda)


ROUTE_LANE0 = MOE_GROUPS


def _router_body(x_ref, mod_ref, nw_ref, wr_ref, br_ref, h_ref, ri_ref, rf_ref, cnt_ref, run_ref):
    @pl.when(pl.program_id(0) == 0)
    def _():
        run_ref[...] = jnp.zeros_like(run_ref)

    h = _norm_mod(x_ref[...], nw_ref[...], mod_ref[0, 3:4, :], mod_ref[0, 4:5, :])
    h_ref[...] = h
    lg = _dot3(h, wr_ref[...]) + br_ref[...]
    tm = lg.shape[0]
    lane = lax.broadcasted_iota(jnp.int32, lg.shape, 1)
    neg = -jnp.inf
    is_g = lane < MOE_GROUPS
    gl = jnp.where(is_g, lg, neg)
    gmax = jnp.max(gl, axis=1, keepdims=True)
    gidx = jnp.min(jnp.where(gl == gmax, lane, LANES), axis=1, keepdims=True)
    gw = 1.0 / jnp.sum(jnp.where(is_g, jnp.exp(gl - gmax), 0.0), axis=1, keepdims=True)
    e_of = lane - ROUTE_LANE0
    in_grp = (e_of >= 0) & (e_of < MOE_EXPERTS) & ((e_of // MOE_PER_GROUP) == gidx)
    el = jnp.where(in_grp, lg, neg)
    t1 = jnp.max(el, axis=1, keepdims=True)
    i1 = jnp.min(jnp.where(el == t1, lane, LANES), axis=1, keepdims=True)
    el2 = jnp.where(lane == i1, neg, el)
    t2 = jnp.max(el2, axis=1, keepdims=True)
    i2 = jnp.min(jnp.where(el2 == t2, lane, LANES), axis=1, keepdims=True)
    ex = jnp.exp(t2 - t1)
    g1 = gw / (1.0 + ex)
    g2 = gw * ex / (1.0 + ex)
    oh = jnp.where((lane == i1) | (lane == i2), 1.0, 0.0)
    r_i = lax.broadcasted_iota(jnp.int32, (tm, tm), 0)
    c_i = lax.broadcasted_iota(jnp.int32, (tm, tm), 1)
    earlier = jnp.where(c_i < r_i, 1.0, 0.0).astype(BF16)
    cum = jnp.dot(earlier, oh.astype(BF16), preferred_element_type=F32) + run_ref[0:1, :]
    r1 = jnp.sum(jnp.where(lane == i1, cum, 0.0), axis=1, keepdims=True).astype(jnp.int32)
    r2 = jnp.sum(jnp.where(lane == i2, cum, 0.0), axis=1, keepdims=True).astype(jnp.int32)
    run = run_ref[0:1, :] + jnp.sum(oh, axis=0, keepdims=True)
    run_ref[...] = jnp.broadcast_to(run, run_ref.shape)
    cnt_ref[...] = jnp.broadcast_to(run, cnt_ref.shape)
    zi = jnp.zeros_like(lane)
    ri_ref[...] = jnp.where(lane == 0, i1 - ROUTE_LANE0, jnp.where(lane == 1, i2 - ROUTE_LANE0,
                            jnp.where(lane == 2, r1, jnp.where(lane == 3, r2, zi))))
    rf_ref[...] = jnp.where(lane == 0, g1, jnp.where(lane == 1, g2, 0.0))


def _router(xa, mod, nw, wg, bg, we, be, n_rows, n_lat_tiles, tiles_per_batch, n_batch):
    D = xa.shape[1]
    tm = ROW_TILE
    pad = LANES - MOE_GROUPS - MOE_EXPERTS
    wr = jnp.concatenate([wg, we, jnp.zeros((D, pad), F32)], axis=1)
    br = jnp.concatenate([bg, be, jnp.zeros((pad,), F32)])[None]
    row = lambda wd: pl.BlockSpec((tm, wd), lambda i: (i, 0))
    full = lambda a: pl.BlockSpec(a.shape, lambda i: (0,) * a.ndim)
    return pl.pallas_call(
        _router_body, grid=(n_rows // tm,),
        in_specs=[row(D), pl.BlockSpec((1,) + mod.shape[1:],
                                       lambda i: (_mod_index(i, n_lat_tiles, tiles_per_batch, n_batch), 0, 0)),
                  full(nw[None]), full(wr), full(br)],
        out_specs=[row(D), row(LANES), row(LANES), pl.BlockSpec((SUBLANES, LANES), lambda i: (0, 0))],
        out_shape=[jax.ShapeDtypeStruct((n_rows, D), F32), jax.ShapeDtypeStruct((n_rows, LANES), jnp.int32),
                   jax.ShapeDtypeStruct((n_rows, LANES), F32), jax.ShapeDtypeStruct((SUBLANES, LANES), F32)],
        scratch_shapes=[pltpu.VMEM((SUBLANES, LANES), F32)],
        compiler_params=_cparams("arbitrary"), name="moe_router",
    )(xa, mod, nw[None], wr, br)


def _plan_body(ri_ref, base_ref, d_ref):
    ri = ri_ref[...]
    lane = lax.broadcasted_iota(jnp.int32, ri.shape, 1)
    base = base_ref[...]
    d = []
    for j in range(2):
        b = jnp.sum(jnp.where(lane == ri[:, j:j + 1], base, 0.0), axis=1, keepdims=True)
        d.append(b + ri[:, 2 + j:3 + j].astype(F32))
    slots = jnp.where(lane == 0, d[0], jnp.where(lane == 1, d[1], 0.0))
    d_ref[0] = slots.T[0:SUBLANES, :].astype(jnp.int32)


def _plan(ri, base):
    n_rows = ri.shape[0]
    tm = ROW_TILE
    basef = jnp.pad(base.astype(F32), (0, LANES - base.shape[0]))[None]
    out = pl.pallas_call(
        _plan_body, grid=(n_rows // tm,),
        in_specs=[pl.BlockSpec((tm, LANES), lambda i: (i, 0)), pl.BlockSpec((1, LANES), lambda i: (0, 0))],
        out_specs=pl.BlockSpec((1, SUBLANES, tm), lambda i: (i, 0, 0)),
        out_shape=jax.ShapeDtypeStruct((n_rows // tm, SUBLANES, tm), jnp.int32),
        compiler_params=_cparams("parallel"), name="moe_plan",
    )(ri, basef)
    return out[:, 0, :], out[:, 1, :]


def _dispatch_body(d0_ref, d1_ref, lb_ref, nb_ref, h_ref, xb_ref, zbuf, sem, zsem):
    i = pl.program_id(0)
    tm = h_ref.shape[0]
    TB = zbuf.shape[0]
    n_blk = xb_ref.shape[0] // TB

    def zero_copy(blk):
        return pltpu.make_async_copy(zbuf, xb_ref.at[pl.ds(pl.multiple_of(blk * TB, TB), TB)], zsem)

    @pl.when(i == 0)
    def _():
        zbuf[...] = jnp.zeros_like(zbuf)
        for phase in ("start", "wait"):
            def tail(blk, carry, phase=phase):
                getattr(zero_copy(blk), phase)()
                return carry

            for e in range(lb_ref.shape[0]):
                @pl.when(lb_ref[e] >= 0)
                def _(e=e, phase=phase):
                    getattr(zero_copy(lb_ref[e]), phase)()
            lax.fori_loop(nb_ref[0], n_blk, tail, 0)

    for r in range(tm):
        for d_ref in (d0_ref, d1_ref):
            pltpu.make_async_copy(h_ref.at[pl.ds(r, 1)], xb_ref.at[pl.ds(d_ref[i, r], 1)], sem).start()
    pltpu.make_async_copy(xb_ref.at[pl.ds(0, 2 * tm)], xb_ref.at[pl.ds(0, 2 * tm)], sem).wait()


def _dispatch(d0, d1, last_blk, n_used, h2, n_slot_rows):
    n_rows, D = h2.shape
    tm = ROW_TILE
    return pl.pallas_call(
        _dispatch_body,
        grid_spec=pltpu.PrefetchScalarGridSpec(
            num_scalar_prefetch=4, grid=(n_rows // tm,),
            in_specs=[pl.BlockSpec((tm, D), lambda i, *_: (i, 0))],
            out_specs=pl.BlockSpec(memory_space=pl.ANY),
            scratch_shapes=[pltpu.VMEM((MOE_ROWS, D), F32), pltpu.SemaphoreType.DMA(()),
                            pltpu.SemaphoreType.DMA(())]),
        out_shape=jax.ShapeDtypeStruct((n_slot_rows, D), F32),
        compiler_params=pltpu.CompilerParams(dimension_semantics=("arbitrary",), vmem_limit_bytes=VMEM_LIMIT,
                                             disable_bounds_checks=True),
        name="moe_dispatch",
    )(d0, d1, last_blk, n_used, h2)


def _ffn_body(be_ref, nb_ref, x_ref, w1_ref, w3_ref, w2_ref, y_ref, w1b, w3b, w2b):
    i = pl.program_id(0)
    used = i < nb_ref[0]
    new_expert = (i == 0) | (be_ref[i] != be_ref[jnp.maximum(i - 1, 0)])

    @pl.when(used & new_expert)
    def _():
        w1b[...] = w1_ref[0, 0].astype(BF16)
        w3b[...] = w3_ref[0, 0].astype(BF16)
        w2b[...] = w2_ref[0, 0].astype(BF16)

    @pl.when(used)
    def _():
        xb = x_ref[...].astype(BF16)
        a = jnp.dot(xb, w1b[...], preferred_element_type=F32)
        b = jnp.dot(xb, w3b[...], preferred_element_type=F32)
        hmid = ((a * jax.nn.sigmoid(a)) * b).astype(BF16)
        y_ref[...] = jnp.dot(hmid, w2b[...], preferred_element_type=F32)

    @pl.when(jnp.logical_not(used))
    def _():
        y_ref[...] = jnp.zeros_like(y_ref)


def _expert_ffn(blk_e, n_used, xb, w1, w3, w2, layer):
    P, D = xb.shape
    F = w1.shape[3]
    TB = MOE_ROWS
    rows = pl.BlockSpec((TB, D), lambda i, be, nb: (i, 0))
    return pl.pallas_call(
        _ffn_body,
        grid_spec=pltpu.PrefetchScalarGridSpec(
            num_scalar_prefetch=2, grid=(P // TB,),
            in_specs=[rows, pl.BlockSpec((1, 1, D, F), lambda i, be, nb: (layer, be[i], 0, 0)),
                      pl.BlockSpec((1, 1, D, F), lambda i, be, nb: (layer, be[i], 0, 0)),
                      pl.BlockSpec((1, 1, F, D), lambda i, be, nb: (layer, be[i], 0, 0))],
            out_specs=rows,
            scratch_shapes=[pltpu.VMEM((D, F), BF16), pltpu.VMEM((D, F), BF16), pltpu.VMEM((F, D), BF16)]),
        out_shape=jax.ShapeDtypeStruct((P, D), F32),
        compiler_params=_cparams("arbitrary"), name="moe_ffn",
    )(blk_e, n_used, xb, w1, w3, w2)


def _combine_body(d0_ref, d1_ref, x_ref, mod_ref, rf_ref, yb_ref, fw_ref, o_ref, buf, sem, *, final):
    i = pl.program_id(0)
    n = pl.num_programs(0)
    tm = x_ref.shape[0]

    def gather(step, slot):
        for r in range(tm):
            for j, d_ref in enumerate((d0_ref, d1_ref)):
                pltpu.make_async_copy(yb_ref.at[pl.ds(d_ref[step, r], 1)], buf.at[slot, j, pl.ds(r, 1)],
                                      sem.at[slot]).start()

    @pl.when(i == 0)
    def _():
        gather(0, 0)

    slot = i % 2
    pltpu.make_async_copy(buf.at[slot], buf.at[slot], sem.at[slot]).wait()

    @pl.when(i + 1 < n)
    def _():
        gather(i + 1, 1 - slot)

    g = rf_ref[...]
    f = g[:, 0:1] * buf[slot, 0] + g[:, 1:2] * buf[slot, 1]
    xn = x_ref[...] + mod_ref[0, 5:6, :] * f
    if final:
        ms = jnp.mean(xn * xn, axis=-1, keepdims=True)
        xn = xn * lax.rsqrt(ms + NORM_EPS) * fw_ref[...]
    o_ref[...] = xn


def _combine(d0, d1, xa, mod, rf, yb, final_w, final, n_rows, n_lat_tiles, tiles_per_batch, n_batch):
    D = xa.shape[1]
    tm = ROW_TILE
    row = lambda wd: pl.BlockSpec((tm, wd), lambda i, *_: (i, 0))
    return pl.pallas_call(
        functools.partial(_combine_body, final=final),
        grid_spec=pltpu.PrefetchScalarGridSpec(
            num_scalar_prefetch=2, grid=(n_rows // tm,),
            in_specs=[row(D), pl.BlockSpec((1,) + mod.shape[1:],
                                           lambda i, *_: (_mod_index(i, n_lat_tiles, tiles_per_batch, n_batch), 0, 0)),
                      row(LANES), pl.BlockSpec(memory_space=pl.ANY), pl.BlockSpec((1, D), lambda i, *_: (0, 0))],
            out_specs=row(D),
            scratch_shapes=[pltpu.VMEM((2, 2, tm, D), F32), pltpu.SemaphoreType.DMA((2,))]),
        out_shape=jax.ShapeDtypeStruct((n_rows, D), F32),
        compiler_params=pltpu.CompilerParams(dimension_semantics=("arbitrary",), vmem_limit_bytes=VMEM_LIMIT,
                                             disable_bounds_checks=True),
        name="moe_combine",
    )(d0, d1, xa, mod, rf, yb, final_w[None])


def _moe(xa, mod, p, expert_w, layer, final_w, final, n_rows, n_lat_tiles, tiles_per_batch, n_batch):
    TB = MOE_ROWS
    h2, ri, rf, cnt = _router(xa, mod, p['norm2_w'], p['moe_wg'], p['moe_bg'], p['moe_we'], p['moe_be'],
                              n_rows, n_lat_tiles, tiles_per_batch, n_batch)
    counts = cnt[0, ROUTE_LANE0:ROUTE_LANE0 + MOE_EXPERTS].astype(jnp.int32)
    pc = (counts + TB - 1) // TB * TB
    pend = jnp.cumsum(pc)
    base = pend - pc
    n_blk = -(-2 * n_rows // TB) + MOE_EXPERTS
    n_used = (pend[-1] // TB).astype(jnp.int32)
    blk = jnp.arange(n_blk, dtype=jnp.int32)
    blk_e = jnp.sum((pend[None, :] <= (jnp.minimum(blk, n_used - 1) * TB)[:, None]).astype(jnp.int32), axis=1)
    blk_e = jnp.minimum(blk_e, MOE_EXPERTS - 1).astype(jnp.int32)
    last_blk = jnp.where(pc > 0, pend // TB - 1, -1).astype(jnp.int32)
    d0, d1 = _plan(ri, base)
    xb = _dispatch(d0, d1, last_blk, n_used[None], h2, n_blk * TB)
    yb = _expert_ffn(blk_e, n_used[None], xb, *expert_w, layer)
    return _combine(d0, d1, xa, mod, rf, yb, final_w, final, n_rows, n_lat_tiles, tiles_per_batch, n_batch)


_LAYER_KEYS = ('ada_w', 'ada_b', 'norm1_w', 'norm2_w', 'w_in', 'b_in', 'w_out', 'hy_conv_w', 'hy_conv_b',
               'hy_filt_w1', 'hy_filt_b1', 'hy_filt_w2', 'hy_filt_b2', 'hy_filt_w3', 'hy_filt_b3', 'hy_sin_freq',
               'hy_bias_d', 'ml_conv_w', 'ml_conv_b', 'ml_norm_w', 'da_lambda', 'da_subln_w', 'moe_wg', 'moe_bg',
               'moe_we', 'moe_be', 'moe_w1', 'moe_w3', 'moe_w2')


def kernel(x, c, ctx, c_ctx, ada_w, ada_b, norm1_w, norm2_w, w_in, b_in, w_out, hy_conv_w, hy_conv_b, hy_filt_w1,
           hy_filt_b1, hy_filt_w2, hy_filt_b2, hy_filt_w3, hy_filt_b3, hy_sin_freq, hy_bias_d, ml_conv_w, ml_conv_b,
           ml_norm_w, da_lambda, da_subln_w, moe_wg, moe_bg, moe_we, moe_be, moe_w1, moe_w3, moe_w2, final_norm_w):
    stacked = dict(zip(_LAYER_KEYS, (ada_w, ada_b, norm1_w, norm2_w, w_in, b_in, w_out, hy_conv_w, hy_conv_b,
                                     hy_filt_w1, hy_filt_b1, hy_filt_w2, hy_filt_b2, hy_filt_w3, hy_filt_b3,
                                     hy_sin_freq, hy_bias_d, ml_conv_w, ml_conv_b, ml_norm_w, da_lambda, da_subln_w,
                                     moe_wg, moe_bg, moe_we, moe_be, moe_w1, moe_w3, moe_w2)))
    B, L, D = x.shape
    Lc = ctx.shape[1]
    depth = ada_w.shape[0]
    tm = ROW_TILE
    assert L % tm == 0 and Lc % tm == 0 and L % GRID_W == 0 and L % ML_CHUNK == 0 and Lc % ML_CHUNK == 0
    ML, MC = B * L, B * Lc
    M = ML + MC
    n_lat, n_ctx = ML // tm, MC // tm
    tpb = L // tm
    regions = ((0, n_lat, tpb), (n_lat, n_ctx, Lc // tm))

    xa = jnp.concatenate([x.reshape(ML, D), ctx.reshape(MC, D)], axis=0)
    R = -(-(B + 1) // SUBLANES) * SUBLANES
    cond = jnp.concatenate([c, c_ctx[None], jnp.zeros((R - B - 1, D), F32)], axis=0)
    cos, sin = _rope_tables(L)

    for l in range(depth):
        last = l == depth - 1
        p = {k: v[l] for k, v in stacked.items() if k not in ('moe_w1', 'moe_w3', 'moe_w2')}
        lam_init = 0.8 - 0.6 * math.exp(-0.3 * l)
        mod = _ada(cond, p['ada_w'], p['ada_b']).reshape(R, 6, D)
        u_hy, u_ml, qa, ka, va, g_t = _proj_in(xa, mod, p['norm1_w'], p['w_in'], p['b_in'], cos, sin, n_lat, tpb, B)

        y_hy = _hyena(*_hy_prep(u_hy, p['hy_conv_w'], p['hy_conv_b'], 0, n_lat, tpb), B, L, p)
        q_m, k_t = _ml_prep(u_ml, p['ml_conv_w'], p['ml_conv_b'], regions)
        h_ml = _mlstm(q_m, k_t, u_ml, g_t, B, L, Lc)
        y_da = _diff_attn(qa, ka, va, p['da_lambda'], p['da_subln_w'], lam_init, B, 0, L, ((0, L), (ML, Lc)))
        n_rows = ML if last else M
        if not last:
            yc_hy = _hyena(*_hy_prep(u_hy, p['hy_conv_w'], p['hy_conv_b'], n_lat, n_ctx, Lc // tm), B, Lc, p)
            yc_da = _diff_attn(qa, ka, va, p['da_lambda'], p['da_subln_w'], lam_init, B, ML, Lc, ((ML, Lc),))
            y_hy = jnp.concatenate([y_hy, yc_hy], axis=1)
            y_da = jnp.concatenate([y_da, yc_da], axis=0)
        xa = _proj_out(xa, mod, y_hy, h_ml, u_ml, y_da, p['ml_norm_w'], p['w_out'], n_rows, n_lat, tpb, B)
        xa = _moe(xa, mod, p, (moe_w1, moe_w3, moe_w2), l, final_norm_w, last, n_rows, n_lat, tpb, B)
    return xa.reshape(B, L, D)
```

```python
import functools
import math

import numpy as np
import jax
import jax.numpy as jnp
from jax import lax
from jax.experimental import pallas as pl
from jax.experimental.pallas import tpu as pltpu

F32 = jnp.float32
BF16 = jnp.bfloat16

NORM_EPS = 1e-6
GRID_W = 64

HY_CH = 256
HY_HALVES = 2
HY_POS_EMB = 33
HY_FAST_DECAY = 0.3
HY_SLOW_DECAY = 1.5
HY_DECAY_TARGET = 1e-2

ML_HEADS = 4
ML_HD = 64
ML_W = ML_HEADS * ML_HD

DA_HEADS = 4
DA_HD = 64
DA_QK = DA_HEADS * 2 * DA_HD
DA_V = DA_HEADS * 2 * DA_HD
ROPE_THETA = 10000.0

HY_OFF = 0
ML_OFF = HY_OFF + 3 * HY_CH
GATE_OFF = ML_OFF + 4 * ML_W
DA_OFF = GATE_OFF + 2 * 2 * ML_HEADS

MOE_GROUPS = 4
MOE_PER_GROUP = 8
MOE_EXPERTS = MOE_GROUPS * MOE_PER_GROUP

LANES = 128
SUBLANES = 8
VMEM_BYTES_V7X = 64 * 1024 * 1024
VMEM_LIMIT = VMEM_BYTES_V7X * 7 // 8

ROW_TILE = 256
ML_CHUNK = 128
ATT_Q_TILE = 1024
ATT_SUB_TILE = 256
MOE_ROWS = 512


def _cparams(*sem):
    return pltpu.CompilerParams(dimension_semantics=tuple(sem), vmem_limit_bytes=VMEM_LIMIT)


def _dot(a, b):
    return jnp.dot(a.astype(BF16), b.astype(BF16), preferred_element_type=F32)


def _split(a):
    hi = a.astype(BF16)
    lo = (a - hi.astype(F32)).astype(BF16)
    return hi, lo


def _dot3(a, b):
    ah, al = _split(a)
    bh, bl = _split(b)
    d = functools.partial(jnp.dot, preferred_element_type=F32)
    return d(ah, bh) + (d(ah, bl) + d(al, bh))


def _dot2_exact_rhs(a, b_exact):
    ah, al = _split(a)
    d = functools.partial(jnp.dot, preferred_element_type=F32)
    return d(ah, b_exact) + d(al, b_exact)


def _ada_body(a_ref, w_ref, b_ref, o_ref):
    a = a_ref[...]
    a = a * jax.nn.sigmoid(a)
    o_ref[...] = _dot3(a, w_ref[...]) + b_ref[...]


def _ada(cond, w, b):
    R, D = cond.shape
    N = w.shape[1]
    tn = 1536
    return pl.pallas_call(
        _ada_body,
        grid=(N // tn,),
        in_specs=[pl.BlockSpec((R, D), lambda j: (0, 0)),
                  pl.BlockSpec((D, tn), lambda j: (0, j)),
                  pl.BlockSpec((1, tn), lambda j: (0, j))],
        out_specs=pl.BlockSpec((R, tn), lambda j: (0, j)),
        out_shape=jax.ShapeDtypeStruct((R, N), F32),
        compiler_params=_cparams("parallel"), name="ada_mod",
    )(cond, w, b[None])


def _mod_index(i, n_lat_tiles, tiles_per_batch, n_batch):
    return jnp.where(i < n_lat_tiles, i // tiles_per_batch, n_batch)


def _norm_mod(x, nw, shift, scale):
    ms = jnp.mean(x * x, axis=-1, keepdims=True)
    return (x * lax.rsqrt(ms + NORM_EPS) * nw) * (1.0 + scale) + shift


def _proj_in_body(x_ref, mod_ref, cos_ref, sin_ref, nw_ref, why_ref, wml_ref, wqk_ref, wvt_ref, wgt_ref,
                  bhy_ref, bml_ref, bqk_ref, bvt_ref, bgt_ref,
                  ohy_ref, oml_ref, oq_ref, ok_ref, ovt_ref, ogt_ref, *, n_lat_tiles):
    h = _norm_mod(x_ref[...], nw_ref[...], mod_ref[0, 0:1, :], mod_ref[0, 1:2, :])
    hb = h.astype(BF16)
    d = functools.partial(jnp.dot, preferred_element_type=F32)
    nt = functools.partial(lax.dot_general, dimension_numbers=(((1,), (1,)), ((), ())), preferred_element_type=F32)
    ohy_ref[...] = d(hb, why_ref[...]) + bhy_ref[...]
    oml_ref[...] = d(hb, wml_ref[...]) + bml_ref[...]
    q, k = _rope_qk(d(hb, wqk_ref[...]) + bqk_ref[...], cos_ref, sin_ref, pl.program_id(0) < n_lat_tiles)
    oq_ref[...] = q
    ok_ref[...] = k
    ovt_ref[...] = (nt(wvt_ref[...], hb) + bvt_ref[...]).astype(BF16)
    ogt_ref[...] = nt(wgt_ref[...], hb) + bgt_ref[...]


def _proj_in(xa, mod, nw, w_in, b_in, cos, sin, n_lat_tiles, tiles_per_batch, n_batch):
    M, D = xa.shape
    tm = ROW_TILE
    wb = w_in.astype(BF16)
    v_off = DA_OFF + 2 * DA_QK
    why, wml, wg = wb[:, :ML_OFF], wb[:, ML_OFF:GATE_OFF], wb[:, GATE_OFF:DA_OFF]
    wqk, wv = wb[:, DA_OFF:v_off], wb[:, v_off:]
    bhy, bml, bg = b_in[:ML_OFF], b_in[ML_OFF:GATE_OFF], b_in[GATE_OFF:DA_OFF]
    bqk, bv = b_in[DA_OFF:v_off], b_in[v_off:]
    ng = wg.shape[1]
    full = lambda a: pl.BlockSpec(a.shape, lambda i: (0,) * a.ndim)
    tab = pl.BlockSpec((tm, cos.shape[1]), lambda i: (jnp.where(i < n_lat_tiles, i % tiles_per_batch, 0), 0))
    consts = (nw[None], why, wml, wqk, wv.T, wg.T, bhy[None], bml[None], bqk[None], bv[:, None], bg[:, None])
    in_specs = [pl.BlockSpec((tm, D), lambda i: (i, 0)),
                pl.BlockSpec((1,) + mod.shape[1:],
                             lambda i: (_mod_index(i, n_lat_tiles, tiles_per_batch, n_batch), 0, 0)), tab, tab]
    in_specs += [full(a) for a in consts]
    rows = lambda wd: pl.BlockSpec((tm, wd), lambda i: (i, 0))
    cols = lambda ht: pl.BlockSpec((ht, tm), lambda i: (0, i))
    out_specs = [rows(why.shape[1]), rows(wml.shape[1]), rows(DA_QK), rows(DA_QK), cols(DA_V), cols(ng)]
    out_shape = [jax.ShapeDtypeStruct((M, why.shape[1]), F32), jax.ShapeDtypeStruct((M, wml.shape[1]), F32),
                 jax.ShapeDtypeStruct((M, DA_QK), BF16), jax.ShapeDtypeStruct((M, DA_QK), BF16),
                 jax.ShapeDtypeStruct((DA_V, M), BF16), jax.ShapeDtypeStruct((ng, M), F32)]
    return pl.pallas_call(
        functools.partial(_proj_in_body, n_lat_tiles=n_lat_tiles), grid=(M // tm,),
        in_specs=in_specs, out_specs=out_specs, out_shape=out_shape,
        compiler_params=_cparams("parallel"), name="proj_in",
    )(xa, mod, cos, sin, *consts)


def _conv3(x, prev_row, next_row, w, b, at_start, at_end):
    T = x.shape[0]
    rows = lax.broadcasted_iota(jnp.int32, x.shape, 0)
    prev_row = jnp.where(at_start, 0.0, prev_row)
    next_row = jnp.where(at_end, 0.0, next_row)
    up = jnp.where(rows == 0, prev_row, pltpu.roll(x, 1, 0))
    dn = jnp.where(rows == T - 1, next_row, pltpu.roll(x, T - 1, 0))
    return up * w[0:1, :] + x * w[1:2, :] + dn * w[2:3, :] + b


def _seq_edges(i, regions):
    at_start = jnp.bool_(False)
    at_end = jnp.bool_(False)
    pos = 0
    for (_, n, tps) in regions:
        inside = (i >= pos) & (i < pos + n)
        r = (i - pos) % tps
        at_start = at_start | (inside & (r == 0))
        at_end = at_end | (inside & (r == tps - 1))
        pos += n
    return at_start, at_end


def _seq_tile(i, regions):
    pos = 0
    t = jnp.int32(0)
    for (first, n, _) in regions:
        t = jnp.where((i >= pos) & (i < pos + n), first + (i - pos), t)
        pos += n
    return t


def _halo_specs(tm, width, colblk, regions, n_rows):
    per = tm // SUBLANES
    last8 = n_rows // SUBLANES - 1
    cur = pl.BlockSpec((tm, width), lambda i: (_seq_tile(i, regions), colblk))
    prv = pl.BlockSpec((SUBLANES, width), lambda i: (jnp.maximum(_seq_tile(i, regions) * per - 1, 0), colblk))
    nxt = pl.BlockSpec((SUBLANES, width),
                       lambda i: (jnp.minimum((_seq_tile(i, regions) + 1) * per, last8), colblk))
    return [cur, prv, nxt]


def _hy_prep_body(x_ref, p_ref, n_ref, w_ref, b_ref, v_ref, x1_ref, x2_ref, *, regions):
    at_start, at_end = _seq_edges(pl.program_id(0), regions)
    y = _conv3(x_ref[...], p_ref[SUBLANES - 1:SUBLANES, :], n_ref[0:1, :], w_ref[...], b_ref[...], at_start, at_end)
    for k, ref in enumerate((v_ref, x1_ref, x2_ref)):
        for h in range(HY_HALVES):
            ref[h] = y[:, k * HY_CH + h * LANES:k * HY_CH + (h + 1) * LANES]


def _hy_prep(u_hy, w, b, first_tile, n_tiles, tiles_per_seq):
    M, W = u_hy.shape
    tm = ROW_TILE
    regions = ((first_tile, n_tiles, tiles_per_seq),)
    out = jax.ShapeDtypeStruct((HY_HALVES, n_tiles * tm, LANES), F32)
    return pl.pallas_call(
        functools.partial(_hy_prep_body, regions=regions),
        grid=(n_tiles,),
        in_specs=_halo_specs(tm, W, 0, regions, M) + [pl.BlockSpec((3, W), lambda i: (0, 0)),
                                                      pl.BlockSpec((1, W), lambda i: (0, 0))],
        out_specs=[pl.BlockSpec((HY_HALVES, tm, LANES), lambda i: (0, i, 0))] * 3,
        out_shape=[out] * 3,
        compiler_params=_cparams("parallel"), name="hy_prep",
    )(u_hy, u_hy, u_hy, w, b[None])


def _filt_body(z_ref, w1_ref, b1_ref, w2_ref, b2_ref, fr_ref, w3_ref, b3_ref, env_ref, o_ref):
    f = fr_ref[...]
    h = jnp.sin(f * (_dot3(z_ref[...], w1_ref[...]) + b1_ref[...]))
    h = jnp.sin(f * (_dot3(h, w2_ref[...]) + b2_ref[...]))
    k = _dot3(h, w3_ref[...]) + b3_ref[...]
    e = env_ref[...]
    kf = k[:, :HY_CH] * e
    kb = k[:, HY_CH:] * e
    s = (jnp.sum(jnp.abs(kf), axis=0, keepdims=True) + jnp.sum(jnp.abs(kb), axis=0, keepdims=True)
         - jnp.abs(kb[0:1, :]))
    o_ref[:, :HY_CH] = kf / s
    o_ref[:, HY_CH:] = kb / s


def _hy_filters(L, p):
    t = jnp.linspace(0.0, 1.0, L, dtype=F32)[:, None]
    bands = (HY_POS_EMB - 1) // 2
    w = (2.0 * math.pi / L) * jnp.arange(L, dtype=F32)[:, None]
    f = jnp.linspace(1e-4, bands - 1, bands, dtype=F32)[None, :]
    z = jnp.concatenate([t, jnp.cos(f * w), -jnp.sin(f * w)], axis=-1)
    zp = jnp.pad(z, ((0, 0), (0, LANES - HY_POS_EMB)))
    w1 = jnp.pad(p['hy_filt_w1'], ((0, LANES - HY_POS_EMB), (0, 0)))
    deltas = jnp.abs(jnp.linspace(math.log(HY_DECAY_TARGET) / HY_SLOW_DECAY,
                                  math.log(HY_DECAY_TARGET) / HY_FAST_DECAY, HY_CH, dtype=F32))
    env = jnp.exp(-t * deltas)
    hid = w1.shape[1]
    nout = p['hy_filt_w3'].shape[1]
    n_order = nout // (2 * HY_CH)
    c0 = lambda a: pl.BlockSpec(a.shape, lambda o: (0,) * a.ndim)
    args = (zp, w1, p['hy_filt_b1'][None], p['hy_filt_w2'], p['hy_filt_b2'][None], p['hy_sin_freq'][None],
            p['hy_filt_w3'], p['hy_filt_b3'][None], env)
    in_specs = [c0(a) for a in args[:6]]
    in_specs += [pl.BlockSpec((hid, 2 * HY_CH), lambda o: (0, o)), pl.BlockSpec((1, 2 * HY_CH), lambda o: (0, o)),
                 c0(env)]
    return pl.pallas_call(
        _filt_body, grid=(n_order,), in_specs=in_specs,
        out_specs=pl.BlockSpec((L, 2 * HY_CH), lambda o: (0, o)),
        out_shape=jax.ShapeDtypeStruct((L, nout), F32),
        compiler_params=_cparams("parallel"), name="hy_filter",
    )(*args)


def _fft_plan(L):
    N = 2 * L
    Bn = 128 if N >= 4096 else 16
    A = N // Bn
    assert A * Bn == N and A % 16 == 0
    return A, Bn


def _dft_small(A, N):
    ka = np.arange(A)[:, None]
    a = np.arange(A)[None, :]
    th = 2.0 * np.pi * ((ka * a) % A) / A
    d1 = np.concatenate([np.cos(th), -np.sin(th)], axis=0)
    d4 = np.concatenate([np.cos(th.T), -np.sin(th.T)], axis=1) / N
    return jnp.asarray(d1, F32), jnp.asarray(d4, F32)


def _dft_mid(A, Bn):
    N = A * Bn
    ka = jnp.arange(A, dtype=jnp.int32)[:, None, None]
    kb = jnp.arange(Bn, dtype=jnp.int32)[None, :, None]
    b = jnp.arange(Bn, dtype=jnp.int32)[None, None, :]
    m = (b * (kb * A + ka)) % N
    ph = m.astype(F32) * (2.0 * math.pi / N)
    c, s = jnp.cos(ph), jnp.sin(ph)
    mf = jnp.concatenate([jnp.concatenate([c, s], axis=2), jnp.concatenate([-s, c], axis=2)], axis=1)
    ct, st = jnp.swapaxes(c, 1, 2), jnp.swapaxes(s, 1, 2)
    mi = jnp.concatenate([jnp.concatenate([ct, -st], axis=2), jnp.concatenate([st, ct], axis=2)], axis=1)
    return mf.astype(BF16), mi.astype(BF16)


FFT_BT = 16
FFT_GROUP = 8


def _fft1_body(d_ref, x_ref, o_ref, *, Bn):
    _, A2, bt, C = o_ref.shape
    Ain = d_ref.shape[1] // bt
    b0 = pl.multiple_of(pl.program_id(1) * bt, bt)
    slabs = [jnp.concatenate([x_ref[h, pl.ds(a * Bn + b0, bt), :] for h in range(HY_HALVES)], axis=1)
             for a in range(Ain)]
    xblk = jnp.concatenate(slabs, axis=0).astype(BF16)
    o_ref[0] = jnp.dot(d_ref[...], xblk, preferred_element_type=F32).reshape(A2, bt, C)


def _fft1(d1, x, n_seq, Bn):
    A2, Ain = d1.shape
    rows = Ain * Bn
    bt = min(FFT_BT, Bn)
    dbig = jnp.kron(d1, jnp.eye(bt, dtype=F32)).astype(BF16)
    return pl.pallas_call(
        functools.partial(_fft1_body, Bn=Bn), grid=(n_seq, Bn // bt),
        in_specs=[pl.BlockSpec(dbig.shape, lambda s, j: (0, 0)),
                  pl.BlockSpec((HY_HALVES, rows, LANES), lambda s, j: (0, s, 0))],
        out_specs=pl.BlockSpec((1, A2, bt, HY_CH), lambda s, j: (s, 0, j, 0)),
        out_shape=jax.ShapeDtypeStruct((n_seq, A2, Bn, HY_CH), F32),
        compiler_params=_cparams("parallel", "arbitrary"), name="hy_dft_slow",
    )(dbig, x)


def _fast_operand(p_ref, j):
    return jnp.concatenate([p_ref[0, 0, j], p_ref[0, 1, j]], axis=0).astype(BF16)


def _fft_spec_body(p_ref, mf_ref, k_ref, *, Bn):
    for j in range(FFT_GROUP):
        X = jnp.dot(mf_ref[j], _fast_operand(p_ref, j), preferred_element_type=F32)
        k_ref[0, j, 0] = X[:Bn]
        k_ref[0, j, 1] = X[Bn:]


def _fft_spec(p5, mf):
    S, _, A, Bn, C = p5.shape
    G = FFT_GROUP
    return pl.pallas_call(
        functools.partial(_fft_spec_body, Bn=Bn), grid=(A // G, S),
        in_specs=[pl.BlockSpec((1, 2, G, Bn, C), lambda g, s: (s, 0, g, 0, 0)),
                  pl.BlockSpec((G, 2 * Bn, 2 * Bn), lambda g, s: (g, 0, 0))],
        out_specs=pl.BlockSpec((1, G, 2, Bn, C), lambda g, s: (s, g, 0, 0, 0)),
        out_shape=jax.ShapeDtypeStruct((S, A, 2, Bn, C), F32),
        compiler_params=_cparams("parallel", "parallel"), name="hy_filter_spec",
    )(p5, mf)


def _fft_mid_body(p_ref, mf_ref, mi_ref, k_ref, q_ref, *, Bn):
    for j in range(FFT_GROUP):
        X = jnp.dot(mf_ref[j], _fast_operand(p_ref, j), preferred_element_type=F32)
        xr, xi = X[:Bn], X[Bn:]
        kr, ki = k_ref[j, 0], k_ref[j, 1]
        Y = jnp.concatenate([xr * kr - xi * ki, xr * ki + xi * kr], axis=0).astype(BF16)
        Q = jnp.dot(mi_ref[j], Y, preferred_element_type=F32)
        q_ref[0, j, 0] = Q[:Bn]
        q_ref[0, j, 1] = Q[Bn:]


def _fft_mid(p5, mf, mi, kspec):
    S, _, A, Bn, C = p5.shape
    G = FFT_GROUP
    mat = pl.BlockSpec((G, 2 * Bn, 2 * Bn), lambda g, s: (g, 0, 0))
    return pl.pallas_call(
        functools.partial(_fft_mid_body, Bn=Bn), grid=(A // G, S),
        in_specs=[pl.BlockSpec((1, 2, G, Bn, C), lambda g, s: (s, 0, g, 0, 0)), mat, mat,
                  pl.BlockSpec((G, 2, Bn, C), lambda g, s: (g, 0, 0, 0))],
        out_specs=pl.BlockSpec((1, G, 2, Bn, C), lambda g, s: (s, g, 0, 0, 0)),
        out_shape=jax.ShapeDtypeStruct((S, A, 2, Bn, C), F32),
        compiler_params=_cparams("parallel", "parallel"), name="hy_dft_fast",
    )(p5, mf, mi, kspec)


def _fft4_body(d_ref, q_ref, z_ref, g_ref, dch_ref, o_ref, *, Bn):
    _, A2, bt, C = q_ref.shape
    Ah = d_ref.shape[0] // bt
    b0 = pl.multiple_of(pl.program_id(1) * bt, bt)
    qblk = q_ref[0].reshape(A2 * bt, C).astype(BF16)
    y = jnp.dot(d_ref[...], qblk, preferred_element_type=F32)
    dch = dch_ref[...]
    for a in range(Ah):
        rows = pl.ds(a * Bn + b0, bt)
        ya = y[a * bt:(a + 1) * bt]
        for h in range(HY_HALVES):
            lanes = slice(h * LANES, (h + 1) * LANES)
            o_ref[h, rows, :] = g_ref[h, rows, :] * (ya[:, lanes] + z_ref[h, rows, :] * dch[:, lanes])


def _fft4(d4, q4, z, gate, dch):
    S, A2, Bn, C = q4.shape
    Ah = d4.shape[0]
    L = Ah * Bn
    bt = min(FFT_BT, Bn)
    dbig = jnp.kron(d4, jnp.eye(bt, dtype=F32)).astype(BF16)
    sig = pl.BlockSpec((HY_HALVES, L, LANES), lambda s, j: (0, s, 0))
    return pl.pallas_call(
        functools.partial(_fft4_body, Bn=Bn), grid=(S, Bn // bt),
        in_specs=[pl.BlockSpec(dbig.shape, lambda s, j: (0, 0)),
                  pl.BlockSpec((1, A2, bt, C), lambda s, j: (s, 0, j, 0)), sig, sig,
                  pl.BlockSpec((1, C), lambda s, j: (0, 0))],
        out_specs=sig,
        out_shape=jax.ShapeDtypeStruct((HY_HALVES, S * L, LANES), F32),
        compiler_params=_cparams("parallel", "arbitrary"), name="hy_idft_gate",
    )(dbig, q4, z, gate, dch)


def _hyena(v, x1, x2, n_seq, L, p):
    C = HY_CH
    A, Bn = _fft_plan(L)
    Ah = A // 2
    d1, d4 = _dft_small(A, A * Bn)
    mf, mi = _dft_mid(A, Bn)
    d4 = d4.reshape(A, 2, A).transpose(0, 2, 1).reshape(A, 2 * A)
    kn = _hy_filters(L, p).reshape(L, -1, 2, C)
    n_order = kn.shape[1]
    zero = jnp.zeros((1, C), F32)
    k2 = jnp.concatenate([piece for o in range(n_order)
                          for piece in (kn[:, o, 0], zero, jnp.flip(kn[1:, o, 1], axis=0))], axis=0)
    k2 = k2.reshape(-1, HY_HALVES, LANES).transpose(1, 0, 2)
    kspec = _fft_spec(_fft1(d1, k2, n_order, Bn).reshape(n_order, 2, A, Bn, C), mf)
    z = v
    for o, gate in enumerate((x1, x2)):
        P = _fft1(d1[:, :Ah], z, n_seq, Bn).reshape(n_seq, 2, A, Bn, C)
        Q = _fft_mid(P, mf, mi, kspec[o]).reshape(n_seq, 2 * A, Bn, C)
        z = _fft4(d4[:Ah], Q, z, gate, p['hy_bias_d'][o][None])
    return z


def _ml_prep_body(x_ref, p_ref, n_ref, w_ref, b_ref, q_ref, kt_ref, *, regions):
    at_start, at_end = _seq_edges(pl.program_id(0), regions)
    y = _conv3(x_ref[...], p_ref[SUBLANES - 1:SUBLANES, :], n_ref[0:1, :], w_ref[...], b_ref[...], at_start, at_end)
    y = y * jax.nn.sigmoid(y)
    q_ref[...] = y[:, :ML_W].astype(BF16)
    kt_ref[...] = (y[:, ML_W:] * (ML_HD ** -0.5)).T


def _ml_prep(u_ml, w, b, regions):
    M = u_ml.shape[0]
    tm = ROW_TILE
    n = sum(r[1] for r in regions)
    return pl.pallas_call(
        functools.partial(_ml_prep_body, regions=regions), grid=(n,),
        in_specs=_halo_specs(tm, 2 * ML_W, 0, regions, M) + [pl.BlockSpec((3, 2 * ML_W), lambda i: (0, 0)),
                                                             pl.BlockSpec((1, 2 * ML_W), lambda i: (0, 0))],
        out_specs=[pl.BlockSpec((tm, ML_W), lambda i: (_seq_tile(i, regions), 0)),
                   pl.BlockSpec((ML_W, tm), lambda i: (0, _seq_tile(i, regions)))],
        out_shape=[jax.ShapeDtypeStruct((M, ML_W), BF16), jax.ShapeDtypeStruct((ML_W, M), F32)],
        compiler_params=_cparams("parallel"), name="ml_prep",
    )(u_ml, u_ml, u_ml, w, b[None])


def _ml_body(*refs):
    ins, outs, scr = refs[:8], refs[8:10], refs[10:]

    @pl.when(pl.program_id(1) == 0)
    def _():
        for ref in scr:
            ref[...] = jnp.zeros_like(ref)

    for d in range(2):
        _ml_chunk(d, *ins[4 * d:4 * d + 4], outs[d], *scr[2 * d:2 * d + 2])


def _ml_chunk(d, q_ref, kt_ref, v_ref, g_ref, o_ref, st_ref, m_ref):
    T = q_ref.shape[0]
    H, W = ML_HEADS, ML_W
    WA = W + LANES
    sgn = 1 - 2 * d
    g = g_ref[...]
    gs = g[2 * H * d:2 * H * (d + 1)]
    ig = gs[0:H]
    lf8 = -(jnp.maximum(-gs, 0.0) + jnp.log1p(jnp.exp(-jnp.abs(gs))))
    lf = lf8[H:2 * H]
    r_i = lax.broadcasted_iota(jnp.int32, (T, T), 0)
    c_i = lax.broadcasted_iota(jnp.int32, (T, T), 1)
    prec = ((c_i - r_i) * sgn) <= 0
    incl = jnp.where(((r_i - c_i) * sgn) <= 0, 1.0, 0.0).astype(BF16)
    after = jnp.where(((c_i - r_i) * sgn) < 0, 1.0, 0.0).astype(BF16)
    b_rows = _dot2_exact_rhs(lf8, incl)[H:2 * H]
    bL = jnp.sum(lf, axis=1, keepdims=True)
    a_row = bL - b_rows + ig
    m_loc = jnp.max(a_row, axis=1, keepdims=True)
    w_row = jnp.exp(a_row - m_loc)
    m0 = m_ref[0:H, 0:1]
    m_new = jnp.maximum(bL + m0, m_loc)
    s_old = jnp.exp(bL + m0 - m_new)
    s_loc = jnp.exp(m_loc - m_new)

    qb = q_ref[...]
    kt = kt_ref[...]
    row_head = lax.broadcasted_iota(jnp.int32, (W, T), 0) // ML_HD
    lane = lax.broadcasted_iota(jnp.int32, (1, WA), 1)
    lane_head = jnp.where(lane < W, lane // ML_HD, lane - W)
    v_aug = jnp.concatenate([v_ref[...], jnp.ones((T, LANES), F32)], axis=1)

    Lfs = [jnp.where(prec, lf[h:h + 1], 0.0) for h in range(H)]
    E_all = _dot2_exact_rhs(jnp.concatenate(Lfs, axis=0), after)
    kth_all = jnp.concatenate([jnp.where(row_head == h, kt, 0.0) for h in range(H)], axis=1).astype(BF16)
    S_all = jnp.dot(qb, kth_all, preferred_element_type=F32)

    ps, vbd = [], []
    w_inter = jnp.zeros((T, WA), F32)
    e_m = jnp.zeros((T, WA), F32)
    for h in range(H):
        b_col = jnp.sum(Lfs[h], axis=1, keepdims=True)
        Dm = jnp.where(prec, E_all[h * T:(h + 1) * T] + ig[h:h + 1], -jnp.inf)
        inter = b_col + m0[h:h + 1]
        m_col = jnp.maximum(inter, jnp.max(Dm, axis=1, keepdims=True))
        S = S_all[:, h * T:(h + 1) * T]
        ps.append((jnp.exp(Dm - m_col) * S).astype(BF16))
        sel = lane_head == h
        vbd.append(jnp.where(sel, v_aug, 0.0).astype(BF16))
        w_inter = w_inter + jnp.where(sel, jnp.exp(inter - m_col), 0.0)
        e_m = e_m + jnp.where(sel, jnp.exp(-m_col), 0.0)
    nd = jnp.dot(jnp.concatenate(ps, axis=1), jnp.concatenate(vbd, axis=0), preferred_element_type=F32)
    nd = nd + w_inter * jnp.dot(qb, st_ref[...].astype(BF16), preferred_element_type=F32)
    den = jnp.zeros((T, W), F32)
    for h in range(H):
        den = den + jnp.where(lane_head[:, :W] == h, nd[:, W + h:W + h + 1], 0.0)
    o_ref[...] = nd[:, :W] / jnp.maximum(jnp.abs(den), e_m[:, :W])

    wk = jnp.zeros((W, T), F32)
    scol = jnp.zeros((1, WA), F32)
    ws = w_row * s_loc
    for h in range(H):
        wk = wk + jnp.where(row_head == h, ws[h:h + 1], 0.0)
        scol = scol + jnp.where(lane_head == h, s_old[h:h + 1], 0.0)
    st_loc = jnp.dot((kt * wk).astype(BF16), v_aug.astype(BF16), preferred_element_type=F32)
    diag = (lax.broadcasted_iota(jnp.int32, (W, WA), 0) // ML_HD) == lane_head
    st_ref[...] = jnp.where(diag, st_ref[...] * scol + st_loc, 0.0)
    m_ref[0:H, :] = jnp.broadcast_to(m_new, (H, LANES))


def _mlstm(q, kt, u_ml, g_t, B, L, Lc):
    M = q.shape[0]
    T = ML_CHUNK
    nC, nL = Lc // T, L // T
    lat0 = 0
    ctx0 = (B * L) // T

    def blk(d, b, i):
        cc = i if d == 0 else nC - 1 - i
        j = i - nC
        cl = j if d == 0 else nL - 1 - j
        return jnp.where(i < nC, ctx0 + b * nC + cc, lat0 + b * nL + cl)

    in_specs, out_specs = [], []
    for d in range(2):
        rows = functools.partial(lambda b, i, d, c: (blk(d, b, i), c), d=d)
        cols = functools.partial(lambda b, i, d: (0, blk(d, b, i)), d=d)
        in_specs += [pl.BlockSpec((T, ML_W), functools.partial(rows, c=0)), pl.BlockSpec((ML_W, T), cols),
                     pl.BlockSpec((T, ML_W), functools.partial(rows, c=2)), pl.BlockSpec((4 * ML_HEADS, T), cols)]
        out_specs.append(pl.BlockSpec((T, ML_W), functools.partial(rows, c=0)))
    state = [pltpu.VMEM((ML_W, ML_W + LANES), F32), pltpu.VMEM((SUBLANES, LANES), F32)]
    return pl.pallas_call(
        _ml_body, grid=(B, nC + nL), in_specs=in_specs, out_specs=out_specs,
        out_shape=[jax.ShapeDtypeStruct((M, ML_W), F32)] * 2,
        scratch_shapes=state * 2,
        compiler_params=_cparams("parallel", "arbitrary"), name="mlstm",
    )(*([q, kt, u_ml, g_t] * 2))


def _rope_tables(L):
    rows = L // GRID_W
    row = jnp.repeat(jnp.arange(rows, dtype=F32), GRID_W)
    col = jnp.tile(jnp.arange(GRID_W, dtype=F32), rows)
    half = DA_HD // 2
    inv = ROPE_THETA ** (-jnp.arange(0, half, 2, dtype=F32) / half)
    ang = jnp.stack([row, col], axis=-1)[:, :, None] * inv
    ang = jnp.stack([ang, ang], axis=-2).reshape(-1, DA_HD)
    ang = jnp.concatenate([ang, ang], axis=1)
    return jnp.cos(ang), jnp.sin(ang)


def _rope_qk(u, cos_ref, sin_ref, is_lat):
    reps = DA_QK // cos_ref.shape[1]
    cs = jnp.concatenate([cos_ref[...]] * reps, axis=1)
    sn = jnp.concatenate([sin_ref[...]] * reps, axis=1)
    q4 = DA_HD // 4
    src = lax.broadcasted_iota(jnp.int32, (LANES, LANES), 0)
    dst = lax.broadcasted_iota(jnp.int32, (LANES, LANES), 1)
    first = (dst % (2 * q4)) < q4
    perm = jnp.where(first & (src == dst + q4), -1.0, jnp.where((~first) & (src == dst - q4), 1.0, 0.0)).astype(BF16)

    def rope(x):
        rot = jnp.concatenate([_dot2_exact_rhs(x[:, g * LANES:(g + 1) * LANES], perm)
                               for g in range(x.shape[1] // LANES)], axis=1)
        return jnp.where(is_lat, x * cs + rot * sn, x)

    q = (rope(u[:, :DA_QK]) * (DA_HD ** -0.5 * math.log2(math.e))).astype(BF16)
    return q, rope(u[:, DA_QK:2 * DA_QK]).astype(BF16)


def _da_body(lam_ref, w_ref, q_ref, *rest, nseg, lam_init, sub):
    ks, vas, o_ref = rest[:nseg], rest[nseg:2 * nseg], rest[2 * nseg]
    lp = lam_ref[...]
    lam = (jnp.exp(jnp.sum(lp[0:1] * lp[1:2], axis=1, keepdims=True))
           - jnp.exp(jnp.sum(lp[2:3] * lp[3:4], axis=1, keepdims=True)) + lam_init)
    HW = 2 * DA_HD
    lane = lax.broadcasted_iota(jnp.int32, (1, HW), 1)
    n_sub = q_ref.shape[0] // sub
    def scores(i):
        t, m = divmod(i, 2)
        q = q_ref[t * sub:(t + 1) * sub, :]
        qm = jnp.where((lane // DA_HD) == m, q, jnp.zeros_like(q))
        return [lax.dot_general(k[...], qm, (((1,), (1,)), ((), ())), preferred_element_type=F32) for k in ks]

    def softmax(ss):
        mx = functools.reduce(jnp.maximum, [jnp.max(s, axis=0, keepdims=True) for s in ss])
        ps = [jnp.exp2(s - mx) for s in ss]
        den = functools.reduce(jnp.add, [jnp.sum(pr, axis=0, keepdims=True) for pr in ps])
        return [pr.astype(BF16) for pr in ps], den

    def values(ps, den):
        acc = functools.reduce(jnp.add, [jnp.dot(va[...], pr, preferred_element_type=F32)
                                         for pr, va in zip(ps, vas)])
        return acc / den

    n_streams = 2 * n_sub
    sss = {0: scores(0)}
    if n_streams > 1:
        sss[1] = scores(1)
    outs = []
    for i in range(n_streams):
        pd = softmax(sss.pop(i))
        if i + 2 < n_streams:
            sss[i + 2] = scores(i + 2)
        outs.append(values(*pd))
    for t in range(n_sub):
        o = outs[2 * t] - lam * outs[2 * t + 1]
        ms = jnp.mean(o * o, axis=0, keepdims=True)
        o_ref[t * sub:(t + 1) * sub, :] = ((o * lax.rsqrt(ms + NORM_EPS) * w_ref[...]) * (1.0 - lam_init)).T


def _diff_attn(qa, ka, vaa, lam_p, subln_w, lam_init, n_batch, q_rows0, q_len, segs):
    tq = min(ATT_Q_TILE, q_len)
    HW = 2 * DA_HD
    nq = q_len // tq
    q0 = q_rows0 // tq
    k_specs = [pl.BlockSpec((n, HW), functools.partial(lambda b, h, i, f, n: (f // n + b, h), f=f, n=n))
               for (f, n) in segs]
    va_specs = [pl.BlockSpec((HW, n), functools.partial(lambda b, h, i, f, n: (h, f // n + b), f=f, n=n))
                for (f, n) in segs]
    return pl.pallas_call(
        functools.partial(_da_body, nseg=len(segs), lam_init=lam_init, sub=min(ATT_SUB_TILE, tq)),
        grid=(n_batch, DA_HEADS, nq),
        in_specs=[pl.BlockSpec(lam_p.shape, lambda b, h, i: (0, 0)), pl.BlockSpec((HW, 1), lambda b, h, i: (0, 0)),
                  pl.BlockSpec((tq, HW), lambda b, h, i: (q0 + b * nq + i, h))] + k_specs + va_specs,
        out_specs=pl.BlockSpec((tq, HW), lambda b, h, i: (b * nq + i, h)),
        out_shape=jax.ShapeDtypeStruct((n_batch * q_len, DA_V), F32),
        compiler_params=_cparams("parallel", "parallel", "parallel"), name="diff_attn",
    )(lam_p, subln_w[:, None], qa, *([ka] * len(segs)), *([vaa] * len(segs)))


def _proj_out_body(x_ref, mod_ref, hy_ref, hf_ref, hb_ref, og_ref, da_ref, mw_ref, why_ref, wml_ref, wda_ref, o_ref):
    hs = hf_ref[...] + hb_ref[...]
    W = hs.shape[1]
    r = lax.broadcasted_iota(jnp.int32, (W, W), 0) // ML_HD
    c = lax.broadcasted_iota(jnp.int32, (W, W), 1) // ML_HD
    same_head = jnp.where(r == c, 1.0, 0.0).astype(BF16)
    ms = _dot2_exact_rhs(hs * hs, same_head) * (1.0 / ML_HD)
    y_ml = jax.nn.sigmoid(og_ref[...]) * (hs * lax.rsqrt(ms + NORM_EPS) * mw_ref[...])
    y_hy = jnp.concatenate([hy_ref[h] for h in range(HY_HALVES)], axis=1)
    y = (_dot(y_hy, why_ref[...]) + _dot(y_ml, wml_ref[...])) + _dot(da_ref[...], wda_ref[...])
    o_ref[...] = x_ref[...] + mod_ref[0, 2:3, :] * y


def _proj_out(xa, mod, y_hy, h_ml, u_ml, y_da, ml_norm_w, w_out, n_rows, n_lat_tiles, tiles_per_batch, n_batch):
    D = xa.shape[1]
    tm = ROW_TILE
    wb = w_out.astype(BF16)
    why, wml, wda = wb[:HY_CH], wb[HY_CH:HY_CH + ML_W], wb[HY_CH + ML_W:]
    full = lambda a: pl.BlockSpec(a.shape, lambda i: (0,) * a.ndim)
    row = lambda wd, cb=0: pl.BlockSpec((tm, wd), lambda i: (i, cb))
    mw = ml_norm_w[None]
    return pl.pallas_call(
        _proj_out_body, grid=(n_rows // tm,),
        in_specs=[row(D), pl.BlockSpec((1,) + mod.shape[1:],
                                       lambda i: (_mod_index(i, n_lat_tiles, tiles_per_batch, n_batch), 0, 0)),
                  pl.BlockSpec((HY_HALVES, tm, LANES), lambda i: (0, i, 0)),
                  row(ML_W), row(ML_W), row(ML_W, 3), row(DA_V),
                  full(mw), full(why), full(wml), full(wda)],
        out_specs=row(D),
        out_shape=jax.ShapeDtypeStruct((n_rows, D), F32),
        compiler_params=_cparams("parallel"), name="proj_out",
    )(xa, mod, y_hy, *h_ml, u_ml, y_da, mw, why, wml, wda)


ROUTE_LANE0 = MOE_GROUPS


def _router_body(x_ref, mod_ref, nw_ref, wr_ref, br_ref, h_ref, ri_ref, rf_ref, cnt_ref, run_ref):
    @pl.when(pl.program_id(0) == 0)
    def _():
        run_ref[...] = jnp.zeros_like(run_ref)

    h = _norm_mod(x_ref[...], nw_ref[...], mod_ref[0, 3:4, :], mod_ref[0, 4:5, :])
    h_ref[...] = h
    lg = _dot3(h, wr_ref[...]) + br_ref[...]
    tm = lg.shape[0]
    lane = lax.broadcasted_iota(jnp.int32, lg.shape, 1)
    neg = -jnp.inf
    is_g = lane < MOE_GROUPS
    gl = jnp.where(is_g, lg, neg)
    gmax = jnp.max(gl, axis=1, keepdims=True)
    gidx = jnp.min(jnp.where(gl == gmax, lane, LANES), axis=1, keepdims=True)
    gw = 1.0 / jnp.sum(jnp.where(is_g, jnp.exp(gl - gmax), 0.0), axis=1, keepdims=True)
    e_of = lane - ROUTE_LANE0
    in_grp = (e_of >= 0) & (e_of < MOE_EXPERTS) & ((e_of // MOE_PER_GROUP) == gidx)
    el = jnp.where(in_grp, lg, neg)
    t1 = jnp.max(el, axis=1, keepdims=True)
    i1 = jnp.min(jnp.where(el == t1, lane, LANES), axis=1, keepdims=True)
    el2 = jnp.where(lane == i1, neg, el)
    t2 = jnp.max(el2, axis=1, keepdims=True)
    i2 = jnp.min(jnp.where(el2 == t2, lane, LANES), axis=1, keepdims=True)
    ex = jnp.exp(t2 - t1)
    g1 = gw / (1.0 + ex)
    g2 = gw * ex / (1.0 + ex)
    oh = jnp.where((lane == i1) | (lane == i2), 1.0, 0.0)
    r_i = lax.broadcasted_iota(jnp.int32, (tm, tm), 0)
    c_i = lax.broadcasted_iota(jnp.int32, (tm, tm), 1)
    earlier = jnp.where(c_i < r_i, 1.0, 0.0).astype(BF16)
    cum = jnp.dot(earlier, oh.astype(BF16), preferred_element_type=F32) + run_ref[0:1, :]
    r1 = jnp.sum(jnp.where(lane == i1, cum, 0.0), axis=1, keepdims=True).astype(jnp.int32)
    r2 = jnp.sum(jnp.where(lane == i2, cum, 0.0), axis=1, keepdims=True).astype(jnp.int32)
    run = run_ref[0:1, :] + jnp.sum(oh, axis=0, keepdims=True)
    run_ref[...] = jnp.broadcast_to(run, run_ref.shape)
    cnt_ref[...] = jnp.broadcast_to(run, cnt_ref.shape)
    zi = jnp.zeros_like(lane)
    ri_ref[...] = jnp.where(lane == 0, i1 - ROUTE_LANE0, jnp.where(lane == 1, i2 - ROUTE_LANE0,
                            jnp.where(lane == 2, r1, jnp.where(lane == 3, r2, zi))))
    rf_ref[...] = jnp.where(lane == 0, g1, jnp.where(lane == 1, g2, 0.0))


def _router(xa, mod, nw, wg, bg, we, be, n_rows, n_lat_tiles, tiles_per_batch, n_batch):
    D = xa.shape[1]
    tm = ROW_TILE
    pad = LANES - MOE_GROUPS - MOE_EXPERTS
    wr = jnp.concatenate([wg, we, jnp.zeros((D, pad), F32)], axis=1)
    br = jnp.concatenate([bg, be, jnp.zeros((pad,), F32)])[None]
    row = lambda wd: pl.BlockSpec((tm, wd), lambda i: (i, 0))
    full = lambda a: pl.BlockSpec(a.shape, lambda i: (0,) * a.ndim)
    return pl.pallas_call(
        _router_body, grid=(n_rows // tm,),
        in_specs=[row(D), pl.BlockSpec((1,) + mod.shape[1:],
                                       lambda i: (_mod_index(i, n_lat_tiles, tiles_per_batch, n_batch), 0, 0)),
                  full(nw[None]), full(wr), full(br)],
        out_specs=[row(D), row(LANES), row(LANES), pl.BlockSpec((SUBLANES, LANES), lambda i: (0, 0))],
        out_shape=[jax.ShapeDtypeStruct((n_rows, D), F32), jax.ShapeDtypeStruct((n_rows, LANES), jnp.int32),
                   jax.ShapeDtypeStruct((n_rows, LANES), F32), jax.ShapeDtypeStruct((SUBLANES, LANES), F32)],
        scratch_shapes=[pltpu.VMEM((SUBLANES, LANES), F32)],
        compiler_params=_cparams("arbitrary"), name="moe_router",
    )(xa, mod, nw[None], wr, br)


def _plan_body(ri_ref, base_ref, d_ref):
    ri = ri_ref[...]
    lane = lax.broadcasted_iota(jnp.int32, ri.shape, 1)
    base = base_ref[...]
    d = []
    for j in range(2):
        b = jnp.sum(jnp.where(lane == ri[:, j:j + 1], base, 0.0), axis=1, keepdims=True)
        d.append(b + ri[:, 2 + j:3 + j].astype(F32))
    slots = jnp.where(lane == 0, d[0], jnp.where(lane == 1, d[1], 0.0))
    d_ref[0] = slots.T[0:SUBLANES, :].astype(jnp.int32)


def _plan(ri, base):
    n_rows = ri.shape[0]
    tm = ROW_TILE
    basef = jnp.pad(base.astype(F32), (0, LANES - base.shape[0]))[None]
    out = pl.pallas_call(
        _plan_body, grid=(n_rows // tm,),
        in_specs=[pl.BlockSpec((tm, LANES), lambda i: (i, 0)), pl.BlockSpec((1, LANES), lambda i: (0, 0))],
        out_specs=pl.BlockSpec((1, SUBLANES, tm), lambda i: (i, 0, 0)),
        out_shape=jax.ShapeDtypeStruct((n_rows // tm, SUBLANES, tm), jnp.int32),
        compiler_params=_cparams("parallel"), name="moe_plan",
    )(ri, basef)
    return out[:, 0, :], out[:, 1, :]


def _dispatch_body(d0_ref, d1_ref, lb_ref, nb_ref, h_ref, xb_ref, zbuf, sem, zsem):
    i = pl.program_id(0)
    tm = h_ref.shape[0]
    TB = zbuf.shape[0]
    n_blk = xb_ref.shape[0] // TB

    def zero_copy(blk):
        return pltpu.make_async_copy(zbuf, xb_ref.at[pl.ds(pl.multiple_of(blk * TB, TB), TB)], zsem)

    @pl.when(i == 0)
    def _():
        zbuf[...] = jnp.zeros_like(zbuf)
        for phase in ("start", "wait"):
            def tail(blk, carry, phase=phase):
                getattr(zero_copy(blk), phase)()
                return carry

            for e in range(lb_ref.shape[0]):
                @pl.when(lb_ref[e] >= 0)
                def _(e=e, phase=phase):
                    getattr(zero_copy(lb_ref[e]), phase)()
            lax.fori_loop(nb_ref[0], n_blk, tail, 0)

    for r in range(tm):
        for d_ref in (d0_ref, d1_ref):
            pltpu.make_async_copy(h_ref.at[pl.ds(r, 1)], xb_ref.at[pl.ds(d_ref[i, r], 1)], sem).start()
    pltpu.make_async_copy(xb_ref.at[pl.ds(0, 2 * tm)], xb_ref.at[pl.ds(0, 2 * tm)], sem).wait()


def _dispatch(d0, d1, last_blk, n_used, h2, n_slot_rows):
    n_rows, D = h2.shape
    tm = ROW_TILE
    return pl.pallas_call(
        _dispatch_body,
        grid_spec=pltpu.PrefetchScalarGridSpec(
            num_scalar_prefetch=4, grid=(n_rows // tm,),
            in_specs=[pl.BlockSpec((tm, D), lambda i, *_: (i, 0))],
            out_specs=pl.BlockSpec(memory_space=pl.ANY),
            scratch_shapes=[pltpu.VMEM((MOE_ROWS, D), F32), pltpu.SemaphoreType.DMA(()),
                            pltpu.SemaphoreType.DMA(())]),
        out_shape=jax.ShapeDtypeStruct((n_slot_rows, D), F32),
        compiler_params=pltpu.CompilerParams(dimension_semantics=("arbitrary",), vmem_limit_bytes=VMEM_LIMIT,
                                             disable_bounds_checks=True),
        name="moe_dispatch",
    )(d0, d1, last_blk, n_used, h2)


def _ffn_body(be_ref, nb_ref, x_ref, w1_ref, w3_ref, w2_ref, y_ref, w1b, w3b, w2b):
    i = pl.program_id(0)
    used = i < nb_ref[0]
    new_expert = (i == 0) | (be_ref[i] != be_ref[jnp.maximum(i - 1, 0)])

    @pl.when(used & new_expert)
    def _():
        w1b[...] = w1_ref[0, 0].astype(BF16)
        w3b[...] = w3_ref[0, 0].astype(BF16)
        w2b[...] = w2_ref[0, 0].astype(BF16)

    @pl.when(used)
    def _():
        xb = x_ref[...].astype(BF16)
        a = jnp.dot(xb, w1b[...], preferred_element_type=F32)
        b = jnp.dot(xb, w3b[...], preferred_element_type=F32)
        hmid = ((a * jax.nn.sigmoid(a)) * b).astype(BF16)
        y_ref[...] = jnp.dot(hmid, w2b[...], preferred_element_type=F32)

    @pl.when(jnp.logical_not(used))
    def _():
        y_ref[...] = jnp.zeros_like(y_ref)


def _expert_ffn(blk_e, n_used, xb, w1, w3, w2, layer):
    P, D = xb.shape
    F = w1.shape[3]
    TB = MOE_ROWS
    rows = pl.BlockSpec((TB, D), lambda i, be, nb: (i, 0))
    return pl.pallas_call(
        _ffn_body,
        grid_spec=pltpu.PrefetchScalarGridSpec(
            num_scalar_prefetch=2, grid=(P // TB,),
            in_specs=[rows, pl.BlockSpec((1, 1, D, F), lambda i, be, nb: (layer, be[i], 0, 0)),
                      pl.BlockSpec((1, 1, D, F), lambda i, be, nb: (layer, be[i], 0, 0)),
                      pl.BlockSpec((1, 1, F, D), lambda i, be, nb: (layer, be[i], 0, 0))],
            out_specs=rows,
            scratch_shapes=[pltpu.VMEM((D, F), BF16), pltpu.VMEM((D, F), BF16), pltpu.VMEM((F, D), BF16)]),
        out_shape=jax.ShapeDtypeStruct((P, D), F32),
        compiler_params=_cparams("arbitrary"), name="moe_ffn",
    )(blk_e, n_used, xb, w1, w3, w2)


def _combine_body(d0_ref, d1_ref, x_ref, mod_ref, rf_ref, yb_ref, fw_ref, o_ref, buf, sem, *, final):
    i = pl.program_id(0)
    n = pl.num_programs(0)
    tm = x_ref.shape[0]

    def gather(step, slot):
        for r in range(tm):
            for j, d_ref in enumerate((d0_ref, d1_ref)):
                pltpu.make_async_copy(yb_ref.at[pl.ds(d_ref[step, r], 1)], buf.at[slot, j, pl.ds(r, 1)],
                                      sem.at[slot]).start()

    @pl.when(i == 0)
    def _():
        gather(0, 0)

    slot = i % 2
    pltpu.make_async_copy(buf.at[slot], buf.at[slot], sem.at[slot]).wait()

    @pl.when(i + 1 < n)
    def _():
        gather(i + 1, 1 - slot)

    g = rf_ref[...]
    f = g[:, 0:1] * buf[slot, 0] + g[:, 1:2] * buf[slot, 1]
    xn = x_ref[...] + mod_ref[0, 5:6, :] * f
    if final:
        ms = jnp.mean(xn * xn, axis=-1, keepdims=True)
        xn = xn * lax.rsqrt(ms + NORM_EPS) * fw_ref[...]
    o_ref[...] = xn


def _combine(d0, d1, xa, mod, rf, yb, final_w, final, n_rows, n_lat_tiles, tiles_per_batch, n_batch):
    D = xa.shape[1]
    tm = ROW_TILE
    row = lambda wd: pl.BlockSpec((tm, wd), lambda i, *_: (i, 0))
    return pl.pallas_call(
        functools.partial(_combine_body, final=final),
        grid_spec=pltpu.PrefetchScalarGridSpec(
            num_scalar_prefetch=2, grid=(n_rows // tm,),
            in_specs=[row(D), pl.BlockSpec((1,) + mod.shape[1:],
                                           lambda i, *_: (_mod_index(i, n_lat_tiles, tiles_per_batch, n_batch), 0, 0)),
                      row(LANES), pl.BlockSpec(memory_space=pl.ANY), pl.BlockSpec((1, D), lambda i, *_: (0, 0))],
            out_specs=row(D),
            scratch_shapes=[pltpu.VMEM((2, 2, tm, D), F32), pltpu.SemaphoreType.DMA((2,))]),
        out_shape=jax.ShapeDtypeStruct((n_rows, D), F32),
        compiler_params=pltpu.CompilerParams(dimension_semantics=("arbitrary",), vmem_limit_bytes=VMEM_LIMIT,
                                             disable_bounds_checks=True),
        name="moe_combine",
    )(d0, d1, xa, mod, rf, yb, final_w[None])


def _moe(xa, mod, p, expert_w, layer, final_w, final, n_rows, n_lat_tiles, tiles_per_batch, n_batch):
    TB = MOE_ROWS
    h2, ri, rf, cnt = _router(xa, mod, p['norm2_w'], p['moe_wg'], p['moe_bg'], p['moe_we'], p['moe_be'],
                              n_rows, n_lat_tiles, tiles_per_batch, n_batch)
    counts = cnt[0, ROUTE_LANE0:ROUTE_LANE0 + MOE_EXPERTS].astype(jnp.int32)
    pc = (counts + TB - 1) // TB * TB
    pend = jnp.cumsum(pc)
    base = pend - pc
    n_blk = -(-2 * n_rows // TB) + MOE_EXPERTS
    n_used = (pend[-1] // TB).astype(jnp.int32)
    blk = jnp.arange(n_blk, dtype=jnp.int32)
    blk_e = jnp.sum((pend[None, :] <= (jnp.minimum(blk, n_used - 1) * TB)[:, None]).astype(jnp.int32), axis=1)
    blk_e = jnp.minimum(blk_e, MOE_EXPERTS - 1).astype(jnp.int32)
    last_blk = jnp.where(pc > 0, pend // TB - 1, -1).astype(jnp.int32)
    d0, d1 = _plan(ri, base)
    xb = _dispatch(d0, d1, last_blk, n_used[None], h2, n_blk * TB)
    yb = _expert_ffn(blk_e, n_used[None], xb, *expert_w, layer)
    return _combine(d0, d1, xa, mod, rf, yb, final_w, final, n_rows, n_lat_tiles, tiles_per_batch, n_batch)


_LAYER_KEYS = ('ada_w', 'ada_b', 'norm1_w', 'norm2_w', 'w_in', 'b_in', 'w_out', 'hy_conv_w', 'hy_conv_b',
               'hy_filt_w1', 'hy_filt_b1', 'hy_filt_w2', 'hy_filt_b2', 'hy_filt_w3', 'hy_filt_b3', 'hy_sin_freq',
               'hy_bias_d', 'ml_conv_w', 'ml_conv_b', 'ml_norm_w', 'da_lambda', 'da_subln_w', 'moe_wg', 'moe_bg',
               'moe_we', 'moe_be', 'moe_w1', 'moe_w3', 'moe_w2')


def kernel(x, c, ctx, c_ctx, ada_w, ada_b, norm1_w, norm2_w, w_in, b_in, w_out, hy_conv_w, hy_conv_b, hy_filt_w1,
           hy_filt_b1, hy_filt_w2, hy_filt_b2, hy_filt_w3, hy_filt_b3, hy_sin_freq, hy_bias_d, ml_conv_w, ml_conv_b,
           ml_norm_w, da_lambda, da_subln_w, moe_wg, moe_bg, moe_we, moe_be, moe_w1, moe_w3, moe_w2, final_norm_w):
    stacked = dict(zip(_LAYER_KEYS, (ada_w, ada_b, norm1_w, norm2_w, w_in, b_in, w_out, hy_conv_w, hy_conv_b,
                                     hy_filt_w1, hy_filt_b1, hy_filt_w2, hy_filt_b2, hy_filt_w3, hy_filt_b3,
                                     hy_sin_freq, hy_bias_d, ml_conv_w, ml_conv_b, ml_norm_w, da_lambda, da_subln_w,
                                     moe_wg, moe_bg, moe_we, moe_be, moe_w1, moe_w3, moe_w2)))
    B, L, D = x.shape
    Lc = ctx.shape[1]
    depth = ada_w.shape[0]
    tm = ROW_TILE
    assert L % tm == 0 and Lc % tm == 0 and L % GRID_W == 0 and L % ML_CHUNK == 0 and Lc % ML_CHUNK == 0
    ML, MC = B * L, B * Lc
    M = ML + MC
    n_lat, n_ctx = ML // tm, MC // tm
    tpb = L // tm
    regions = ((0, n_lat, tpb), (n_lat, n_ctx, Lc // tm))

    xa = jnp.concatenate([x.reshape(ML, D), ctx.reshape(MC, D)], axis=0)
    R = -(-(B + 1) // SUBLANES) * SUBLANES
    cond = jnp.concatenate([c, c_ctx[None], jnp.zeros((R - B - 1, D), F32)], axis=0)
    cos, sin = _rope_tables(L)

    for l in range(depth):
        last = l == depth - 1
        p = {k: v[l] for k, v in stacked.items() if k not in ('moe_w1', 'moe_w3', 'moe_w2')}
        lam_init = 0.8 - 0.6 * math.exp(-0.3 * l)
        mod = _ada(cond, p['ada_w'], p['ada_b']).reshape(R, 6, D)
        u_hy, u_ml, qa, ka, va, g_t = _proj_in(xa, mod, p['norm1_w'], p['w_in'], p['b_in'], cos, sin, n_lat, tpb, B)

        y_hy = _hyena(*_hy_prep(u_hy, p['hy_conv_w'], p['hy_conv_b'], 0, n_lat, tpb), B, L, p)
        q_m, k_t = _ml_prep(u_ml, p['ml_conv_w'], p['ml_conv_b'], regions)
        h_ml = _mlstm(q_m, k_t, u_ml, g_t, B, L, Lc)
        y_da = _diff_attn(qa, ka, va, p['da_lambda'], p['da_subln_w'], lam_init, B, 0, L, ((0, L), (ML, Lc)))
        n_rows = ML if last else M
        if not last:
            yc_hy = _hyena(*_hy_prep(u_hy, p['hy_conv_w'], p['hy_conv_b'], n_lat, n_ctx, Lc // tm), B, Lc, p)
            yc_da = _diff_attn(qa, ka, va, p['da_lambda'], p['da_subln_w'], lam_init, B, ML, Lc, ((ML, Lc),))
            y_hy = jnp.concatenate([y_hy, yc_hy], axis=1)
            y_da = jnp.concatenate([y_da, yc_da], axis=0)
        xa = _proj_out(xa, mod, y_hy, h_ml, u_ml, y_da, p['ml_norm_w'], p['w_out'], n_rows, n_lat, tpb, B)
        xa = _moe(xa, mod, p, (moe_w1, moe_w3, moe_w2), l, final_norm_w, last, n_rows, n_lat, tpb, B)
    return xa.reshape(B, L, D)
```

```python
import functools
import math

import numpy as np
import jax
import jax.numpy as jnp
from jax import lax
from jax.experimental import pallas as pl
from jax.experimental.pallas import tpu as pltpu

F32 = jnp.float32
BF16 = jnp.bfloat16

NORM_EPS = 1e-6
GRID_W = 64

HY_CH = 256
HY_HALVES = 2
HY_POS_EMB = 33
HY_FAST_DECAY = 0.3
HY_SLOW_DECAY = 1.5
HY_DECAY_TARGET = 1e-2

ML_HEADS = 4
ML_HD = 64
ML_W = ML_HEADS * ML_HD

DA_HEADS = 4
DA_HD = 64
DA_QK = DA_HEADS * 2 * DA_HD
DA_V = DA_HEADS * 2 * DA_HD
ROPE_THETA = 10000.0

HY_OFF = 0
ML_OFF = HY_OFF + 3 * HY_CH
GATE_OFF = ML_OFF + 4 * ML_W
DA_OFF = GATE_OFF + 2 * 2 * ML_HEADS

MOE_GROUPS = 4
MOE_PER_GROUP = 8
MOE_EXPERTS = MOE_GROUPS * MOE_PER_GROUP

LANES = 128
SUBLANES = 8
VMEM_BYTES_V7X = 64 * 1024 * 1024
VMEM_LIMIT = VMEM_BYTES_V7X * 7 // 8

ROW_TILE = 256
ML_CHUNK = 128
ATT_Q_TILE = 1024
ATT_SUB_TILE = 512
MOE_ROWS = 512


def _cparams(*sem):
    return pltpu.CompilerParams(dimension_semantics=tuple(sem), vmem_limit_bytes=VMEM_LIMIT)


def _dot(a, b):
    return jnp.dot(a.astype(BF16), b.astype(BF16), preferred_element_type=F32)


def _split(a):
    hi = a.astype(BF16)
    lo = (a - hi.astype(F32)).astype(BF16)
    return hi, lo


def _dot3(a, b):
    ah, al = _split(a)
    bh, bl = _split(b)
    d = functools.partial(jnp.dot, preferred_element_type=F32)
    return d(ah, bh) + (d(ah, bl) + d(al, bh))


def _dot2_exact_rhs(a, b_exact):
    ah, al = _split(a)
    d = functools.partial(jnp.dot, preferred_element_type=F32)
    return d(ah, b_exact) + d(al, b_exact)


def _ada_body(a_ref, w_ref, b_ref, o_ref):
    a = a_ref[...]
    a = a * jax.nn.sigmoid(a)
    o_ref[...] = _dot3(a, w_ref[...]) + b_ref[...]


def _ada(cond, w, b):
    R, D = cond.shape
    N = w.shape[1]
    tn = 1536
    return pl.pallas_call(
        _ada_body,
        grid=(N // tn,),
        in_specs=[pl.BlockSpec((R, D), lambda j: (0, 0)),
                  pl.BlockSpec((D, tn), lambda j: (0, j)),
                  pl.BlockSpec((1, tn), lambda j: (0, j))],
        out_specs=pl.BlockSpec((R, tn), lambda j: (0, j)),
        out_shape=jax.ShapeDtypeStruct((R, N), F32),
        compiler_params=_cparams("parallel"), name="ada_mod",
    )(cond, w, b[None])


def _mod_index(i, n_lat_tiles, tiles_per_batch, n_batch):
    return jnp.where(i < n_lat_tiles, i // tiles_per_batch, n_batch)


def _norm_mod(x, nw, shift, scale):
    ms = jnp.mean(x * x, axis=-1, keepdims=True)
    return (x * lax.rsqrt(ms + NORM_EPS) * nw) * (1.0 + scale) + shift


def _proj_in_body(x_ref, mod_ref, cos_ref, sin_ref, nw_ref, why_ref, wml_ref, wqk_ref, wvt_ref, wgt_ref,
                  bhy_ref, bml_ref, bqk_ref, bvt_ref, bgt_ref,
                  ohy_ref, oml_ref, oq_ref, ok_ref, ovt_ref, ogt_ref, *, n_lat_tiles):
    h = _norm_mod(x_ref[...], nw_ref[...], mod_ref[0, 0:1, :], mod_ref[0, 1:2, :])
    hb = h.astype(BF16)
    d = functools.partial(jnp.dot, preferred_element_type=F32)
    nt = functools.partial(lax.dot_general, dimension_numbers=(((1,), (1,)), ((), ())), preferred_element_type=F32)
    ohy_ref[...] = d(hb, why_ref[...]) + bhy_ref[...]
    oml_ref[...] = d(hb, wml_ref[...]) + bml_ref[...]
    q, k = _rope_qk(d(hb, wqk_ref[...]) + bqk_ref[...], cos_ref, sin_ref, pl.program_id(0) < n_lat_tiles)
    oq_ref[...] = q
    ok_ref[...] = k
    ovt_ref[...] = (nt(wvt_ref[...], hb) + bvt_ref[...]).astype(BF16)
    ogt_ref[...] = nt(wgt_ref[...], hb) + bgt_ref[...]


def _proj_in(xa, mod, nw, w_in, b_in, cos, sin, n_lat_tiles, tiles_per_batch, n_batch):
    M, D = xa.shape
    tm = ROW_TILE
    wb = w_in.astype(BF16)
    v_off = DA_OFF + 2 * DA_QK
    why, wml, wg = wb[:, :ML_OFF], wb[:, ML_OFF:GATE_OFF], wb[:, GATE_OFF:DA_OFF]
    wqk, wv = wb[:, DA_OFF:v_off], wb[:, v_off:]
    bhy, bml, bg = b_in[:ML_OFF], b_in[ML_OFF:GATE_OFF], b_in[GATE_OFF:DA_OFF]
    bqk, bv = b_in[DA_OFF:v_off], b_in[v_off:]
    ng = wg.shape[1]
    full = lambda a: pl.BlockSpec(a.shape, lambda i: (0,) * a.ndim)
    tab = pl.BlockSpec((tm, cos.shape[1]), lambda i: (jnp.where(i < n_lat_tiles, i % tiles_per_batch, 0), 0))
    consts = (nw[None], why, wml, wqk, wv.T, wg.T, bhy[None], bml[None], bqk[None], bv[:, None], bg[:, None])
    in_specs = [pl.BlockSpec((tm, D), lambda i: (i, 0)),
                pl.BlockSpec((1,) + mod.shape[1:],
                             lambda i: (_mod_index(i, n_lat_tiles, tiles_per_batch, n_batch), 0, 0)), tab, tab]
    in_specs += [full(a) for a in consts]
    rows = lambda wd: pl.BlockSpec((tm, wd), lambda i: (i, 0))
    cols = lambda ht: pl.BlockSpec((ht, tm), lambda i: (0, i))
    out_specs = [rows(why.shape[1]), rows(wml.shape[1]), rows(DA_QK), rows(DA_QK), cols(DA_V), cols(ng)]
    out_shape = [jax.ShapeDtypeStruct((M, why.shape[1]), F32), jax.ShapeDtypeStruct((M, wml.shape[1]), F32),
                 jax.ShapeDtypeStruct((M, DA_QK), BF16), jax.ShapeDtypeStruct((M, DA_QK), BF16),
                 jax.ShapeDtypeStruct((DA_V, M), BF16), jax.ShapeDtypeStruct((ng, M), F32)]
    return pl.pallas_call(
        functools.partial(_proj_in_body, n_lat_tiles=n_lat_tiles), grid=(M // tm,),
        in_specs=in_specs, out_specs=out_specs, out_shape=out_shape,
        compiler_params=_cparams("parallel"), name="proj_in",
    )(xa, mod, cos, sin, *consts)


def _conv3(x, prev_row, next_row, w, b, at_start, at_end):
    T = x.shape[0]
    rows = lax.broadcasted_iota(jnp.int32, x.shape, 0)
    prev_row = jnp.where(at_start, 0.0, prev_row)
    next_row = jnp.where(at_end, 0.0, next_row)
    up = jnp.where(rows == 0, prev_row, pltpu.roll(x, 1, 0))
    dn = jnp.where(rows == T - 1, next_row, pltpu.roll(x, T - 1, 0))
    return up * w[0:1, :] + x * w[1:2, :] + dn * w[2:3, :] + b


def _seq_edges(i, regions):
    at_start = jnp.bool_(False)
    at_end = jnp.bool_(False)
    pos = 0
    for (_, n, tps) in regions:
        inside = (i >= pos) & (i < pos + n)
        r = (i - pos) % tps
        at_start = at_start | (inside & (r == 0))
        at_end = at_end | (inside & (r == tps - 1))
        pos += n
    return at_start, at_end


def _seq_tile(i, regions):
    pos = 0
    t = jnp.int32(0)
    for (first, n, _) in regions:
        t = jnp.where((i >= pos) & (i < pos + n), first + (i - pos), t)
        pos += n
    return t


def _halo_specs(tm, width, colblk, regions, n_rows):
    per = tm // SUBLANES
    last8 = n_rows // SUBLANES - 1
    cur = pl.BlockSpec((tm, width), lambda i: (_seq_tile(i, regions), colblk))
    prv = pl.BlockSpec((SUBLANES, width), lambda i: (jnp.maximum(_seq_tile(i, regions) * per - 1, 0), colblk))
    nxt = pl.BlockSpec((SUBLANES, width),
                       lambda i: (jnp.minimum((_seq_tile(i, regions) + 1) * per, last8), colblk))
    return [cur, prv, nxt]


def _hy_prep_body(x_ref, p_ref, n_ref, w_ref, b_ref, v_ref, x1_ref, x2_ref, *, regions):
    at_start, at_end = _seq_edges(pl.program_id(0), regions)
    y = _conv3(x_ref[...], p_ref[SUBLANES - 1:SUBLANES, :], n_ref[0:1, :], w_ref[...], b_ref[...], at_start, at_end)
    for k, ref in enumerate((v_ref, x1_ref, x2_ref)):
        for h in range(HY_HALVES):
            ref[h] = y[:, k * HY_CH + h * LANES:k * HY_CH + (h + 1) * LANES]


def _hy_prep(u_hy, w, b, first_tile, n_tiles, tiles_per_seq):
    M, W = u_hy.shape
    tm = ROW_TILE
    regions = ((first_tile, n_tiles, tiles_per_seq),)
    out = jax.ShapeDtypeStruct((HY_HALVES, n_tiles * tm, LANES), F32)
    return pl.pallas_call(
        functools.partial(_hy_prep_body, regions=regions),
        grid=(n_tiles,),
        in_specs=_halo_specs(tm, W, 0, regions, M) + [pl.BlockSpec((3, W), lambda i: (0, 0)),
                                                      pl.BlockSpec((1, W), lambda i: (0, 0))],
        out_specs=[pl.BlockSpec((HY_HALVES, tm, LANES), lambda i: (0, i, 0))] * 3,
        out_shape=[out] * 3,
        compiler_params=_cparams("parallel"), name="hy_prep",
    )(u_hy, u_hy, u_hy, w, b[None])


def _filt_body(z_ref, w1_ref, b1_ref, w2_ref, b2_ref, fr_ref, w3_ref, b3_ref, env_ref, o_ref):
    f = fr_ref[...]
    h = jnp.sin(f * (_dot3(z_ref[...], w1_ref[...]) + b1_ref[...]))
    h = jnp.sin(f * (_dot3(h, w2_ref[...]) + b2_ref[...]))
    k = _dot3(h, w3_ref[...]) + b3_ref[...]
    e = env_ref[...]
    kf = k[:, :HY_CH] * e
    kb = k[:, HY_CH:] * e
    s = (jnp.sum(jnp.abs(kf), axis=0, keepdims=True) + jnp.sum(jnp.abs(kb), axis=0, keepdims=True)
         - jnp.abs(kb[0:1, :]))
    o_ref[:, :HY_CH] = kf / s
    o_ref[:, HY_CH:] = kb / s


def _hy_filters(L, p):
    t = jnp.linspace(0.0, 1.0, L, dtype=F32)[:, None]
    bands = (HY_POS_EMB - 1) // 2
    w = (2.0 * math.pi / L) * jnp.arange(L, dtype=F32)[:, None]
    f = jnp.linspace(1e-4, bands - 1, bands, dtype=F32)[None, :]
    z = jnp.concatenate([t, jnp.cos(f * w), -jnp.sin(f * w)], axis=-1)
    zp = jnp.pad(z, ((0, 0), (0, LANES - HY_POS_EMB)))
    w1 = jnp.pad(p['hy_filt_w1'], ((0, LANES - HY_POS_EMB), (0, 0)))
    deltas = jnp.abs(jnp.linspace(math.log(HY_DECAY_TARGET) / HY_SLOW_DECAY,
                                  math.log(HY_DECAY_TARGET) / HY_FAST_DECAY, HY_CH, dtype=F32))
    env = jnp.exp(-t * deltas)
    hid = w1.shape[1]
    nout = p['hy_filt_w3'].shape[1]
    n_order = nout // (2 * HY_CH)
    c0 = lambda a: pl.BlockSpec(a.shape, lambda o: (0,) * a.ndim)
    args = (zp, w1, p['hy_filt_b1'][None], p['hy_filt_w2'], p['hy_filt_b2'][None], p['hy_sin_freq'][None],
            p['hy_filt_w3'], p['hy_filt_b3'][None], env)
    in_specs = [c0(a) for a in args[:6]]
    in_specs += [pl.BlockSpec((hid, 2 * HY_CH), lambda o: (0, o)), pl.BlockSpec((1, 2 * HY_CH), lambda o: (0, o)),
                 c0(env)]
    return pl.pallas_call(
        _filt_body, grid=(n_order,), in_specs=in_specs,
        out_specs=pl.BlockSpec((L, 2 * HY_CH), lambda o: (0, o)),
        out_shape=jax.ShapeDtypeStruct((L, nout), F32),
        compiler_params=_cparams("parallel"), name="hy_filter",
    )(*args)


def _fft_plan(L):
    N = 2 * L
    Bn = 128 if N >= 4096 else 16
    A = N // Bn
    assert A * Bn == N and A % 16 == 0
    return A, Bn


def _dft_small(A, N):
    ka = np.arange(A)[:, None]
    a = np.arange(A)[None, :]
    th = 2.0 * np.pi * ((ka * a) % A) / A
    d1 = np.concatenate([np.cos(th), -np.sin(th)], axis=0)
    d4 = np.concatenate([np.cos(th.T), -np.sin(th.T)], axis=1) / N
    return jnp.asarray(d1, F32), jnp.asarray(d4, F32)


def _dft_mid(A, Bn):
    N = A * Bn
    ka = jnp.arange(A, dtype=jnp.int32)[:, None, None]
    kb = jnp.arange(Bn, dtype=jnp.int32)[None, :, None]
    b = jnp.arange(Bn, dtype=jnp.int32)[None, None, :]
    m = (b * (kb * A + ka)) % N
    ph = m.astype(F32) * (2.0 * math.pi / N)
    c, s = jnp.cos(ph), jnp.sin(ph)
    mf = jnp.concatenate([jnp.concatenate([c, s], axis=2), jnp.concatenate([-s, c], axis=2)], axis=1)
    ct, st = jnp.swapaxes(c, 1, 2), jnp.swapaxes(s, 1, 2)
    mi = jnp.concatenate([jnp.concatenate([ct, -st], axis=2), jnp.concatenate([st, ct], axis=2)], axis=1)
    return mf.astype(BF16), mi.astype(BF16)


FFT_BT = 16
FFT_GROUP = 8


def _fft1_body(d_ref, x_ref, o_ref, *, Bn):
    _, A2, bt, C = o_ref.shape
    Ain = d_ref.shape[1] // bt
    b0 = pl.multiple_of(pl.program_id(1) * bt, bt)
    slabs = [jnp.concatenate([x_ref[h, pl.ds(a * Bn + b0, bt), :] for h in range(HY_HALVES)], axis=1)
             for a in range(Ain)]
    xblk = jnp.concatenate(slabs, axis=0).astype(BF16)
    o_ref[0] = jnp.dot(d_ref[...], xblk, preferred_element_type=F32).reshape(A2, bt, C)


def _fft1(d1, x, n_seq, Bn):
    A2, Ain = d1.shape
    rows = Ain * Bn
    bt = min(FFT_BT, Bn)
    dbig = jnp.kron(d1, jnp.eye(bt, dtype=F32)).astype(BF16)
    return pl.pallas_call(
        functools.partial(_fft1_body, Bn=Bn), grid=(n_seq, Bn // bt),
        in_specs=[pl.BlockSpec(dbig.shape, lambda s, j: (0, 0)),
                  pl.BlockSpec((HY_HALVES, rows, LANES), lambda s, j: (0, s, 0))],
        out_specs=pl.BlockSpec((1, A2, bt, HY_CH), lambda s, j: (s, 0, j, 0)),
        out_shape=jax.ShapeDtypeStruct((n_seq, A2, Bn, HY_CH), F32),
        compiler_params=_cparams("parallel", "arbitrary"), name="hy_dft_slow",
    )(dbig, x)


def _fast_operand(p_ref, j):
    return jnp.concatenate([p_ref[0, 0, j], p_ref[0, 1, j]], axis=0).astype(BF16)


def _fft_spec_body(p_ref, mf_ref, k_ref, *, Bn):
    for j in range(FFT_GROUP):
        X = jnp.dot(mf_ref[j], _fast_operand(p_ref, j), preferred_element_type=F32)
        k_ref[0, j, 0] = X[:Bn]
        k_ref[0, j, 1] = X[Bn:]


def _fft_spec(p5, mf):
    S, _, A, Bn, C = p5.shape
    G = FFT_GROUP
    return pl.pallas_call(
        functools.partial(_fft_spec_body, Bn=Bn), grid=(A // G, S),
        in_specs=[pl.BlockSpec((1, 2, G, Bn, C), lambda g, s: (s, 0, g, 0, 0)),
                  pl.BlockSpec((G, 2 * Bn, 2 * Bn), lambda g, s: (g, 0, 0))],
        out_specs=pl.BlockSpec((1, G, 2, Bn, C), lambda g, s: (s, g, 0, 0, 0)),
        out_shape=jax.ShapeDtypeStruct((S, A, 2, Bn, C), F32),
        compiler_params=_cparams("parallel", "parallel"), name="hy_filter_spec",
    )(p5, mf)


def _fft_mid_body(p_ref, mf_ref, mi_ref, k_ref, q_ref, *, Bn):
    for j in range(FFT_GROUP):
        X = jnp.dot(mf_ref[j], _fast_operand(p_ref, j), preferred_element_type=F32)
        xr, xi = X[:Bn], X[Bn:]
        kr, ki = k_ref[j, 0], k_ref[j, 1]
        Y = jnp.concatenate([xr * kr - xi * ki, xr * ki + xi * kr], axis=0).astype(BF16)
        Q = jnp.dot(mi_ref[j], Y, preferred_element_type=F32)
        q_ref[0, j, 0] = Q[:Bn]
        q_ref[0, j, 1] = Q[Bn:]


def _fft_mid(p5, mf, mi, kspec):
    S, _, A, Bn, C = p5.shape
    G = FFT_GROUP
    mat = pl.BlockSpec((G, 2 * Bn, 2 * Bn), lambda g, s: (g, 0, 0))
    return pl.pallas_call(
        functools.partial(_fft_mid_body, Bn=Bn), grid=(A // G, S),
        in_specs=[pl.BlockSpec((1, 2, G, Bn, C), lambda g, s: (s, 0, g, 0, 0)), mat, mat,
                  pl.BlockSpec((G, 2, Bn, C), lambda g, s: (g, 0, 0, 0))],
        out_specs=pl.BlockSpec((1, G, 2, Bn, C), lambda g, s: (s, g, 0, 0, 0)),
        out_shape=jax.ShapeDtypeStruct((S, A, 2, Bn, C), F32),
        compiler_params=_cparams("parallel", "parallel"), name="hy_dft_fast",
    )(p5, mf, mi, kspec)


def _fft4_body(d_ref, q_ref, z_ref, g_ref, dch_ref, o_ref, *, Bn):
    _, A2, bt, C = q_ref.shape
    Ah = d_ref.shape[0] // bt
    b0 = pl.multiple_of(pl.program_id(1) * bt, bt)
    qblk = q_ref[0].reshape(A2 * bt, C).astype(BF16)
    y = jnp.dot(d_ref[...], qblk, preferred_element_type=F32)
    dch = dch_ref[...]
    for a in range(Ah):
        rows = pl.ds(a * Bn + b0, bt)
        ya = y[a * bt:(a + 1) * bt]
        for h in range(HY_HALVES):
            lanes = slice(h * LANES, (h + 1) * LANES)
            o_ref[h, rows, :] = g_ref[h, rows, :] * (ya[:, lanes] + z_ref[h, rows, :] * dch[:, lanes])


def _fft4(d4, q4, z, gate, dch):
    S, A2, Bn, C = q4.shape
    Ah = d4.shape[0]
    L = Ah * Bn
    bt = min(FFT_BT, Bn)
    dbig = jnp.kron(d4, jnp.eye(bt, dtype=F32)).astype(BF16)
    sig = pl.BlockSpec((HY_HALVES, L, LANES), lambda s, j: (0, s, 0))
    return pl.pallas_call(
        functools.partial(_fft4_body, Bn=Bn), grid=(S, Bn // bt),
        in_specs=[pl.BlockSpec(dbig.shape, lambda s, j: (0, 0)),
                  pl.BlockSpec((1, A2, bt, C), lambda s, j: (s, 0, j, 0)), sig, sig,
                  pl.BlockSpec((1, C), lambda s, j: (0, 0))],
        out_specs=sig,
        out_shape=jax.ShapeDtypeStruct((HY_HALVES, S * L, LANES), F32),
        compiler_params=_cparams("parallel", "arbitrary"), name="hy_idft_gate",
    )(dbig, q4, z, gate, dch)


def _hyena(v, x1, x2, n_seq, L, p):
    C = HY_CH
    A, Bn = _fft_plan(L)
    Ah = A // 2
    d1, d4 = _dft_small(A, A * Bn)
    mf, mi = _dft_mid(A, Bn)
    d4 = d4.reshape(A, 2, A).transpose(0, 2, 1).reshape(A, 2 * A)
    kn = _hy_filters(L, p).reshape(L, -1, 2, C)
    n_order = kn.shape[1]
    zero = jnp.zeros((1, C), F32)
    k2 = jnp.concatenate([piece for o in range(n_order)
                          for piece in (kn[:, o, 0], zero, jnp.flip(kn[1:, o, 1], axis=0))], axis=0)
    k2 = k2.reshape(-1, HY_HALVES, LANES).transpose(1, 0, 2)
    kspec = _fft_spec(_fft1(d1, k2, n_order, Bn).reshape(n_order, 2, A, Bn, C), mf)
    z = v
    for o, gate in enumerate((x1, x2)):
        P = _fft1(d1[:, :Ah], z, n_seq, Bn).reshape(n_seq, 2, A, Bn, C)
        Q = _fft_mid(P, mf, mi, kspec[o]).reshape(n_seq, 2 * A, Bn, C)
        z = _fft4(d4[:Ah], Q, z, gate, p['hy_bias_d'][o][None])
    return z


def _ml_prep_body(x_ref, p_ref, n_ref, w_ref, b_ref, q_ref, kt_ref, *, regions):
    at_start, at_end = _seq_edges(pl.program_id(0), regions)
    y = _conv3(x_ref[...], p_ref[SUBLANES - 1:SUBLANES, :], n_ref[0:1, :], w_ref[...], b_ref[...], at_start, at_end)
    y = y * jax.nn.sigmoid(y)
    q_ref[...] = y[:, :ML_W].astype(BF16)
    kt_ref[...] = (y[:, ML_W:] * (ML_HD ** -0.5)).T


def _ml_prep(u_ml, w, b, regions):
    M = u_ml.shape[0]
    tm = ROW_TILE
    n = sum(r[1] for r in regions)
    return pl.pallas_call(
        functools.partial(_ml_prep_body, regions=regions), grid=(n,),
        in_specs=_halo_specs(tm, 2 * ML_W, 0, regions, M) + [pl.BlockSpec((3, 2 * ML_W), lambda i: (0, 0)),
                                                             pl.BlockSpec((1, 2 * ML_W), lambda i: (0, 0))],
        out_specs=[pl.BlockSpec((tm, ML_W), lambda i: (_seq_tile(i, regions), 0)),
                   pl.BlockSpec((ML_W, tm), lambda i: (0, _seq_tile(i, regions)))],
        out_shape=[jax.ShapeDtypeStruct((M, ML_W), BF16), jax.ShapeDtypeStruct((ML_W, M), F32)],
        compiler_params=_cparams("parallel"), name="ml_prep",
    )(u_ml, u_ml, u_ml, w, b[None])


def _ml_body(*refs):
    ins, outs, scr = refs[:8], refs[8:10], refs[10:]

    @pl.when(pl.program_id(1) == 0)
    def _():
        for ref in scr:
            ref[...] = jnp.zeros_like(ref)

    for d in range(2):
        _ml_chunk(d, *ins[4 * d:4 * d + 4], outs[d], *scr[2 * d:2 * d + 2])


def _ml_chunk(d, q_ref, kt_ref, v_ref, g_ref, o_ref, st_ref, m_ref):
    T = q_ref.shape[0]
    H, W = ML_HEADS, ML_W
    WA = W + LANES
    sgn = 1 - 2 * d
    g = g_ref[...]
    gs = g[2 * H * d:2 * H * (d + 1)]
    ig = gs[0:H]
    lf8 = -(jnp.maximum(-gs, 0.0) + jnp.log1p(jnp.exp(-jnp.abs(gs))))
    lf = lf8[H:2 * H]
    r_i = lax.broadcasted_iota(jnp.int32, (T, T), 0)
    c_i = lax.broadcasted_iota(jnp.int32, (T, T), 1)
    prec = ((c_i - r_i) * sgn) <= 0
    incl = jnp.where(((r_i - c_i) * sgn) <= 0, 1.0, 0.0).astype(BF16)
    after = jnp.where(((c_i - r_i) * sgn) < 0, 1.0, 0.0).astype(BF16)
    b_rows = _dot2_exact_rhs(lf8, incl)[H:2 * H]
    bL = jnp.sum(lf, axis=1, keepdims=True)
    a_row = bL - b_rows + ig
    m_loc = jnp.max(a_row, axis=1, keepdims=True)
    w_row = jnp.exp(a_row - m_loc)
    m0 = m_ref[0:H, 0:1]
    m_new = jnp.maximum(bL + m0, m_loc)
    s_old = jnp.exp(bL + m0 - m_new)
    s_loc = jnp.exp(m_loc - m_new)

    qb = q_ref[...]
    kt = kt_ref[...]
    row_head = lax.broadcasted_iota(jnp.int32, (W, T), 0) // ML_HD
    lane = lax.broadcasted_iota(jnp.int32, (1, WA), 1)
    lane_head = jnp.where(lane < W, lane // ML_HD, lane - W)
    v_aug = jnp.concatenate([v_ref[...], jnp.ones((T, LANES), F32)], axis=1)

    Lfs = [jnp.where(prec, lf[h:h + 1], 0.0) for h in range(H)]
    E_all = _dot2_exact_rhs(jnp.concatenate(Lfs, axis=0), after)
    kth_all = jnp.concatenate([jnp.where(row_head == h, kt, 0.0) for h in range(H)], axis=1).astype(BF16)
    S_all = jnp.dot(qb, kth_all, preferred_element_type=F32)

    ps, vbd = [], []
    w_inter = jnp.zeros((T, WA), F32)
    e_m = jnp.zeros((T, WA), F32)
    for h in range(H):
        b_col = jnp.sum(Lfs[h], axis=1, keepdims=True)
        Dm = jnp.where(prec, E_all[h * T:(h + 1) * T] + ig[h:h + 1], -jnp.inf)
        inter = b_col + m0[h:h + 1]
        m_col = jnp.maximum(inter, jnp.max(Dm, axis=1, keepdims=True))
        S = S_all[:, h * T:(h + 1) * T]
        ps.append((jnp.exp(Dm - m_col) * S).astype(BF16))
        sel = lane_head == h
        vbd.append(jnp.where(sel, v_aug, 0.0).astype(BF16))
        w_inter = w_inter + jnp.where(sel, jnp.exp(inter - m_col), 0.0)
        e_m = e_m + jnp.where(sel, jnp.exp(-m_col), 0.0)
    nd = jnp.dot(jnp.concatenate(ps, axis=1), jnp.concatenate(vbd, axis=0), preferred_element_type=F32)
    nd = nd + w_inter * jnp.dot(qb, st_ref[...].astype(BF16), preferred_element_type=F32)
    den = jnp.zeros((T, W), F32)
    for h in range(H):
        den = den + jnp.where(lane_head[:, :W] == h, nd[:, W + h:W + h + 1], 0.0)
    o_ref[...] = nd[:, :W] / jnp.maximum(jnp.abs(den), e_m[:, :W])

    wk = jnp.zeros((W, T), F32)
    scol = jnp.zeros((1, WA), F32)
    ws = w_row * s_loc
    for h in range(H):
        wk = wk + jnp.where(row_head == h, ws[h:h + 1], 0.0)
        scol = scol + jnp.where(lane_head == h, s_old[h:h + 1], 0.0)
    st_loc = jnp.dot((kt * wk).astype(BF16), v_aug.astype(BF16), preferred_element_type=F32)
    diag = (lax.broadcasted_iota(jnp.int32, (W, WA), 0) // ML_HD) == lane_head
    st_ref[...] = jnp.where(diag, st_ref[...] * scol + st_loc, 0.0)
    m_ref[0:H, :] = jnp.broadcast_to(m_new, (H, LANES))


def _mlstm(q, kt, u_ml, g_t, B, L, Lc):
    M = q.shape[0]
    T = ML_CHUNK
    nC, nL = Lc // T, L // T
    lat0 = 0
    ctx0 = (B * L) // T

    def blk(d, b, i):
        cc = i if d == 0 else nC - 1 - i
        j = i - nC
        cl = j if d == 0 else nL - 1 - j
        return jnp.where(i < nC, ctx0 + b * nC + cc, lat0 + b * nL + cl)

    in_specs, out_specs = [], []
    for d in range(2):
        rows = functools.partial(lambda b, i, d, c: (blk(d, b, i), c), d=d)
        cols = functools.partial(lambda b, i, d: (0, blk(d, b, i)), d=d)
        in_specs += [pl.BlockSpec((T, ML_W), functools.partial(rows, c=0)), pl.BlockSpec((ML_W, T), cols),
                     pl.BlockSpec((T, ML_W), functools.partial(rows, c=2)), pl.BlockSpec((4 * ML_HEADS, T), cols)]
        out_specs.append(pl.BlockSpec((T, ML_W), functools.partial(rows, c=0)))
    state = [pltpu.VMEM((ML_W, ML_W + LANES), F32), pltpu.VMEM((SUBLANES, LANES), F32)]
    return pl.pallas_call(
        _ml_body, grid=(B, nC + nL), in_specs=in_specs, out_specs=out_specs,
        out_shape=[jax.ShapeDtypeStruct((M, ML_W), F32)] * 2,
        scratch_shapes=state * 2,
        compiler_params=_cparams("parallel", "arbitrary"), name="mlstm",
    )(*([q, kt, u_ml, g_t] * 2))


def _rope_tables(L):
    rows = L // GRID_W
    row = jnp.repeat(jnp.arange(rows, dtype=F32), GRID_W)
    col = jnp.tile(jnp.arange(GRID_W, dtype=F32), rows)
    half = DA_HD // 2
    inv = ROPE_THETA ** (-jnp.arange(0, half, 2, dtype=F32) / half)
    ang = jnp.stack([row, col], axis=-1)[:, :, None] * inv
    ang = jnp.stack([ang, ang], axis=-2).reshape(-1, DA_HD)
    ang = jnp.concatenate([ang, ang], axis=1)
    return jnp.cos(ang), jnp.sin(ang)


def _rope_qk(u, cos_ref, sin_ref, is_lat):
    reps = DA_QK // cos_ref.shape[1]
    cs = jnp.concatenate([cos_ref[...]] * reps, axis=1)
    sn = jnp.concatenate([sin_ref[...]] * reps, axis=1)
    q4 = DA_HD // 4
    src = lax.broadcasted_iota(jnp.int32, (LANES, LANES), 0)
    dst = lax.broadcasted_iota(jnp.int32, (LANES, LANES), 1)
    first = (dst % (2 * q4)) < q4
    perm = jnp.where(first & (src == dst + q4), -1.0, jnp.where((~first) & (src == dst - q4), 1.0, 0.0)).astype(BF16)

    def rope(x):
        rot = jnp.concatenate([_dot2_exact_rhs(x[:, g * LANES:(g + 1) * LANES], perm)
                               for g in range(x.shape[1] // LANES)], axis=1)
        return jnp.where(is_lat, x * cs + rot * sn, x)

    q = (rope(u[:, :DA_QK]) * (DA_HD ** -0.5 * math.log2(math.e))).astype(BF16)
    return q, rope(u[:, DA_QK:2 * DA_QK]).astype(BF16)


def _da_body(lam_ref, w_ref, q_ref, *rest, nseg, lam_init, sub):
    ks, vas, o_ref = rest[:nseg], rest[nseg:2 * nseg], rest[2 * nseg]
    lp = lam_ref[...]
    lam = (jnp.exp(jnp.sum(lp[0:1] * lp[1:2], axis=1, keepdims=True))
           - jnp.exp(jnp.sum(lp[2:3] * lp[3:4], axis=1, keepdims=True)) + lam_init)
    HW = 2 * DA_HD
    lane = lax.broadcasted_iota(jnp.int32, (1, HW), 1)
    n_sub = q_ref.shape[0] // sub
    def scores(i):
        t, m = divmod(i, 2)
        q = q_ref[t * sub:(t + 1) * sub, :]
        qm = jnp.where((lane // DA_HD) == m, q, jnp.zeros_like(q))
        return [lax.dot_general(k[...], qm, (((1,), (1,)), ((), ())), preferred_element_type=F32) for k in ks]

    def softmax(ss):
        mx = functools.reduce(jnp.maximum, [jnp.max(s, axis=0, keepdims=True) for s in ss])
        ps = [jnp.exp2(s - mx) for s in ss]
        den = functools.reduce(jnp.add, [jnp.sum(pr, axis=0, keepdims=True) for pr in ps])
        return [pr.astype(BF16) for pr in ps], den

    def values(ps, den):
        acc = functools.reduce(jnp.add, [jnp.dot(va[...], pr, preferred_element_type=F32)
                                         for pr, va in zip(ps, vas)])
        return acc / den

    n_streams = 2 * n_sub
    sss = {0: scores(0)}
    if n_streams > 1:
        sss[1] = scores(1)
    outs = []
    for i in range(n_streams):
        pd = softmax(sss.pop(i))
        if i + 2 < n_streams:
            sss[i + 2] = scores(i + 2)
        outs.append(values(*pd))
    for t in range(n_sub):
        o = outs[2 * t] - lam * outs[2 * t + 1]
        ms = jnp.mean(o * o, axis=0, keepdims=True)
        o_ref[t * sub:(t + 1) * sub, :] = ((o * lax.rsqrt(ms + NORM_EPS) * w_ref[...]) * (1.0 - lam_init)).T


def _diff_attn(qa, ka, vaa, lam_p, subln_w, lam_init, n_batch, q_rows0, q_len, segs):
    tq = min(ATT_Q_TILE, q_len)
    HW = 2 * DA_HD
    nq = q_len // tq
    q0 = q_rows0 // tq
    k_specs = [pl.BlockSpec((n, HW), functools.partial(lambda b, h, i, f, n: (f // n + b, h), f=f, n=n))
               for (f, n) in segs]
    va_specs = [pl.BlockSpec((HW, n), functools.partial(lambda b, h, i, f, n: (h, f // n + b), f=f, n=n))
                for (f, n) in segs]
    return pl.pallas_call(
        functools.partial(_da_body, nseg=len(segs), lam_init=lam_init, sub=min(ATT_SUB_TILE, tq)),
        grid=(n_batch, DA_HEADS, nq),
        in_specs=[pl.BlockSpec(lam_p.shape, lambda b, h, i: (0, 0)), pl.BlockSpec((HW, 1), lambda b, h, i: (0, 0)),
                  pl.BlockSpec((tq, HW), lambda b, h, i: (q0 + b * nq + i, h))] + k_specs + va_specs,
        out_specs=pl.BlockSpec((tq, HW), lambda b, h, i: (b * nq + i, h)),
        out_shape=jax.ShapeDtypeStruct((n_batch * q_len, DA_V), F32),
        compiler_params=_cparams("parallel", "parallel", "parallel"), name="diff_attn",
    )(lam_p, subln_w[:, None], qa, *([ka] * len(segs)), *([vaa] * len(segs)))


def _proj_out_body(x_ref, mod_ref, hy_ref, hf_ref, hb_ref, og_ref, da_ref, mw_ref, why_ref, wml_ref, wda_ref, o_ref):
    hs = hf_ref[...] + hb_ref[...]
    W = hs.shape[1]
    r = lax.broadcasted_iota(jnp.int32, (W, W), 0) // ML_HD
    c = lax.broadcasted_iota(jnp.int32, (W, W), 1) // ML_HD
    same_head = jnp.where(r == c, 1.0, 0.0).astype(BF16)
    ms = _dot2_exact_rhs(hs * hs, same_head) * (1.0 / ML_HD)
    y_ml = jax.nn.sigmoid(og_ref[...]) * (hs * lax.rsqrt(ms + NORM_EPS) * mw_ref[...])
    y_hy = jnp.concatenate([hy_ref[h] for h in range(HY_HALVES)], axis=1)
    y = (_dot(y_hy, why_ref[...]) + _dot(y_ml, wml_ref[...])) + _dot(da_ref[...], wda_ref[...])
    o_ref[...] = x_ref[...] + mod_ref[0, 2:3, :] * y


def _proj_out(xa, mod, y_hy, h_ml, u_ml, y_da, ml_norm_w, w_out, n_rows, n_lat_tiles, tiles_per_batch, n_batch):
    D = xa.shape[1]
    tm = ROW_TILE
    wb = w_out.astype(BF16)
    why, wml, wda = wb[:HY_CH], wb[HY_CH:HY_CH + ML_W], wb[HY_CH + ML_W:]
    full = lambda a: pl.BlockSpec(a.shape, lambda i: (0,) * a.ndim)
    row = lambda wd, cb=0: pl.BlockSpec((tm, wd), lambda i: (i, cb))
    mw = ml_norm_w[None]
    return pl.pallas_call(
        _proj_out_body, grid=(n_rows // tm,),
        in_specs=[row(D), pl.BlockSpec((1,) + mod.shape[1:],
                                       lambda i: (_mod_index(i, n_lat_tiles, tiles_per_batch, n_batch), 0, 0)),
                  pl.BlockSpec((HY_HALVES, tm, LANES), lambda i: (0, i, 0)),
                  row(ML_W), row(ML_W), row(ML_W, 3), row(DA_V),
                  full(mw), full(why), full(wml), full(wda)],
        out_specs=row(D),
        out_shape=jax.ShapeDtypeStruct((n_rows, D), F32),
        compiler_params=_cparams("parallel"), name="proj_out",
    )(xa, mod, y_hy, *h_ml, u_ml, y_da, mw, why, wml, wda)


ROUTE_LANE0 = MOE_GROUPS


def _router_body(x_ref, mod_ref, nw_ref, wr_ref, br_ref, h_ref, ri_ref, rf_ref, cnt_ref, run_ref):
    @pl.when(pl.program_id(0) == 0)
    def _():
        run_ref[...] = jnp.zeros_like(run_ref)

    h = _norm_mod(x_ref[...], nw_ref[...], mod_ref[0, 3:4, :], mod_ref[0, 4:5, :])
    h_ref[...] = h
    lg = _dot3(h, wr_ref[...]) + br_ref[...]
    tm = lg.shape[0]
    lane = lax.broadcasted_iota(jnp.int32, lg.shape, 1)
    neg = -jnp.inf
    is_g = lane < MOE_GROUPS
    gl = jnp.where(is_g, lg, neg)
    gmax = jnp.max(gl, axis=1, keepdims=True)
    gidx = jnp.min(jnp.where(gl == gmax, lane, LANES), axis=1, keepdims=True)
    gw = 1.0 / jnp.sum(jnp.where(is_g, jnp.exp(gl - gmax), 0.0), axis=1, keepdims=True)
    e_of = lane - ROUTE_LANE0
    in_grp = (e_of >= 0) & (e_of < MOE_EXPERTS) & ((e_of // MOE_PER_GROUP) == gidx)
    el = jnp.where(in_grp, lg, neg)
    t1 = jnp.max(el, axis=1, keepdims=True)
    i1 = jnp.min(jnp.where(el == t1, lane, LANES), axis=1, keepdims=True)
    el2 = jnp.where(lane == i1, neg, el)
    t2 = jnp.max(el2, axis=1, keepdims=True)
    i2 = jnp.min(jnp.where(el2 == t2, lane, LANES), axis=1, keepdims=True)
    ex = jnp.exp(t2 - t1)
    g1 = gw / (1.0 + ex)
    g2 = gw * ex / (1.0 + ex)
    oh = jnp.where((lane == i1) | (lane == i2), 1.0, 0.0)
    r_i = lax.broadcasted_iota(jnp.int32, (tm, tm), 0)
    c_i = lax.broadcasted_iota(jnp.int32, (tm, tm), 1)
    earlier = jnp.where(c_i < r_i, 1.0, 0.0).astype(BF16)
    cum = jnp.dot(earlier, oh.astype(BF16), preferred_element_type=F32) + run_ref[0:1, :]
    r1 = jnp.sum(jnp.where(lane == i1, cum, 0.0), axis=1, keepdims=True).astype(jnp.int32)
    r2 = jnp.sum(jnp.where(lane == i2, cum, 0.0), axis=1, keepdims=True).astype(jnp.int32)
    run = run_ref[0:1, :] + jnp.sum(oh, axis=0, keepdims=True)
    run_ref[...] = jnp.broadcast_to(run, run_ref.shape)
    cnt_ref[...] = jnp.broadcast_to(run, cnt_ref.shape)
    zi = jnp.zeros_like(lane)
    ri_ref[...] = jnp.where(lane == 0, i1 - ROUTE_LANE0, jnp.where(lane == 1, i2 - ROUTE_LANE0,
                            jnp.where(lane == 2, r1, jnp.where(lane == 3, r2, zi))))
    rf_ref[...] = jnp.where(lane == 0, g1, jnp.where(lane == 1, g2, 0.0))


def _router(xa, mod, nw, wg, bg, we, be, n_rows, n_lat_tiles, tiles_per_batch, n_batch):
    D = xa.shape[1]
    tm = ROW_TILE
    pad = LANES - MOE_GROUPS - MOE_EXPERTS
    wr = jnp.concatenate([wg, we, jnp.zeros((D, pad), F32)], axis=1)
    br = jnp.concatenate([bg, be, jnp.zeros((pad,), F32)])[None]
    row = lambda wd: pl.BlockSpec((tm, wd), lambda i: (i, 0))
    full = lambda a: pl.BlockSpec(a.shape, lambda i: (0,) * a.ndim)
    return pl.pallas_call(
        _router_body, grid=(n_rows // tm,),
        in_specs=[row(D), pl.BlockSpec((1,) + mod.shape[1:],
                                       lambda i: (_mod_index(i, n_lat_tiles, tiles_per_batch, n_batch), 0, 0)),
                  full(nw[None]), full(wr), full(br)],
        out_specs=[row(D), row(LANES), row(LANES), pl.BlockSpec((SUBLANES, LANES), lambda i: (0, 0))],
        out_shape=[jax.ShapeDtypeStruct((n_rows, D), F32), jax.ShapeDtypeStruct((n_rows, LANES), jnp.int32),
                   jax.ShapeDtypeStruct((n_rows, LANES), F32), jax.ShapeDtypeStruct((SUBLANES, LANES), F32)],
        scratch_shapes=[pltpu.VMEM((SUBLANES, LANES), F32)],
        compiler_params=_cparams("arbitrary"), name="moe_router",
    )(xa, mod, nw[None], wr, br)


def _plan_body(ri_ref, base_ref, d_ref):
    ri = ri_ref[...]
    lane = lax.broadcasted_iota(jnp.int32, ri.shape, 1)
    base = base_ref[...]
    d = []
    for j in range(2):
        b = jnp.sum(jnp.where(lane == ri[:, j:j + 1], base, 0.0), axis=1, keepdims=True)
        d.append(b + ri[:, 2 + j:3 + j].astype(F32))
    slots = jnp.where(lane == 0, d[0], jnp.where(lane == 1, d[1], 0.0))
    d_ref[0] = slots.T[0:SUBLANES, :].astype(jnp.int32)


def _plan(ri, base):
    n_rows = ri.shape[0]
    tm = ROW_TILE
    basef = jnp.pad(base.astype(F32), (0, LANES - base.shape[0]))[None]
    out = pl.pallas_call(
        _plan_body, grid=(n_rows // tm,),
        in_specs=[pl.BlockSpec((tm, LANES), lambda i: (i, 0)), pl.BlockSpec((1, LANES), lambda i: (0, 0))],
        out_specs=pl.BlockSpec((1, SUBLANES, tm), lambda i: (i, 0, 0)),
        out_shape=jax.ShapeDtypeStruct((n_rows // tm, SUBLANES, tm), jnp.int32),
        compiler_params=_cparams("parallel"), name="moe_plan",
    )(ri, basef)
    return out[:, 0, :], out[:, 1, :]


def _dispatch_body(d0_ref, d1_ref, lb_ref, nb_ref, h_ref, xb_ref, zbuf, sem, zsem):
    i = pl.program_id(0)
    tm = h_ref.shape[0]
    TB = zbuf.shape[0]
    n_blk = xb_ref.shape[0] // TB

    def zero_copy(blk):
        return pltpu.make_async_copy(zbuf, xb_ref.at[pl.ds(pl.multiple_of(blk * TB, TB), TB)], zsem)

    @pl.when(i == 0)
    def _():
        zbuf[...] = jnp.zeros_like(zbuf)
        for phase in ("start", "wait"):
            def tail(blk, carry, phase=phase):
                getattr(zero_copy(blk), phase)()
                return carry

            for e in range(lb_ref.shape[0]):
                @pl.when(lb_ref[e] >= 0)
                def _(e=e, phase=phase):
                    getattr(zero_copy(lb_ref[e]), phase)()
            lax.fori_loop(nb_ref[0], n_blk, tail, 0)

    for r in range(tm):
        for d_ref in (d0_ref, d1_ref):
            pltpu.make_async_copy(h_ref.at[pl.ds(r, 1)], xb_ref.at[pl.ds(d_ref[i, r], 1)], sem).start()
    pltpu.make_async_copy(xb_ref.at[pl.ds(0, 2 * tm)], xb_ref.at[pl.ds(0, 2 * tm)], sem).wait()


def _dispatch(d0, d1, last_blk, n_used, h2, n_slot_rows):
    n_rows, D = h2.shape
    tm = ROW_TILE
    return pl.pallas_call(
        _dispatch_body,
        grid_spec=pltpu.PrefetchScalarGridSpec(
            num_scalar_prefetch=4, grid=(n_rows // tm,),
            in_specs=[pl.BlockSpec((tm, D), lambda i, *_: (i, 0))],
            out_specs=pl.BlockSpec(memory_space=pl.ANY),
            scratch_shapes=[pltpu.VMEM((MOE_ROWS, D), F32), pltpu.SemaphoreType.DMA(()),
                            pltpu.SemaphoreType.DMA(())]),
        out_shape=jax.ShapeDtypeStruct((n_slot_rows, D), F32),
        compiler_params=pltpu.CompilerParams(dimension_semantics=("arbitrary",), vmem_limit_bytes=VMEM_LIMIT,
                                             disable_bounds_checks=True),
        name="moe_dispatch",
    )(d0, d1, last_blk, n_used, h2)


def _ffn_body(be_ref, nb_ref, x_ref, w1_ref, w3_ref, w2_ref, y_ref, w1b, w3b, w2b):
    i = pl.program_id(0)
    used = i < nb_ref[0]
    new_expert = (i == 0) | (be_ref[i] != be_ref[jnp.maximum(i - 1, 0)])

    @pl.when(used & new_expert)
    def _():
        w1b[...] = w1_ref[0, 0].astype(BF16)
        w3b[...] = w3_ref[0, 0].astype(BF16)
        w2b[...] = w2_ref[0, 0].astype(BF16)

    @pl.when(used)
    def _():
        xb = x_ref[...].astype(BF16)
        a = jnp.dot(xb, w1b[...], preferred_element_type=F32)
        b = jnp.dot(xb, w3b[...], preferred_element_type=F32)
        hmid = ((a * jax.nn.sigmoid(a)) * b).astype(BF16)
        y_ref[...] = jnp.dot(hmid, w2b[...], preferred_element_type=F32)

    @pl.when(jnp.logical_not(used))
    def _():
        y_ref[...] = jnp.zeros_like(y_ref)


def _expert_ffn(blk_e, n_used, xb, w1, w3, w2, layer):
    P, D = xb.shape
    F = w1.shape[3]
    TB = MOE_ROWS
    rows = pl.BlockSpec((TB, D), lambda i, be, nb: (i, 0))
    return pl.pallas_call(
        _ffn_body,
        grid_spec=pltpu.PrefetchScalarGridSpec(
            num_scalar_prefetch=2, grid=(P // TB,),
            in_specs=[rows, pl.BlockSpec((1, 1, D, F), lambda i, be, nb: (layer, be[i], 0, 0)),
                      pl.BlockSpec((1, 1, D, F), lambda i, be, nb: (layer, be[i], 0, 0)),
                      pl.BlockSpec((1, 1, F, D), lambda i, be, nb: (layer, be[i], 0, 0))],
            out_specs=rows,
            scratch_shapes=[pltpu.VMEM((D, F), BF16), pltpu.VMEM((D, F), BF16), pltpu.VMEM((F, D), BF16)]),
        out_shape=jax.ShapeDtypeStruct((P, D), F32),
        compiler_params=_cparams("arbitrary"), name="moe_ffn",
    )(blk_e, n_used, xb, w1, w3, w2)


def _combine_body(d0_ref, d1_ref, x_ref, mod_ref, rf_ref, yb_ref, fw_ref, o_ref, buf, sem, *, final):
    i = pl.program_id(0)
    n = pl.num_programs(0)
    tm = x_ref.shape[0]

    def gather(step, slot):
        for r in range(tm):
            for j, d_ref in enumerate((d0_ref, d1_ref)):
                pltpu.make_async_copy(yb_ref.at[pl.ds(d_ref[step, r], 1)], buf.at[slot, j, pl.ds(r, 1)],
                                      sem.at[slot]).start()

    @pl.when(i == 0)
    def _():
        gather(0, 0)

    slot = i % 2
    pltpu.make_async_copy(buf.at[slot], buf.at[slot], sem.at[slot]).wait()

    @pl.when(i + 1 < n)
    def _():
        gather(i + 1, 1 - slot)

    g = rf_ref[...]
    f = g[:, 0:1] * buf[slot, 0] + g[:, 1:2] * buf[slot, 1]
    xn = x_ref[...] + mod_ref[0, 5:6, :] * f
    if final:
        ms = jnp.mean(xn * xn, axis=-1, keepdims=True)
        xn = xn * lax.rsqrt(ms + NORM_EPS) * fw_ref[...]
    o_ref[...] = xn


def _combine(d0, d1, xa, mod, rf, yb, final_w, final, n_rows, n_lat_tiles, tiles_per_batch, n_batch):
    D = xa.shape[1]
    tm = ROW_TILE
    row = lambda wd: pl.BlockSpec((tm, wd), lambda i, *_: (i, 0))
    return pl.pallas_call(
        functools.partial(_combine_body, final=final),
        grid_spec=pltpu.PrefetchScalarGridSpec(
            num_scalar_prefetch=2, grid=(n_rows // tm,),
            in_specs=[row(D), pl.BlockSpec((1,) + mod.shape[1:],
                                           lambda i, *_: (_mod_index(i, n_lat_tiles, tiles_per_batch, n_batch), 0, 0)),
                      row(LANES), pl.BlockSpec(memory_space=pl.ANY), pl.BlockSpec((1, D), lambda i, *_: (0, 0))],
            out_specs=row(D),
            scratch_shapes=[pltpu.VMEM((2, 2, tm, D), F32), pltpu.SemaphoreType.DMA((2,))]),
        out_shape=jax.ShapeDtypeStruct((n_rows, D), F32),
        compiler_params=pltpu.CompilerParams(dimension_semantics=("arbitrary",), vmem_limit_bytes=VMEM_LIMIT,
                                             disable_bounds_checks=True),
        name="moe_combine",
    )(d0, d1, xa, mod, rf, yb, final_w[None])


def _moe(xa, mod, p, expert_w, layer, final_w, final, n_rows, n_lat_tiles, tiles_per_batch, n_batch):
    TB = MOE_ROWS
    h2, ri, rf, cnt = _router(xa, mod, p['norm2_w'], p['moe_wg'], p['moe_bg'], p['moe_we'], p['moe_be'],
                              n_rows, n_lat_tiles, tiles_per_batch, n_batch)
    counts = cnt[0, ROUTE_LANE0:ROUTE_LANE0 + MOE_EXPERTS].astype(jnp.int32)
    pc = (counts + TB - 1) // TB * TB
    pend = jnp.cumsum(pc)
    base = pend - pc
    n_blk = -(-2 * n_rows // TB) + MOE_EXPERTS
    n_used = (pend[-1] // TB).astype(jnp.int32)
    blk = jnp.arange(n_blk, dtype=jnp.int32)
    blk_e = jnp.sum((pend[None, :] <= (jnp.minimum(blk, n_used - 1) * TB)[:, None]).astype(jnp.int32), axis=1)
    blk_e = jnp.minimum(blk_e, MOE_EXPERTS - 1).astype(jnp.int32)
    last_blk = jnp.where(pc > 0, pend // TB - 1, -1).astype(jnp.int32)
    d0, d1 = _plan(ri, base)
    xb = _dispatch(d0, d1, last_blk, n_used[None], h2, n_blk * TB)
    yb = _expert_ffn(blk_e, n_used[None], xb, *expert_w, layer)
    return _combine(d0, d1, xa, mod, rf, yb, final_w, final, n_rows, n_lat_tiles, tiles_per_batch, n_batch)


_LAYER_KEYS = ('ada_w', 'ada_b', 'norm1_w', 'norm2_w', 'w_in', 'b_in', 'w_out', 'hy_conv_w', 'hy_conv_b',
               'hy_filt_w1', 'hy_filt_b1', 'hy_filt_w2', 'hy_filt_b2', 'hy_filt_w3', 'hy_filt_b3', 'hy_sin_freq',
               'hy_bias_d', 'ml_conv_w', 'ml_conv_b', 'ml_norm_w', 'da_lambda', 'da_subln_w', 'moe_wg', 'moe_bg',
               'moe_we', 'moe_be', 'moe_w1', 'moe_w3', 'moe_w2')


def kernel(x, c, ctx, c_ctx, ada_w, ada_b, norm1_w, norm2_w, w_in, b_in, w_out, hy_conv_w, hy_conv_b, hy_filt_w1,
           hy_filt_b1, hy_filt_w2, hy_filt_b2, hy_filt_w3, hy_filt_b3, hy_sin_freq, hy_bias_d, ml_conv_w, ml_conv_b,
           ml_norm_w, da_lambda, da_subln_w, moe_wg, moe_bg, moe_we, moe_be, moe_w1, moe_w3, moe_w2, final_norm_w):
    stacked = dict(zip(_LAYER_KEYS, (ada_w, ada_b, norm1_w, norm2_w, w_in, b_in, w_out, hy_conv_w, hy_conv_b,
                                     hy_filt_w1, hy_filt_b1, hy_filt_w2, hy_filt_b2, hy_filt_w3, hy_filt_b3,
                                     hy_sin_freq, hy_bias_d, ml_conv_w, ml_conv_b, ml_norm_w, da_lambda, da_subln_w,
                                     moe_wg, moe_bg, moe_we, moe_be, moe_w1, moe_w3, moe_w2)))
    B, L, D = x.shape
    Lc = ctx.shape[1]
    depth = ada_w.shape[0]
    tm = ROW_TILE
    assert L % tm == 0 and Lc % tm == 0 and L % GRID_W == 0 and L % ML_CHUNK == 0 and Lc % ML_CHUNK == 0
    ML, MC = B * L, B * Lc
    M = ML + MC
    n_lat, n_ctx = ML // tm, MC // tm
    tpb = L // tm
    regions = ((0, n_lat, tpb), (n_lat, n_ctx, Lc // tm))

    xa = jnp.concatenate([x.reshape(ML, D), ctx.reshape(MC, D)], axis=0)
    R = -(-(B + 1) // SUBLANES) * SUBLANES
    cond = jnp.concatenate([c, c_ctx[None], jnp.zeros((R - B - 1, D), F32)], axis=0)
    cos, sin = _rope_tables(L)

    for l in range(depth):
        last = l == depth - 1
        p = {k: v[l] for k, v in stacked.items() if k not in ('moe_w1', 'moe_w3', 'moe_w2')}
        lam_init = 0.8 - 0.6 * math.exp(-0.3 * l)
        mod = _ada(cond, p['ada_w'], p['ada_b']).reshape(R, 6, D)
        u_hy, u_ml, qa, ka, va, g_t = _proj_in(xa, mod, p['norm1_w'], p['w_in'], p['b_in'], cos, sin, n_lat, tpb, B)

        y_hy = _hyena(*_hy_prep(u_hy, p['hy_conv_w'], p['hy_conv_b'], 0, n_lat, tpb), B, L, p)
        q_m, k_t = _ml_prep(u_ml, p['ml_conv_w'], p['ml_conv_b'], regions)
        h_ml = _mlstm(q_m, k_t, u_ml, g_t, B, L, Lc)
        y_da = _diff_attn(qa, ka, va, p['da_lambda'], p['da_subln_w'], lam_init, B, 0, L, ((0, L), (ML, Lc)))
        n_rows = ML if last else M
        if not last:
            yc_hy = _hyena(*_hy_prep(u_hy, p['hy_conv_w'], p['hy_conv_b'], n_lat, n_ctx, Lc // tm), B, Lc, p)
            yc_da = _diff_attn(qa, ka, va, p['da_lambda'], p['da_subln_w'], lam_init, B, ML, Lc, ((ML, Lc),))
            y_hy = jnp.concatenate([y_hy, yc_hy], axis=1)
            y_da = jnp.concatenate([y_da, yc_da], axis=0)
        xa = _proj_out(xa, mod, y_hy, h_ml, u_ml, y_da, p['ml_norm_w'], p['w_out'], n_rows, n_lat, tpb, B)
        xa = _moe(xa, mod, p, (moe_w1, moe_w3, moe_w2), l, final_norm_w, last, n_rows, n_lat, tpb, B)
    return xa.reshape(B, L, D)
```
